```python
import math
import jax, jax.numpy as jnp
from jax import lax
import numpy as np

D_MODEL = 1024
BATCH = 8
SEQ = 4096
DEPTH = 1

MIX_WIDTH = D_MODEL
HEAD_DIM = 64
SWA_Q_HEADS = 8
SWA_KV_HEADS = 2
SWA_GROUP = SWA_Q_HEADS // SWA_KV_HEADS
WINDOW = 128
BLOCK = 128
DIFF_HEADS = 4
DIFF_VDIM = 2 * HEAD_DIM
Q_BLOCK = 128
SWA_Q_COLS = SWA_Q_HEADS * HEAD_DIM
SWA_KV_COLS = SWA_KV_HEADS * HEAD_DIM
DIFF_QK_COLS = DIFF_HEADS * 2 * HEAD_DIM
DIFF_V_COLS = DIFF_HEADS * DIFF_VDIM
IN_COLS = SWA_Q_COLS + 2 * SWA_KV_COLS + 2 * DIFF_QK_COLS + DIFF_V_COLS
SWA_OUT = SWA_Q_HEADS * HEAD_DIM
DIFF_OUT = DIFF_HEADS * DIFF_VDIM
MEM_LEN = 256
CROSS_HEADS = 4
CROSS_HEAD_DIM = D_MODEL // CROSS_HEADS
D_FF = 4 * D_MODEL
ROPE_THETA = 10000.0
NORM_EPS = 1e-5

kernel_name = "hymba_swa_sink_diffattn_xattn_sqrelu"


def rms_norm(x, g, eps=NORM_EPS):
    xf = x.astype(jnp.float32)
    y = xf * lax.rsqrt(jnp.mean(xf * xf, axis=-1, keepdims=True) + eps)
    return (y * g.astype(jnp.float32)).astype(x.dtype)


def rope_tables(positions, dim):
    inv_freq = ROPE_THETA ** (-jnp.arange(0, dim, 2, dtype=jnp.float32) / dim)
    ang = positions.astype(jnp.float32)[..., None] * inv_freq
    return jnp.cos(ang), jnp.sin(ang)


def apply_rope(x, cos, sin):
    shp = cos.shape[:2] + (1,) * (x.ndim - 3) + cos.shape[-1:]
    c = cos.reshape(shp).astype(x.dtype)
    s = sin.reshape(shp).astype(x.dtype)
    x1, x2 = jnp.split(x, 2, axis=-1)
    return jnp.concatenate([x1 * c - x2 * s, x2 * c + x1 * s], axis=-1)


def sliding_window_sink_attention(q, k, v, sinks):
    Bn, S = q.shape[0], q.shape[1]
    nb = S // BLOCK
    scale = HEAD_DIM ** -0.5
    qb = q.reshape(Bn, nb, BLOCK, SWA_KV_HEADS, SWA_GROUP, HEAD_DIM)
    kb = k.reshape(Bn, nb, BLOCK, SWA_KV_HEADS, HEAD_DIM)
    vb = v.reshape(Bn, nb, BLOCK, SWA_KV_HEADS, HEAD_DIM)
    pad = ((0, 0), (1, 0), (0, 0), (0, 0), (0, 0))
    kw = jnp.concatenate([jnp.pad(kb, pad)[:, :-1], kb], axis=2)
    vw = jnp.concatenate([jnp.pad(vb, pad)[:, :-1], vb], axis=2)
    s = jnp.einsum('bnqhgd,bnkhd->bnhgqk', qb, kw).astype(jnp.float32) * scale
    qi = jnp.arange(BLOCK)[:, None]
    kj = jnp.arange(2 * BLOCK)[None, :]
    rel = qi + BLOCK - kj
    band = (rel >= 0) & (rel < WINDOW)
    exists = (jnp.arange(nb)[:, None, None] * BLOCK + kj[None]) >= BLOCK
    valid = band[None] & exists
    s = jnp.where(valid[None, :, None, None], s, -jnp.inf)
    sink = sinks.astype(jnp.float32).reshape(SWA_KV_HEADS, SWA_GROUP)[None, None, :, :, None, None]
    m = jnp.maximum(jnp.max(s, axis=-1, keepdims=True), sink)
    e = jnp.exp(s - m)
    p = e / (jnp.sum(e, axis=-1, keepdims=True) + jnp.exp(sink - m))
    o = jnp.einsum('bnhgqk,bnkhd->bnqhgd', p.astype(v.dtype), vw)
    return o.reshape(Bn, S, SWA_OUT)


def differential_attention(q, k, v, lam):
    Bn, S = q.shape[0], q.shape[1]
    nb = S // Q_BLOCK
    scale = HEAD_DIM ** -0.5
    qb = q.reshape(Bn, nb, Q_BLOCK, DIFF_HEADS, 2, HEAD_DIM).swapaxes(0, 1)
    starts = jnp.arange(nb) * Q_BLOCK
    kpos = jnp.arange(S)

    def one_block(args):
        qblk, start = args
        s = jnp.einsum('bqhcd,bkhcd->bhcqk', qblk, k).astype(jnp.float32) * scale
        mask = kpos[None, :] <= (start + jnp.arange(Q_BLOCK))[:, None]
        s = jnp.where(mask, s, -jnp.inf)
        p = jax.nn.softmax(s, axis=-1)
        a = p[:, :, 0] - lam * p[:, :, 1]
        return jnp.einsum('bhqk,bkhe->bqhe', a.astype(v.dtype), v)

    o = lax.map(one_block, (qb, starts))
    return o.swapaxes(0, 1).reshape(Bn, S, DIFF_HEADS, DIFF_VDIM)


def cross_attention(h, m, w_q, w_kv, w_o):
    Bn, S = h.shape[0], h.shape[1]
    q = (h @ w_q).reshape(Bn, S, CROSS_HEADS, CROSS_HEAD_DIM)
    kv = (m @ w_kv).reshape(Bn, m.shape[1], 2, CROSS_HEADS, CROSS_HEAD_DIM)
    k, v = kv[:, :, 0], kv[:, :, 1]
    s = jnp.einsum('bshd,bmhd->bhsm', q, k).astype(jnp.float32) * CROSS_HEAD_DIM ** -0.5
    p = jax.nn.softmax(s, axis=-1)
    o = jnp.einsum('bhsm,bmhd->bshd', p.astype(v.dtype), v).reshape(Bn, S, D_MODEL)
    return o @ w_o


def setup_inputs(seed: int = 0) -> dict:
    key = jax.random.key(seed)
    ks = jax.random.split(key, 24)
    f32 = jnp.float32
    L = DEPTH

    def w(k, shape, fan_in):
        return jax.random.normal(k, shape, f32) * fan_in ** -0.5

    def gain(k, shape):
        return 1.0 + 0.02 * jax.random.normal(k, shape, f32)

    x = jax.random.normal(ks[0], (BATCH, SEQ, D_MODEL), f32)
    mem = jax.random.normal(ks[1], (BATCH, MEM_LEN, D_MODEL), f32)
    positions = (jnp.arange(SEQ, dtype=jnp.int32)[None, :]
                 + jax.random.randint(ks[2], (BATCH, 1), 0, 1024, dtype=jnp.int32))
    return {
        "x": x,
        "mem": mem,
        "positions": positions,
        "g_mix": gain(ks[3], (L, D_MODEL)),
        "w_in": w(ks[4], (L, D_MODEL, IN_COLS), D_MODEL),
        "sinks": 0.5 * jax.random.normal(ks[5], (L, SWA_Q_HEADS), f32),
        "lambda_q1": 0.1 * jax.random.normal(ks[6], (L, HEAD_DIM), f32),
        "lambda_k1": 0.1 * jax.random.normal(ks[7], (L, HEAD_DIM), f32),
        "lambda_q2": 0.1 * jax.random.normal(ks[8], (L, HEAD_DIM), f32),
        "lambda_k2": 0.1 * jax.random.normal(ks[9], (L, HEAD_DIM), f32),
        "g_diff": gain(ks[10], (L, DIFF_VDIM)),
        "w_out": w(ks[11], (L, MIX_WIDTH, D_MODEL), MIX_WIDTH),
        "g_cross": gain(ks[12], (L, D_MODEL)),
        "g_mem": gain(ks[13], (L, D_MODEL)),
        "w_cq": w(ks[14], (L, D_MODEL, D_MODEL), D_MODEL),
        "w_ckv": w(ks[15], (L, D_MODEL, 2 * D_MODEL), D_MODEL),
        "w_co": w(ks[16], (L, D_MODEL, D_MODEL), D_MODEL),
        "g_mlp": gain(ks[17], (L, D_MODEL)),
        "w_up": w(ks[18], (L, D_MODEL, D_FF), D_MODEL),
        "w_down": w(ks[19], (L, D_FF, D_MODEL), D_FF),
        "g_final": gain(ks[20], (D_MODEL,)),
    }


def reference(x, mem, positions, g_mix, w_in, sinks, lambda_q1, lambda_k1, lambda_q2, lambda_k2,
              g_diff, w_out, g_cross, g_mem, w_cq, w_ckv, w_co, g_mlp, w_up, w_down, g_final):
    Bn, S = x.shape[0], x.shape[1]
    cos, sin = rope_tables(positions, HEAD_DIM)
    splits = np.cumsum([SWA_Q_COLS, SWA_KV_COLS, SWA_KV_COLS, DIFF_QK_COLS, DIFF_QK_COLS])
    for l in range(DEPTH):
        h = rms_norm(x, g_mix[l])
        proj = h @ w_in[l]
        qa, ka, va, qd, kd, vd = jnp.split(proj, splits, axis=-1)
        qa = apply_rope(qa.reshape(Bn, S, SWA_KV_HEADS, SWA_GROUP, HEAD_DIM), cos, sin)
        ka = apply_rope(ka.reshape(Bn, S, SWA_KV_HEADS, HEAD_DIM), cos, sin)
        va = va.reshape(Bn, S, SWA_KV_HEADS, HEAD_DIM)
        out_a = sliding_window_sink_attention(qa, ka, va, sinks[l])

        lam_init = 0.8 - 0.6 * math.exp(-0.3 * l)
        lam = (jnp.exp(jnp.sum(lambda_q1[l].astype(jnp.float32) * lambda_k1[l].astype(jnp.float32)))
               - jnp.exp(jnp.sum(lambda_q2[l].astype(jnp.float32) * lambda_k2[l].astype(jnp.float32)))
               + lam_init)
        qd = apply_rope(qd.reshape(Bn, S, DIFF_HEADS, 2, HEAD_DIM), cos, sin)
        kd = apply_rope(kd.reshape(Bn, S, DIFF_HEADS, 2, HEAD_DIM), cos, sin)
        vd = vd.reshape(Bn, S, DIFF_HEADS, DIFF_VDIM)
        od = differential_attention(qd, kd, vd, lam)
        out_b = (rms_norm(od, g_diff[l]) * (1.0 - lam_init)).reshape(Bn, S, DIFF_OUT)

        x = x + jnp.concatenate([out_a, out_b], axis=-1) @ w_out[l]

        x = x + cross_attention(rms_norm(x, g_cross[l]), rms_norm(mem, g_mem[l]),
                                w_cq[l], w_ckv[l], w_co[l])

        u = rms_norm(x, g_mlp[l]) @ w_up[l]
        x = x + jnp.square(jax.nn.relu(u)) @ w_down[l]
    return rms_norm(x, g_final)
```

```python
import functools
import math

import jax
import jax.numpy as jnp
from jax import lax
from jax.experimental import pallas as pl
from jax.experimental.pallas import tpu as pltpu

D_MODEL = 1024
HEAD_DIM = 64
SWA_Q_HEADS = 8
SWA_KV_HEADS = 2
SWA_GROUP = SWA_Q_HEADS // SWA_KV_HEADS
WINDOW = 128
DIFF_HEADS = 4
DIFF_VDIM = 2 * HEAD_DIM
SWA_Q_COLS = SWA_Q_HEADS * HEAD_DIM
SWA_KV_COLS = SWA_KV_HEADS * HEAD_DIM
DIFF_QK_COLS = DIFF_HEADS * 2 * HEAD_DIM
DIFF_V_COLS = DIFF_HEADS * DIFF_VDIM
CROSS_HEADS = 4
CROSS_HEAD_DIM = D_MODEL // CROSS_HEADS
ROPE_THETA = 10000.0
NORM_EPS = 1e-5

LANES = 128
TOK_TILE = 512
DIFF_TILE = 512
FF_CHUNK = 512
VMEM_LIMIT = 56 * 1024 * 1024
NEG_BIG = -1e30

BF16 = jnp.bfloat16
F32 = jnp.float32


def _cparams(n_axes):
    return pltpu.CompilerParams(dimension_semantics=("arbitrary",) * n_axes,
                                vmem_limit_bytes=VMEM_LIMIT)


def _rms(x, g):
    ms = jnp.mean(x * x, axis=-1, keepdims=True)
    return x * lax.rsqrt(ms + NORM_EPS) * g


def _const_spec(shape):
    return pl.BlockSpec(shape, lambda *_: (0,) * len(shape))


def _rope_table_kernel(pos_ref, invf_ref, cos_ref, sin_ref):
    ang = pos_ref[...].astype(F32) * invf_ref[...]
    cos_ref[...] = jnp.cos(ang)
    sin_ref[...] = jnp.sin(ang)


def _rope_tables(positions):
    bsz, seq = positions.shape
    half = HEAD_DIM // 2
    inv_freq = ROPE_THETA ** (-jnp.arange(0, HEAD_DIM, 2, dtype=F32) / HEAD_DIM)
    per_row = LANES // half
    rows = bsz * seq // per_row
    pos_exp = jnp.repeat(positions.reshape(-1), half).reshape(rows, LANES)
    invf = jnp.tile(inv_freq, per_row).reshape(1, LANES)
    blk = 1024
    cos, sin = pl.pallas_call(
        _rope_table_kernel,
        out_shape=[jax.ShapeDtypeStruct((rows, LANES), F32)] * 2,
        grid=(rows // blk,),
        in_specs=[pl.BlockSpec((blk, LANES), lambda i: (i, 0)), _const_spec((1, LANES))],
        out_specs=[pl.BlockSpec((blk, LANES), lambda i: (i, 0))] * 2,
        compiler_params=_cparams(1),
        name="rope_tables",
    )(pos_exp, invf)
    cos = cos.reshape(bsz * seq, half)
    sin = sin.reshape(bsz * seq, half)
    cos_t = jnp.concatenate([cos, cos, cos, cos], axis=-1)
    sin_t = jnp.concatenate([-sin, sin, -sin, sin], axis=-1)
    return cos_t, sin_t


def _rope(p, cos, sin):
    width = p.shape[-1]
    lane = lax.broadcasted_iota(jnp.int32, p.shape, 1)
    first_half = (lane % HEAD_DIM) < (HEAD_DIM // 2)
    swapped = jnp.where(first_half,
                        pltpu.roll(p, width - HEAD_DIM // 2, 1),
                        pltpu.roll(p, HEAD_DIM // 2, 1))
    return p * cos + swapped * sin


def _in_proj_kernel(x_ref, g_ref, w_ref, wvt_ref, cos_ref, sin_ref,
                    qa_ref, ka_ref, va_ref, qd_ref, kd_ref, vdt_ref):
    hb = _rms(x_ref[...], g_ref[...]).astype(BF16)
    cos1 = cos_ref[...]
    sin1 = sin_ref[...]
    cos2 = jnp.concatenate([cos1, cos1], axis=-1)
    sin2 = jnp.concatenate([sin1, sin1], axis=-1)
    scale = HEAD_DIM ** -0.5

    def proj(c0, width):
        return jnp.dot(hb, w_ref[:, c0:c0 + width], preferred_element_type=F32)

    for c in range(SWA_Q_COLS // 256):
        p = _rope(proj(c * 256, 256), cos2, sin2) * scale
        qa_ref[:, c * 256:(c + 1) * 256] = p.astype(BF16)
    off = SWA_Q_COLS
    ka_ref[...] = _rope(proj(off, SWA_KV_COLS), cos1, sin1).astype(BF16)
    off += SWA_KV_COLS
    va_ref[...] = proj(off, SWA_KV_COLS).astype(BF16)
    off += SWA_KV_COLS
    for c in range(DIFF_QK_COLS // 256):
        p = _rope(proj(off + c * 256, 256), cos2, sin2) * scale
        qd_ref[:, c * 256:(c + 1) * 256] = p.astype(BF16)
    off += DIFF_QK_COLS
    for c in range(DIFF_QK_COLS // 256):
        p = _rope(proj(off + c * 256, 256), cos2, sin2)
        kd_ref[:, c * 256:(c + 1) * 256] = p.astype(BF16)
    vdt = lax.dot_general(wvt_ref[...], hb, (((1,), (1,)), ((), ())), preferred_element_type=F32)
    vdt_ref[...] = vdt.astype(BF16)


def _in_proj(x2d, g_mix, w_in, cos_t, sin_t):
    n_tok = x2d.shape[0]
    ts = TOK_TILE
    v_off = SWA_Q_COLS + 2 * SWA_KV_COLS + 2 * DIFF_QK_COLS
    w_main = w_in[:, :v_off].astype(BF16)
    w_vt = w_in[:, v_off:].T.astype(BF16)
    tok = lambda width: pl.BlockSpec((ts, width), lambda i: (i, 0))
    out_shape = [
        jax.ShapeDtypeStruct((n_tok, SWA_Q_COLS), BF16),
        jax.ShapeDtypeStruct((n_tok, SWA_KV_COLS), BF16),
        jax.ShapeDtypeStruct((n_tok, SWA_KV_COLS), BF16),
        jax.ShapeDtypeStruct((n_tok, DIFF_QK_COLS), BF16),
        jax.ShapeDtypeStruct((n_tok, DIFF_QK_COLS), BF16),
        jax.ShapeDtypeStruct((DIFF_V_COLS, n_tok), BF16),
    ]
    return pl.pallas_call(
        _in_proj_kernel,
        out_shape=out_shape,
        grid=(n_tok // ts,),
        in_specs=[tok(D_MODEL), _const_spec((1, D_MODEL)), _const_spec(w_main.shape),
                  _const_spec(w_vt.shape), tok(LANES), tok(LANES)],
        out_specs=[tok(SWA_Q_COLS), tok(SWA_KV_COLS), tok(SWA_KV_COLS), tok(DIFF_QK_COLS),
                   tok(DIFF_QK_COLS), pl.BlockSpec((DIFF_V_COLS, ts), lambda i: (0, i))],
        compiler_params=_cparams(1),
        name="in_proj",
    )(x2d, g_mix.reshape(1, D_MODEL), w_main, w_vt, cos_t, sin_t)


def _swa_kernel(q_ref, kc_ref, kp_ref, vc_ref, vp_ref, sink_ref, o_ref):
    first_tile = pl.program_id(1) == 0
    n_sub = q_ref.shape[0] // WINDOW
    qi = lax.broadcasted_iota(jnp.int32, (WINDOW, 2 * WINDOW), 0)
    kj = lax.broadcasted_iota(jnp.int32, (WINDOW, 2 * WINDOW), 1)
    rel = qi + WINDOW - kj
    band = (rel >= 0) & (rel < WINDOW)
    band_first = band & ((kj >= WINDOW) | jnp.logical_not(first_tile))
    band4 = jnp.concatenate([band] * SWA_GROUP, axis=0)
    band4_first = jnp.concatenate([band_first] * SWA_GROUP, axis=0)
    for r in range(n_sub):
        rows = slice(r * WINDOW, (r + 1) * WINDOW)
        if r == 0:
            kwin = jnp.concatenate([kp_ref[...], kc_ref[rows, :]], axis=0)
            vwin = jnp.concatenate([vp_ref[...], vc_ref[rows, :]], axis=0)
            valid = band4_first
        else:
            kwin = kc_ref[(r - 1) * WINDOW:(r + 1) * WINDOW, :]
            vwin = vc_ref[(r - 1) * WINDOW:(r + 1) * WINDOW, :]
            valid = band4
        outs = []
        for kv in range(SWA_KV_HEADS):
            kh = kwin[:, kv * HEAD_DIM:(kv + 1) * HEAD_DIM]
            vh = vwin[:, kv * HEAD_DIM:(kv + 1) * HEAD_DIM]
            base = kv * SWA_GROUP
            q4 = jnp.concatenate(
                [q_ref[rows, (base + g) * HEAD_DIM:(base + g + 1) * HEAD_DIM]
                 for g in range(SWA_GROUP)], axis=0)
            s = lax.dot_general(q4, kh, (((1,), (1,)), ((), ())), preferred_element_type=F32)
            s = jnp.where(valid, s, NEG_BIG)
            sink = jnp.concatenate(
                [jnp.broadcast_to(sink_ref[:, base + g:base + g + 1], (WINDOW, 1))
                 for g in range(SWA_GROUP)], axis=0)
            m = jnp.maximum(jnp.max(s, axis=-1, keepdims=True), sink)
            e = jnp.exp(s - m)
            denom = jnp.sum(e, axis=-1, keepdims=True) + jnp.exp(sink - m)
            p = (e / denom).astype(BF16)
            o4 = jnp.dot(p, vh, preferred_element_type=F32)
            outs.extend(o4[g * WINDOW:(g + 1) * WINDOW] for g in range(SWA_GROUP))
        o_ref[rows, :] = jnp.concatenate(outs, axis=-1).astype(BF16)


def _swa(qa, ka, va, sinks, bsz, seq):
    ts = TOK_TILE
    per_b = seq // ts
    sub = ts // WINDOW
    cur = lambda width: pl.BlockSpec((ts, width), lambda b, t: (b * per_b + t, 0))
    prev = pl.BlockSpec(
        (WINDOW, SWA_KV_COLS),
        lambda b, t: (jnp.maximum(b * per_b * sub + t * sub - 1, 0), 0))
    return pl.pallas_call(
        _swa_kernel,
        out_shape=jax.ShapeDtypeStruct(qa.shape, BF16),
        grid=(bsz, per_b),
        in_specs=[cur(SWA_Q_COLS), cur(SWA_KV_COLS), prev, cur(SWA_KV_COLS), prev,
                  _const_spec((1, SWA_Q_HEADS))],
        out_specs=cur(SWA_Q_COLS),
        compiler_params=_cparams(2),
        name="swa",
    )(qa, ka, ka, va, va, sinks.reshape(1, SWA_Q_HEADS))


def _diff_kernel(lam_init, q_ref, k_ref, vt_ref, lamp_ref, g_ref, o_ref, acc_ref):
    i = pl.program_id(2)
    tq = q_ref.shape[0]
    tk = tq
    q = q_ref[...]
    lane = lax.broadcasted_iota(jnp.int32, q.shape, 1)
    zero = jnp.zeros_like(q)
    qs = (jnp.where(lane < HEAD_DIM, q, zero), jnp.where(lane >= HEAD_DIM, q, zero))
    acc_ref[...] = jnp.zeros_like(acc_ref)

    def step(j, stats, masked):
        start = pl.multiple_of(j * tk, tk)
        k = k_ref[pl.ds(start, tk), :]
        vt = vt_ref[:, pl.ds(start, tk)]
        new_stats = []
        for c in range(2):
            m, l = stats[2 * c], stats[2 * c + 1]
            s = lax.dot_general(k, qs[c], (((1,), (1,)), ((), ())), preferred_element_type=F32)
            if masked:
                key = lax.broadcasted_iota(jnp.int32, s.shape, 0)
                qry = lax.broadcasted_iota(jnp.int32, s.shape, 1)
                s = jnp.where(key <= qry, s, NEG_BIG)
            m_new = jnp.maximum(m, jnp.max(s, axis=0, keepdims=True))
            alpha = jnp.exp(m - m_new)
            p = jnp.exp(s - m_new)
            l_new = alpha * l + jnp.sum(p, axis=0, keepdims=True)
            pv = jnp.dot(vt, p.astype(BF16), preferred_element_type=F32)
            acc_ref[c] = alpha * acc_ref[c] + pv
            new_stats += [m_new, l_new]
        return tuple(new_stats)

    init = (jnp.full((1, tq), NEG_BIG, F32), jnp.zeros((1, tq), F32)) * 2
    stats = lax.fori_loop(0, i, lambda j, st: step(j, st, False), init)
    _, l0, _, l1 = step(i, stats, True)

    lp = lamp_ref[...]
    lam = (jnp.exp(jnp.sum(lp[0:1] * lp[1:2], axis=-1, keepdims=True))
           - jnp.exp(jnp.sum(lp[2:3] * lp[3:4], axis=-1, keepdims=True)) + lam_init)
    o = acc_ref[0] / l0 - lam * (acc_ref[1] / l1)
    ms = jnp.mean(o * o, axis=0, keepdims=True)
    y = o * lax.rsqrt(ms + NORM_EPS) * g_ref[...] * (1.0 - lam_init)
    o_ref[...] = y.T.astype(BF16)


def _diff_attn(qd, kd, vdt, lam_params, g_diff, lam_init, bsz, seq):
    tq = DIFF_TILE
    nq = seq // tq
    return pl.pallas_call(
        functools.partial(_diff_kernel, lam_init),
        out_shape=jax.ShapeDtypeStruct(qd.shape, BF16),
        grid=(bsz, DIFF_HEADS, nq),
        in_specs=[
            pl.BlockSpec((tq, DIFF_VDIM), lambda b, h, i: (b * nq + i, h)),
            pl.BlockSpec((seq, DIFF_VDIM), lambda b, h, i: (b, h)),
            pl.BlockSpec((DIFF_VDIM, seq), lambda b, h, i: (h, b)),
            _const_spec((4, HEAD_DIM)),
            _const_spec((DIFF_VDIM, 1)),
        ],
        out_specs=pl.BlockSpec((tq, DIFF_VDIM), lambda b, h, i: (b * nq + i, h)),
        scratch_shapes=[pltpu.VMEM((2, DIFF_VDIM, tq), F32)],
        compiler_params=_cparams(3),
        name="diff_attn",
    )(qd, kd, vdt, lam_params, g_diff.reshape(DIFF_VDIM, 1))


def _out_proj_kernel(a_ref, b_ref, x_ref, woa_ref, wob_ref, g_ref, wq_ref, x1_ref, qc_ref):
    y = (jnp.dot(a_ref[...], woa_ref[...], preferred_element_type=F32)
         + jnp.dot(b_ref[...], wob_ref[...], preferred_element_type=F32))
    x1 = x_ref[...] + y
    x1_ref[...] = x1
    hc = _rms(x1, g_ref[...]).astype(BF16)
    qc = jnp.dot(hc, wq_ref[...], preferred_element_type=F32) * (CROSS_HEAD_DIM ** -0.5)
    qc_ref[...] = qc.astype(BF16)


def _out_proj(out_a, out_b, x2d, w_out, g_cross, w_cq):
    n_tok = x2d.shape[0]
    ts = TOK_TILE
    tok = lambda width: pl.BlockSpec((ts, width), lambda i: (i, 0))
    half = SWA_Q_COLS
    return pl.pallas_call(
        _out_proj_kernel,
        out_shape=[jax.ShapeDtypeStruct((n_tok, D_MODEL), F32),
                   jax.ShapeDtypeStruct((n_tok, D_MODEL), BF16)],
        grid=(n_tok // ts,),
        in_specs=[tok(half), tok(half), tok(D_MODEL), _const_spec((half, D_MODEL)),
                  _const_spec((half, D_MODEL)), _const_spec((1, D_MODEL)),
                  _const_spec((D_MODEL, D_MODEL))],
        out_specs=[tok(D_MODEL), tok(D_MODEL)],
        compiler_params=_cparams(1),
        name="out_proj",
    )(out_a, out_b, x2d, w_out[:half].astype(BF16), w_out[half:].astype(BF16),
      g_cross.reshape(1, D_MODEL), w_cq.astype(BF16))


def _mem_kv_kernel(m_ref, g_ref, w_ref, k_ref, v_ref):
    hm = _rms(m_ref[...], g_ref[...]).astype(BF16)
    k_ref[...] = jnp.dot(hm, w_ref[:, :D_MODEL], preferred_element_type=F32).astype(BF16)
    v_ref[...] = jnp.dot(hm, w_ref[:, D_MODEL:], preferred_element_type=F32).astype(BF16)


def _mem_kv(mem2d, g_mem, w_ckv, bsz, mem_len):
    blk = pl.BlockSpec((mem_len, D_MODEL), lambda b: (b, 0))
    return pl.pallas_call(
        _mem_kv_kernel,
        out_shape=[jax.ShapeDtypeStruct(mem2d.shape, BF16)] * 2,
        grid=(bsz,),
        in_specs=[blk, _const_spec((1, D_MODEL)), _const_spec((D_MODEL, 2 * D_MODEL))],
        out_specs=[blk, blk],
        compiler_params=_cparams(1),
        name="mem_kv",
    )(mem2d, g_mem.reshape(1, D_MODEL), w_ckv.astype(BF16))


def _cross_kernel(q_ref, k_ref, v_ref, wo_ref, x_ref, o_ref):
    heads = []
    for h in range(CROSS_HEADS):
        cols = slice(h * CROSS_HEAD_DIM, (h + 1) * CROSS_HEAD_DIM)
        s = lax.dot_general(q_ref[:, cols], k_ref[:, cols], (((1,), (1,)), ((), ())),
                            preferred_element_type=F32)
        m = jnp.max(s, axis=-1, keepdims=True)
        e = jnp.exp(s - m)
        p = (e / jnp.sum(e, axis=-1, keepdims=True)).astype(BF16)
        heads.append(jnp.dot(p, v_ref[:, cols], preferred_element_type=F32).astype(BF16))
    o = jnp.concatenate(heads, axis=-1)
    o_ref[...] = x_ref[...] + jnp.dot(o, wo_ref[...], preferred_element_type=F32)


def _cross_attn(qc, kc, vc, w_co, x1, seq, mem_len):
    n_tok = x1.shape[0]
    ts = TOK_TILE
    per_b = seq // ts
    tok = pl.BlockSpec((ts, D_MODEL), lambda i: (i, 0))
    memblk = pl.BlockSpec((mem_len, D_MODEL), lambda i: (i // per_b, 0))
    return pl.pallas_call(
        _cross_kernel,
        out_shape=jax.ShapeDtypeStruct(x1.shape, F32),
        grid=(n_tok // ts,),
        in_specs=[tok, memblk, memblk, _const_spec((D_MODEL, D_MODEL)), tok],
        out_specs=tok,
        compiler_params=_cparams(1),
        name="cross_attn",
    )(qc, kc, vc, w_co.astype(BF16), x1)


def _mlp_kernel(final_norm, x_ref, g_ref, wu_ref, wd_ref, gf_ref, o_ref):
    x = x_ref[...]
    hb = _rms(x, g_ref[...]).astype(BF16)
    acc = x
    d_ff = wu_ref.shape[1]
    for c in range(d_ff // FF_CHUNK):
        cols = slice(c * FF_CHUNK, (c + 1) * FF_CHUNK)
        u = jnp.dot(hb, wu_ref[:, cols], preferred_element_type=F32)
        r = jnp.maximum(u, 0.0)
        acc = acc + jnp.dot((r * r).astype(BF16), wd_ref[cols, :], preferred_element_type=F32)
    o_ref[...] = _rms(acc, gf_ref[...]) if final_norm else acc


def _mlp(x2, g_mlp, w_up, w_down, g_final, final_norm):
    n_tok = x2.shape[0]
    ts = TOK_TILE
    tok = pl.BlockSpec((ts, D_MODEL), lambda i: (i, 0))
    return pl.pallas_call(
        functools.partial(_mlp_kernel, final_norm),
        out_shape=jax.ShapeDtypeStruct(x2.shape, F32),
        grid=(n_tok // ts,),
        in_specs=[tok, _const_spec((1, D_MODEL)), _const_spec(w_up.shape),
                  _const_spec(w_down.shape), _const_spec((1, D_MODEL))],
        out_specs=tok,
        compiler_params=_cparams(1),
        name="mlp",
    )(x2, g_mlp.reshape(1, D_MODEL), w_up.astype(BF16), w_down.astype(BF16),
      g_final.reshape(1, D_MODEL))


def kernel(x, mem, positions, g_mix, w_in, sinks, lambda_q1, lambda_k1, lambda_q2, lambda_k2,
           g_diff, w_out, g_cross, g_mem, w_cq, w_ckv, w_co, g_mlp, w_up, w_down, g_final):
    bsz, seq, _ = x.shape
    mem_len = mem.shape[1]
    depth = w_in.shape[0]
    cos_t, sin_t = _rope_tables(positions)
    xf = x.reshape(bsz * seq, D_MODEL)
    memf = mem.reshape(bsz * mem_len, D_MODEL)
    for l in range(depth):
        lam_init = 0.8 - 0.6 * math.exp(-0.3 * l)
        qa, ka, va, qd, kd, vdt = _in_proj(xf, g_mix[l], w_in[l], cos_t, sin_t)
        out_a = _swa(qa, ka, va, sinks[l], bsz, seq)
        lam_params = jnp.stack([lambda_q1[l], lambda_k1[l], lambda_q2[l], lambda_k2[l]])
        out_b = _diff_attn(qd, kd, vdt, lam_params, g_diff[l], lam_init, bsz, seq)
        x1, qc = _out_proj(out_a, out_b, xf, w_out[l], g_cross[l], w_cq[l])
        kc, vc = _mem_kv(memf, g_mem[l], w_ckv[l], bsz, mem_len)
        x2 = _cross_attn(qc, kc, vc, w_co[l], x1, seq, mem_len)
        xf = _mlp(x2, g_mlp[l], w_up[l], w_down[l], g_final, l == depth - 1)
    return xf.reshape(bsz, seq, D_MODEL)
```

```python
import functools
import math

import jax
import jax.numpy as jnp
from jax import lax
from jax.experimental import pallas as pl
from jax.experimental.pallas import tpu as pltpu

D_MODEL = 1024
HEAD_DIM = 64
SWA_Q_HEADS = 8
SWA_KV_HEADS = 2
SWA_GROUP = SWA_Q_HEADS // SWA_KV_HEADS
WINDOW = 128
DIFF_HEADS = 4
DIFF_VDIM = 2 * HEAD_DIM
SWA_Q_COLS = SWA_Q_HEADS * HEAD_DIM
SWA_KV_COLS = SWA_KV_HEADS * HEAD_DIM
DIFF_QK_COLS = DIFF_HEADS * 2 * HEAD_DIM
DIFF_V_COLS = DIFF_HEADS * DIFF_VDIM
CROSS_HEADS = 4
CROSS_HEAD_DIM = D_MODEL // CROSS_HEADS
ROPE_THETA = 10000.0
NORM_EPS = 1e-5

LANES = 128
TOK_TILE = 512
DIFF_TILE = 512
FF_CHUNK = 512
VMEM_LIMIT = 56 * 1024 * 1024
NEG_BIG = -1e30
LOG2E = math.log2(math.e)

BF16 = jnp.bfloat16
F32 = jnp.float32


def _cparams(n_axes):
    return pltpu.CompilerParams(dimension_semantics=("arbitrary",) * n_axes,
                                vmem_limit_bytes=VMEM_LIMIT)


def _rms(x, g):
    ms = jnp.mean(x * x, axis=-1, keepdims=True)
    return x * lax.rsqrt(ms + NORM_EPS) * g


def _const_spec(shape):
    return pl.BlockSpec(shape, lambda *_: (0,) * len(shape))


def _rope_table_kernel(pos_ref, invf_ref, cos_ref, sin_ref):
    ang = pos_ref[...].astype(F32) * invf_ref[...]
    cos_ref[...] = jnp.cos(ang)
    sin_ref[...] = jnp.sin(ang)


def _rope_tables(positions):
    bsz, seq = positions.shape
    half = HEAD_DIM // 2
    inv_freq = ROPE_THETA ** (-jnp.arange(0, HEAD_DIM, 2, dtype=F32) / HEAD_DIM)
    per_row = LANES // half
    rows = bsz * seq // per_row
    pos_exp = jnp.repeat(positions.reshape(-1), half).reshape(rows, LANES)
    invf = jnp.tile(inv_freq, per_row).reshape(1, LANES)
    blk = 1024
    cos, sin = pl.pallas_call(
        _rope_table_kernel,
        out_shape=[jax.ShapeDtypeStruct((rows, LANES), F32)] * 2,
        grid=(rows // blk,),
        in_specs=[pl.BlockSpec((blk, LANES), lambda i: (i, 0)), _const_spec((1, LANES))],
        out_specs=[pl.BlockSpec((blk, LANES), lambda i: (i, 0))] * 2,
        compiler_params=_cparams(1),
        name="rope_tables",
    )(pos_exp, invf)
    cos = cos.reshape(bsz * seq, half)
    sin = sin.reshape(bsz * seq, half)
    cos_t = jnp.concatenate([cos, cos, cos, cos], axis=-1)
    sin_t = jnp.concatenate([-sin, sin, -sin, sin], axis=-1)
    return cos_t, sin_t


def _rope(p, cos, sin):
    width = p.shape[-1]
    lane = lax.broadcasted_iota(jnp.int32, p.shape, 1)
    first_half = (lane % HEAD_DIM) < (HEAD_DIM // 2)
    swapped = jnp.where(first_half,
                        pltpu.roll(p, width - HEAD_DIM // 2, 1),
                        pltpu.roll(p, HEAD_DIM // 2, 1))
    return p * cos + swapped * sin


def _in_proj_kernel(x_ref, g_ref, w_ref, wvt_ref, cos_ref, sin_ref,
                    qa_ref, ka_ref, va_ref, qd_ref, kd_ref, vdt_ref):
    hb = _rms(x_ref[...], g_ref[...]).astype(BF16)
    cos1 = cos_ref[...]
    sin1 = sin_ref[...]
    cos2 = jnp.concatenate([cos1, cos1], axis=-1)
    sin2 = jnp.concatenate([sin1, sin1], axis=-1)
    scale = HEAD_DIM ** -0.5

    def proj(c0, width):
        return jnp.dot(hb, w_ref[:, c0:c0 + width], preferred_element_type=F32)

    for c in range(SWA_Q_COLS // 256):
        p = _rope(proj(c * 256, 256), cos2, sin2) * scale
        qa_ref[:, c * 256:(c + 1) * 256] = p.astype(BF16)
    off = SWA_Q_COLS
    ka_ref[...] = _rope(proj(off, SWA_KV_COLS), cos1, sin1).astype(BF16)
    off += SWA_KV_COLS
    va_ref[...] = proj(off, SWA_KV_COLS).astype(BF16)
    off += SWA_KV_COLS
    for c in range(DIFF_QK_COLS // 256):
        p = _rope(proj(off + c * 256, 256), cos2, sin2) * (scale * LOG2E)
        qd_ref[:, c * 256:(c + 1) * 256] = p.astype(BF16)
    off += DIFF_QK_COLS
    for c in range(DIFF_QK_COLS // 256):
        p = _rope(proj(off + c * 256, 256), cos2, sin2)
        kd_ref[:, c * 256:(c + 1) * 256] = p.astype(BF16)
    vdt = lax.dot_general(wvt_ref[...], hb, (((1,), (1,)), ((), ())), preferred_element_type=F32)
    vdt_ref[...] = vdt.astype(BF16)


def _in_proj(x2d, g_mix, w_in, cos_t, sin_t):
    n_tok = x2d.shape[0]
    ts = TOK_TILE
    v_off = SWA_Q_COLS + 2 * SWA_KV_COLS + 2 * DIFF_QK_COLS
    w_main = w_in[:, :v_off].astype(BF16)
    w_vt = w_in[:, v_off:].T.astype(BF16)
    tok = lambda width: pl.BlockSpec((ts, width), lambda i: (i, 0))
    out_shape = [
        jax.ShapeDtypeStruct((n_tok, SWA_Q_COLS), BF16),
        jax.ShapeDtypeStruct((n_tok, SWA_KV_COLS), BF16),
        jax.ShapeDtypeStruct((n_tok, SWA_KV_COLS), BF16),
        jax.ShapeDtypeStruct((n_tok, DIFF_QK_COLS), BF16),
        jax.ShapeDtypeStruct((n_tok, DIFF_QK_COLS), BF16),
        jax.ShapeDtypeStruct((DIFF_V_COLS, n_tok), BF16),
    ]
    return pl.pallas_call(
        _in_proj_kernel,
        out_shape=out_shape,
        grid=(n_tok // ts,),
        in_specs=[tok(D_MODEL), _const_spec((1, D_MODEL)), _const_spec(w_main.shape),
                  _const_spec(w_vt.shape), tok(LANES), tok(LANES)],
        out_specs=[tok(SWA_Q_COLS), tok(SWA_KV_COLS), tok(SWA_KV_COLS), tok(DIFF_QK_COLS),
                   tok(DIFF_QK_COLS), pl.BlockSpec((DIFF_V_COLS, ts), lambda i: (0, i))],
        compiler_params=_cparams(1),
        name="in_proj",
    )(x2d, g_mix.reshape(1, D_MODEL), w_main, w_vt, cos_t, sin_t)


def _swa_kernel(q_ref, kc_ref, kp_ref, vc_ref, vp_ref, sink_ref, o_ref):
    first_tile = pl.program_id(1) == 0
    n_sub = q_ref.shape[0] // WINDOW
    qi = lax.broadcasted_iota(jnp.int32, (WINDOW, 2 * WINDOW), 0)
    kj = lax.broadcasted_iota(jnp.int32, (WINDOW, 2 * WINDOW), 1)
    rel = qi + WINDOW - kj
    band = (rel >= 0) & (rel < WINDOW)
    band_first = band & ((kj >= WINDOW) | jnp.logical_not(first_tile))
    band4 = jnp.concatenate([band] * SWA_GROUP, axis=0)
    band4_first = jnp.concatenate([band_first] * SWA_GROUP, axis=0)
    for r in range(n_sub):
        rows = slice(r * WINDOW, (r + 1) * WINDOW)
        if r == 0:
            kwin = jnp.concatenate([kp_ref[...], kc_ref[rows, :]], axis=0)
            vwin = jnp.concatenate([vp_ref[...], vc_ref[rows, :]], axis=0)
            valid = band4_first
        else:
            kwin = kc_ref[(r - 1) * WINDOW:(r + 1) * WINDOW, :]
            vwin = vc_ref[(r - 1) * WINDOW:(r + 1) * WINDOW, :]
            valid = band4
        outs = []
        for kv in range(SWA_KV_HEADS):
            kh = kwin[:, kv * HEAD_DIM:(kv + 1) * HEAD_DIM]
            vh = vwin[:, kv * HEAD_DIM:(kv + 1) * HEAD_DIM]
            base = kv * SWA_GROUP
            q4 = jnp.concatenate(
                [q_ref[rows, (base + g) * HEAD_DIM:(base + g + 1) * HEAD_DIM]
                 for g in range(SWA_GROUP)], axis=0)
            s = lax.dot_general(q4, kh, (((1,), (1,)), ((), ())), preferred_element_type=F32)
            s = jnp.where(valid, s, NEG_BIG)
            sink = jnp.concatenate(
                [jnp.broadcast_to(sink_ref[:, base + g:base + g + 1], (WINDOW, 1))
                 for g in range(SWA_GROUP)], axis=0)
            m = jnp.maximum(jnp.max(s, axis=-1, keepdims=True), sink)
            e = jnp.exp(s - m)
            denom = jnp.sum(e, axis=-1, keepdims=True) + jnp.exp(sink - m)
            p = (e / denom).astype(BF16)
            o4 = jnp.dot(p, vh, preferred_element_type=F32)
            outs.extend(o4[g * WINDOW:(g + 1) * WINDOW] for g in range(SWA_GROUP))
        o_ref[rows, :] = jnp.concatenate(outs, axis=-1).astype(BF16)


def _swa(qa, ka, va, sinks, bsz, seq):
    ts = TOK_TILE
    per_b = seq // ts
    sub = ts // WINDOW
    cur = lambda width: pl.BlockSpec((ts, width), lambda b, t: (b * per_b + t, 0))
    prev = pl.BlockSpec(
        (WINDOW, SWA_KV_COLS),
        lambda b, t: (jnp.maximum(b * per_b * sub + t * sub - 1, 0), 0))
    return pl.pallas_call(
        _swa_kernel,
        out_shape=jax.ShapeDtypeStruct(qa.shape, BF16),
        grid=(bsz, per_b),
        in_specs=[cur(SWA_Q_COLS), cur(SWA_KV_COLS), prev, cur(SWA_KV_COLS), prev,
                  _const_spec((1, SWA_Q_HEADS))],
        out_specs=cur(SWA_Q_COLS),
        compiler_params=_cparams(2),
        name="swa",
    )(qa, ka, ka, va, va, sinks.reshape(1, SWA_Q_HEADS))


def _diff_kernel(lam_init, q_ref, k_ref, vt_ref, lamp_ref, g_ref, o_ref,
                 s0_ref, s1_ref, acc_ref):
    i = pl.program_id(2)
    tq = q_ref.shape[0]
    tk = tq
    q = q_ref[...]
    lane = lax.broadcasted_iota(jnp.int32, q.shape, 1)
    zero = jnp.zeros_like(q)
    qs = (jnp.where(lane < HEAD_DIM, q, zero), jnp.where(lane >= HEAD_DIM, q, zero))
    s_refs = (s0_ref, s1_ref)
    acc_ref[...] = jnp.zeros_like(acc_ref)

    def k_tile(j):
        return k_ref[pl.ds(pl.multiple_of(j * tk, tk), tk), :]

    def vt_tile(j):
        return vt_ref[:, pl.ds(pl.multiple_of(j * tk, tk), tk)]

    def scores(c, k, masked):
        s = lax.dot_general(k, qs[c], (((1,), (1,)), ((), ())), preferred_element_type=F32)
        if masked:
            key = lax.broadcasted_iota(jnp.int32, s.shape, 0)
            qry = lax.broadcasted_iota(jnp.int32, s.shape, 1)
            s = jnp.where(key <= qry, s, NEG_BIG)
        s_refs[c][...] = s
        return jnp.max(s, axis=0, keepdims=True)

    def consume(c, tile_max, m, l, vt):
        m_new = jnp.maximum(m, tile_max)
        alpha = jnp.exp2(m - m_new)
        p = jnp.exp2(s_refs[c][...] - m_new)
        l_new = alpha * l + jnp.sum(p, axis=0, keepdims=True)
        pv = jnp.dot(vt, p.astype(BF16), preferred_element_type=F32)
        acc_ref[c] = alpha * acc_ref[c] + pv
        return m_new, l_new

    def pair(j, j_next, carry, masked):
        m0, l0, m1, l1, max0 = carry
        k = k_tile(j)
        vt = vt_tile(j)
        max1 = scores(1, k, masked)
        m0, l0 = consume(0, max0, m0, l0, vt)
        max0 = scores(0, k_tile(j_next), False)
        m1, l1 = consume(1, max1, m1, l1, vt)
        return m0, l0, m1, l1, max0

    neg = jnp.full((1, tq), NEG_BIG, F32)
    zeros = jnp.zeros((1, tq), F32)
    max0 = scores(0, k_tile(i), True)
    carry = pair(i, 0, (neg, zeros, neg, zeros, max0), True)
    carry = lax.fori_loop(0, i, lambda j, cr: pair(j, j + 1, cr, False), carry)
    _, l0, _, l1, _ = carry

    lp = lamp_ref[...]
    lam = (jnp.exp(jnp.sum(lp[0:1] * lp[1:2], axis=-1, keepdims=True))
           - jnp.exp(jnp.sum(lp[2:3] * lp[3:4], axis=-1, keepdims=True)) + lam_init)
    o = acc_ref[0] / l0 - lam * (acc_ref[1] / l1)
    ms = jnp.mean(o * o, axis=0, keepdims=True)
    y = o * lax.rsqrt(ms + NORM_EPS) * g_ref[...] * (1.0 - lam_init)
    o_ref[...] = y.T.astype(BF16)


def _diff_attn(qd, kd, vdt, lam_params, g_diff, lam_init, bsz, seq):
    tq = DIFF_TILE
    nq = seq // tq
    return pl.pallas_call(
        functools.partial(_diff_kernel, lam_init),
        out_shape=jax.ShapeDtypeStruct(qd.shape, BF16),
        grid=(bsz, DIFF_HEADS, nq),
        in_specs=[
            pl.BlockSpec((tq, DIFF_VDIM), lambda b, h, i: (b * nq + i, h)),
            pl.BlockSpec((seq, DIFF_VDIM), lambda b, h, i: (b, h)),
            pl.BlockSpec((DIFF_VDIM, seq), lambda b, h, i: (h, b)),
            _const_spec((4, HEAD_DIM)),
            _const_spec((DIFF_VDIM, 1)),
        ],
        out_specs=pl.BlockSpec((tq, DIFF_VDIM), lambda b, h, i: (b * nq + i, h)),
        scratch_shapes=[pltpu.VMEM((tq, tq), F32), pltpu.VMEM((tq, tq), F32),
                        pltpu.VMEM((2, DIFF_VDIM, tq), F32)],
        compiler_params=_cparams(3),
        name="diff_attn",
    )(qd, kd, vdt, lam_params, g_diff.reshape(DIFF_VDIM, 1))


def _out_proj_kernel(a_ref, b_ref, x_ref, woa_ref, wob_ref, g_ref, wq_ref, x1_ref, qc_ref):
    y = (jnp.dot(a_ref[...], woa_ref[...], preferred_element_type=F32)
         + jnp.dot(b_ref[...], wob_ref[...], preferred_element_type=F32))
    x1 = x_ref[...] + y
    x1_ref[...] = x1
    hc = _rms(x1, g_ref[...]).astype(BF16)
    qc = jnp.dot(hc, wq_ref[...], preferred_element_type=F32) * (CROSS_HEAD_DIM ** -0.5)
    qc_ref[...] = qc.astype(BF16)


def _out_proj(out_a, out_b, x2d, w_out, g_cross, w_cq):
    n_tok = x2d.shape[0]
    ts = TOK_TILE
    tok = lambda width: pl.BlockSpec((ts, width), lambda i: (i, 0))
    half = SWA_Q_COLS
    return pl.pallas_call(
        _out_proj_kernel,
        out_shape=[jax.ShapeDtypeStruct((n_tok, D_MODEL), F32),
                   jax.ShapeDtypeStruct((n_tok, D_MODEL), BF16)],
        grid=(n_tok // ts,),
        in_specs=[tok(half), tok(half), tok(D_MODEL), _const_spec((half, D_MODEL)),
                  _const_spec((half, D_MODEL)), _const_spec((1, D_MODEL)),
                  _const_spec((D_MODEL, D_MODEL))],
        out_specs=[tok(D_MODEL), tok(D_MODEL)],
        compiler_params=_cparams(1),
        name="out_proj",
    )(out_a, out_b, x2d, w_out[:half].astype(BF16), w_out[half:].astype(BF16),
      g_cross.reshape(1, D_MODEL), w_cq.astype(BF16))


def _mem_kv_kernel(m_ref, g_ref, w_ref, k_ref, v_ref):
    hm = _rms(m_ref[...], g_ref[...]).astype(BF16)
    k_ref[...] = jnp.dot(hm, w_ref[:, :D_MODEL], preferred_element_type=F32).astype(BF16)
    v_ref[...] = jnp.dot(hm, w_ref[:, D_MODEL:], preferred_element_type=F32).astype(BF16)


def _mem_kv(mem2d, g_mem, w_ckv, bsz, mem_len):
    blk = pl.BlockSpec((mem_len, D_MODEL), lambda b: (b, 0))
    return pl.pallas_call(
        _mem_kv_kernel,
        out_shape=[jax.ShapeDtypeStruct(mem2d.shape, BF16)] * 2,
        grid=(bsz,),
        in_specs=[blk, _const_spec((1, D_MODEL)), _const_spec((D_MODEL, 2 * D_MODEL))],
        out_specs=[blk, blk],
        compiler_params=_cparams(1),
        name="mem_kv",
    )(mem2d, g_mem.reshape(1, D_MODEL), w_ckv.astype(BF16))


def _cross_kernel(q_ref, k_ref, v_ref, wo_ref, x_ref, o_ref):
    heads = []
    for h in range(CROSS_HEADS):
        cols = slice(h * CROSS_HEAD_DIM, (h + 1) * CROSS_HEAD_DIM)
        s = lax.dot_general(q_ref[:, cols], k_ref[:, cols], (((1,), (1,)), ((), ())),
                            preferred_element_type=F32)
        m = jnp.max(s, axis=-1, keepdims=True)
        e = jnp.exp(s - m)
        p = (e / jnp.sum(e, axis=-1, keepdims=True)).astype(BF16)
        heads.append(jnp.dot(p, v_ref[:, cols], preferred_element_type=F32).astype(BF16))
    o = jnp.concatenate(heads, axis=-1)
    o_ref[...] = x_ref[...] + jnp.dot(o, wo_ref[...], preferred_element_type=F32)


def _cross_attn(qc, kc, vc, w_co, x1, seq, mem_len):
    n_tok = x1.shape[0]
    ts = TOK_TILE
    per_b = seq // ts
    tok = pl.BlockSpec((ts, D_MODEL), lambda i: (i, 0))
    memblk = pl.BlockSpec((mem_len, D_MODEL), lambda i: (i // per_b, 0))
    return pl.pallas_call(
        _cross_kernel,
        out_shape=jax.ShapeDtypeStruct(x1.shape, F32),
        grid=(n_tok // ts,),
        in_specs=[tok, memblk, memblk, _const_spec((D_MODEL, D_MODEL)), tok],
        out_specs=tok,
        compiler_params=_cparams(1),
        name="cross_attn",
    )(qc, kc, vc, w_co.astype(BF16), x1)


def _mlp_kernel(final_norm, x_ref, g_ref, wu_ref, wd_ref, gf_ref, o_ref):
    x = x_ref[...]
    hb = _rms(x, g_ref[...]).astype(BF16)
    acc = x
    d_ff = wu_ref.shape[1]
    for c in range(d_ff // FF_CHUNK):
        cols = slice(c * FF_CHUNK, (c + 1) * FF_CHUNK)
        u = jnp.dot(hb, wu_ref[:, cols], preferred_element_type=F32)
        r = jnp.maximum(u, 0.0)
        acc = acc + jnp.dot((r * r).astype(BF16), wd_ref[cols, :], preferred_element_type=F32)
    o_ref[...] = _rms(acc, gf_ref[...]) if final_norm else acc


def _mlp(x2, g_mlp, w_up, w_down, g_final, final_norm):
    n_tok = x2.shape[0]
    ts = TOK_TILE
    tok = pl.BlockSpec((ts, D_MODEL), lambda i: (i, 0))
    return pl.pallas_call(
        functools.partial(_mlp_kernel, final_norm),
        out_shape=jax.ShapeDtypeStruct(x2.shape, F32),
        grid=(n_tok // ts,),
        in_specs=[tok, _const_spec((1, D_MODEL)), _const_spec(w_up.shape),
                  _const_spec(w_down.shape), _const_spec((1, D_MODEL))],
        out_specs=tok,
        compiler_params=_cparams(1),
        name="mlp",
    )(x2, g_mlp.reshape(1, D_MODEL), w_up.astype(BF16), w_down.astype(BF16),
      g_final.reshape(1, D_MODEL))


def kernel(x, mem, positions, g_mix, w_in, sinks, lambda_q1, lambda_k1, lambda_q2, lambda_k2,
           g_diff, w_out, g_cross, g_mem, w_cq, w_ckv, w_co, g_mlp, w_up, w_down, g_final):
    bsz, seq, _ = x.shape
    mem_len = mem.shape[1]
    depth = w_in.shape[0]
    cos_t, sin_t = _rope_tables(positions)
    xf = x.reshape(bsz * seq, D_MODEL)
    memf = mem.reshape(bsz * mem_len, D_MODEL)
    for l in range(depth):
        lam_init = 0.8 - 0.6 * math.exp(-0.3 * l)
        qa, ka, va, qd, kd, vdt = _in_proj(xf, g_mix[l], w_in[l], cos_t, sin_t)
        out_a = _swa(qa, ka, va, sinks[l], bsz, seq)
        lam_params = jnp.stack([lambda_q1[l], lambda_k1[l], lambda_q2[l], lambda_k2[l]])
        out_b = _diff_attn(qd, kd, vdt, lam_params, g_diff[l], lam_init, bsz, seq)
        x1, qc = _out_proj(out_a, out_b, xf, w_out[l], g_cross[l], w_cq[l])
        kc, vc = _mem_kv(memf, g_mem[l], w_ckv[l], bsz, mem_len)
        x2 = _cross_attn(qc, kc, vc, w_co[l], x1, seq, mem_len)
        xf = _mlp(x2, g_mlp[l], w_up[l], w_down[l], g_final, l == depth - 1)
    return xf.reshape(bsz, seq, D_MODEL)
```

```python
import functools
import math

import jax
import jax.numpy as jnp
from jax import lax
from jax.experimental import pallas as pl
from jax.experimental.pallas import tpu as pltpu

D_MODEL = 1024
HEAD_DIM = 64
SWA_Q_HEADS = 8
SWA_KV_HEADS = 2
SWA_GROUP = SWA_Q_HEADS // SWA_KV_HEADS
WINDOW = 128
DIFF_HEADS = 4
DIFF_VDIM = 2 * HEAD_DIM
SWA_Q_COLS = SWA_Q_HEADS * HEAD_DIM
SWA_KV_COLS = SWA_KV_HEADS * HEAD_DIM
DIFF_QK_COLS = DIFF_HEADS * 2 * HEAD_DIM
DIFF_V_COLS = DIFF_HEADS * DIFF_VDIM
CROSS_HEADS = 4
CROSS_HEAD_DIM = D_MODEL // CROSS_HEADS
ROPE_THETA = 10000.0
NORM_EPS = 1e-5

LANES = 128
TOK_TILE = 512
DIFF_TILE = 512
FF_CHUNK = 512
VMEM_LIMIT = 56 * 1024 * 1024
NEG_BIG = -1e30
LOG2E = math.log2(math.e)

BF16 = jnp.bfloat16
F32 = jnp.float32


def _cparams(n_axes):
    return pltpu.CompilerParams(dimension_semantics=("arbitrary",) * n_axes,
                                vmem_limit_bytes=VMEM_LIMIT)


def _rms(x, g):
    ms = jnp.mean(x * x, axis=-1, keepdims=True)
    return x * lax.rsqrt(ms + NORM_EPS) * g


def _const_spec(shape):
    return pl.BlockSpec(shape, lambda *_: (0,) * len(shape))


def _rope_table_kernel(pos_ref, invf_ref, cos_ref, sin_ref):
    ang = pos_ref[...].astype(F32) * invf_ref[...]
    cos_ref[...] = jnp.cos(ang)
    sin_ref[...] = jnp.sin(ang)


def _rope_tables(positions):
    bsz, seq = positions.shape
    half = HEAD_DIM // 2
    inv_freq = ROPE_THETA ** (-jnp.arange(0, HEAD_DIM, 2, dtype=F32) / HEAD_DIM)
    per_row = LANES // half
    rows = bsz * seq // per_row
    pos_exp = jnp.repeat(positions.reshape(-1), half).reshape(rows, LANES)
    invf = jnp.tile(inv_freq, per_row).reshape(1, LANES)
    blk = 1024
    cos, sin = pl.pallas_call(
        _rope_table_kernel,
        out_shape=[jax.ShapeDtypeStruct((rows, LANES), F32)] * 2,
        grid=(rows // blk,),
        in_specs=[pl.BlockSpec((blk, LANES), lambda i: (i, 0)), _const_spec((1, LANES))],
        out_specs=[pl.BlockSpec((blk, LANES), lambda i: (i, 0))] * 2,
        compiler_params=_cparams(1),
        name="rope_tables",
    )(pos_exp, invf)
    cos = cos.reshape(bsz * seq, half)
    sin = sin.reshape(bsz * seq, half)
    cos_t = jnp.concatenate([cos, cos, cos, cos], axis=-1)
    sin_t = jnp.concatenate([-sin, sin, -sin, sin], axis=-1)
    return cos_t, sin_t


def _rope(p, cos, sin):
    width = p.shape[-1]
    lane = lax.broadcasted_iota(jnp.int32, p.shape, 1)
    first_half = (lane % HEAD_DIM) < (HEAD_DIM // 2)
    swapped = jnp.where(first_half,
                        pltpu.roll(p, width - HEAD_DIM // 2, 1),
                        pltpu.roll(p, HEAD_DIM // 2, 1))
    return p * cos + swapped * sin


def _in_proj_kernel(x_ref, g_ref, w_ref, wvt_ref, cos_ref, sin_ref,
                    qa_ref, kaw_ref, vat_ref, qd_ref, kd_ref, vdt_ref):
    hb = _rms(x_ref[...], g_ref[...]).astype(BF16)
    cos1 = cos_ref[...]
    sin1 = sin_ref[...]
    cos2 = jnp.concatenate([cos1, cos1], axis=-1)
    sin2 = jnp.concatenate([sin1, sin1], axis=-1)
    q_scale = HEAD_DIM ** -0.5 * LOG2E

    def proj(c0, width):
        return jnp.dot(hb, w_ref[:, c0:c0 + width], preferred_element_type=F32)

    for c in range(SWA_Q_COLS // 256):
        p = _rope(proj(c * 256, 256), cos2, sin2) * q_scale
        qa_ref[:, c * 256:(c + 1) * 256] = p.astype(BF16)
    off = SWA_Q_COLS
    ka = _rope(proj(off, SWA_KV_COLS), cos1, sin1)
    ka_swapped = pltpu.roll(ka, HEAD_DIM, 1)
    lane = lax.broadcasted_iota(jnp.int32, ka.shape, 1)
    for kv, rep in enumerate((jnp.where(lane < HEAD_DIM, ka, ka_swapped),
                              jnp.where(lane < HEAD_DIM, ka_swapped, ka))):
        rep = rep.astype(BF16)
        for half in range(2):
            c0 = kv * SWA_GROUP * HEAD_DIM + half * LANES
            kaw_ref[:, c0:c0 + LANES] = rep
    off += SWA_KV_COLS
    for c in range(DIFF_QK_COLS // 256):
        p = _rope(proj(off + c * 256, 256), cos2, sin2) * q_scale
        qd_ref[:, c * 256:(c + 1) * 256] = p.astype(BF16)
    off += DIFF_QK_COLS
    for c in range(DIFF_QK_COLS // 256):
        p = _rope(proj(off + c * 256, 256), cos2, sin2)
        kd_ref[:, c * 256:(c + 1) * 256] = p.astype(BF16)
    vt = lax.dot_general(wvt_ref[...], hb, (((1,), (1,)), ((), ())), preferred_element_type=F32)
    vat_ref[...] = vt[:SWA_KV_COLS].astype(BF16)
    vdt_ref[...] = vt[SWA_KV_COLS:].astype(BF16)


def _in_proj(x2d, g_mix, w_in, cos_t, sin_t):
    n_tok = x2d.shape[0]
    ts = TOK_TILE
    va_off = SWA_Q_COLS + SWA_KV_COLS
    qd_off = va_off + SWA_KV_COLS
    vd_off = qd_off + 2 * DIFF_QK_COLS
    w_main = jnp.concatenate([w_in[:, :va_off], w_in[:, qd_off:vd_off]], axis=1).astype(BF16)
    w_vt = jnp.concatenate([w_in[:, va_off:qd_off], w_in[:, vd_off:]], axis=1).T.astype(BF16)
    tok = lambda width: pl.BlockSpec((ts, width), lambda i: (i, 0))
    tok_t = lambda rows: pl.BlockSpec((rows, ts), lambda i: (0, i))
    kaw_cols = SWA_KV_HEADS * SWA_GROUP * HEAD_DIM
    out_shape = [
        jax.ShapeDtypeStruct((n_tok, SWA_Q_COLS), BF16),
        jax.ShapeDtypeStruct((n_tok, kaw_cols), BF16),
        jax.ShapeDtypeStruct((SWA_KV_COLS, n_tok), BF16),
        jax.ShapeDtypeStruct((n_tok, DIFF_QK_COLS), BF16),
        jax.ShapeDtypeStruct((n_tok, DIFF_QK_COLS), BF16),
        jax.ShapeDtypeStruct((DIFF_V_COLS, n_tok), BF16),
    ]
    return pl.pallas_call(
        _in_proj_kernel,
        out_shape=out_shape,
        grid=(n_tok // ts,),
        in_specs=[tok(D_MODEL), _const_spec((1, D_MODEL)), _const_spec(w_main.shape),
                  _const_spec(w_vt.shape), tok(LANES), tok(LANES)],
        out_specs=[tok(SWA_Q_COLS), tok(kaw_cols), tok_t(SWA_KV_COLS), tok(DIFF_QK_COLS),
                   tok(DIFF_QK_COLS), tok_t(DIFF_V_COLS)],
        compiler_params=_cparams(1),
        name="in_proj",
    )(x2d, g_mix.reshape(1, D_MODEL), w_main, w_vt, cos_t, sin_t)


def _swa_kernel(q_ref, kc_ref, kp_ref, vc_ref, vp_ref, sink_ref, o_ref):
    n_sub = q_ref.shape[0] // WINDOW
    group_cols = SWA_GROUP * HEAD_DIM
    kj = lax.broadcasted_iota(jnp.int32, (WINDOW, 2 * WINDOW), 0)
    qi = lax.broadcasted_iota(jnp.int32, (WINDOW, 2 * WINDOW), 1) % WINDOW
    from_prev = kj > qi
    no_prev_bias = jnp.where(pl.program_id(1) == 0, NEG_BIG, 0.0)
    qlane = lax.broadcasted_iota(jnp.int32, (WINDOW, group_cols), 1) // HEAD_DIM
    sinks = sink_ref[...] * LOG2E
    pairs = SWA_GROUP // 2
    units = [(r, kv, pair) for r in range(n_sub) for kv in range(SWA_KV_HEADS)
             for pair in range(pairs)]

    def key_rows(r):
        if r == 0:
            return jnp.concatenate([kp_ref[...], kc_ref[0:WINDOW, :]], axis=0)
        return kc_ref[(r - 1) * WINDOW:(r + 1) * WINDOW, :]

    def value_cols(r):
        if r == 0:
            return jnp.concatenate([vp_ref[...], vc_ref[:, 0:WINDOW]], axis=1)
        return vc_ref[:, (r - 1) * WINDOW:(r + 1) * WINDOW]

    def scores(unit):
        r, kv, pair = unit
        qg = q_ref[r * WINDOW:(r + 1) * WINDOW, kv * group_cols:(kv + 1) * group_cols]
        zero = jnp.zeros_like(qg)
        q2 = jnp.concatenate([jnp.where(qlane == 2 * pair + g, qg, zero) for g in range(2)],
                             axis=0)
        kw = key_rows(r)[:, kv * group_cols:(kv + 1) * group_cols]
        s = lax.dot_general(kw, q2, (((1,), (1,)), ((), ())), preferred_element_type=F32)
        s_prev = s[:WINDOW] + no_prev_bias if r == 0 else s[:WINDOW]
        return jnp.where(from_prev, s_prev, s[WINDOW:])

    def attend(unit, s):
        r, kv, pair = unit
        tile = kv * pairs + pair
        sink = sinks[tile:tile + 1, :]
        m = jnp.maximum(jnp.max(s, axis=0, keepdims=True), sink)
        e = jnp.exp2(s - m)
        denom = jnp.sum(e, axis=0, keepdims=True) + jnp.exp2(sink - m)
        eb = e.astype(BF16)
        ezero = jnp.zeros_like(eb)
        e2 = jnp.concatenate([jnp.where(from_prev, eb, ezero),
                              jnp.where(from_prev, ezero, eb)], axis=0)
        vt = value_cols(r)[kv * HEAD_DIM:(kv + 1) * HEAD_DIM, :]
        o = jnp.dot(vt, e2, preferred_element_type=F32) / denom
        return [o[:, :WINDOW], o[:, WINDOW:]]

    pieces = []
    s_next = scores(units[0])
    for idx, unit in enumerate(units):
        s_cur = s_next
        if idx + 1 < len(units):
            s_next = scores(units[idx + 1])
        pieces += attend(unit, s_cur)
        if len(pieces) == SWA_Q_HEADS:
            r = unit[0]
            o_ref[r * WINDOW:(r + 1) * WINDOW, :] = (
                jnp.concatenate(pieces, axis=0).T.astype(BF16))
            pieces = []


def _swa(qa, kaw, vat, sinks, bsz, seq):
    ts = TOK_TILE
    per_b = seq // ts
    sub = ts // WINDOW
    prev_blk = lambda b, t: jnp.maximum((b * per_b + t) * sub - 1, 0)
    cur = lambda width: pl.BlockSpec((ts, width), lambda b, t: (b * per_b + t, 0))
    sink_rows = jnp.repeat(sinks.reshape(SWA_Q_HEADS // 2, 2), WINDOW, axis=1)
    return pl.pallas_call(
        _swa_kernel,
        out_shape=jax.ShapeDtypeStruct(qa.shape, BF16),
        grid=(bsz, per_b),
        in_specs=[cur(SWA_Q_COLS), cur(kaw.shape[1]),
                  pl.BlockSpec((WINDOW, kaw.shape[1]), lambda b, t: (prev_blk(b, t), 0)),
                  pl.BlockSpec((SWA_KV_COLS, ts), lambda b, t: (0, b * per_b + t)),
                  pl.BlockSpec((SWA_KV_COLS, WINDOW), lambda b, t: (0, prev_blk(b, t))),
                  _const_spec(sink_rows.shape)],
        out_specs=cur(SWA_Q_COLS),
        compiler_params=_cparams(2),
        name="swa",
    )(qa, kaw, kaw, vat, vat, sink_rows)


def _diff_kernel(lam_init, q_ref, k_ref, vt_ref, lamp_ref, g_ref, o_ref,
                 s0_ref, s1_ref, acc_ref):
    i = pl.program_id(2)
    tq = q_ref.shape[0]
    tk = tq
    q = q_ref[...]
    lane = lax.broadcasted_iota(jnp.int32, q.shape, 1)
    zero = jnp.zeros_like(q)
    qs = (jnp.where(lane < HEAD_DIM, q, zero), jnp.where(lane >= HEAD_DIM, q, zero))
    s_refs = (s0_ref, s1_ref)
    acc_ref[...] = jnp.zeros_like(acc_ref)

    def k_tile(j):
        return k_ref[pl.ds(pl.multiple_of(j * tk, tk), tk), :]

    def vt_tile(j):
        return vt_ref[:, pl.ds(pl.multiple_of(j * tk, tk), tk)]

    def scores(c, k, masked):
        s = lax.dot_general(k, qs[c], (((1,), (1,)), ((), ())), preferred_element_type=F32)
        if masked:
            key = lax.broadcasted_iota(jnp.int32, s.shape, 0)
            qry = lax.broadcasted_iota(jnp.int32, s.shape, 1)
            s = jnp.where(key <= qry, s, NEG_BIG)
        s_refs[c][...] = s
        return jnp.max(s, axis=0, keepdims=True)

    def consume(c, tile_max, m, l, vt):
        m_new = jnp.maximum(m, tile_max)
        alpha = jnp.exp2(m - m_new)
        p = jnp.exp2(s_refs[c][...] - m_new)
        l_new = alpha * l + jnp.sum(p, axis=0, keepdims=True)
        pv = jnp.dot(vt, p.astype(BF16), preferred_element_type=F32)
        acc_ref[c] = alpha * acc_ref[c] + pv
        return m_new, l_new

    def pair(j, j_next, carry, masked):
        m0, l0, m1, l1, max0 = carry
        k = k_tile(j)
        vt = vt_tile(j)
        max1 = scores(1, k, masked)
        m0, l0 = consume(0, max0, m0, l0, vt)
        max0 = scores(0, k_tile(j_next), False)
        m1, l1 = consume(1, max1, m1, l1, vt)
        return m0, l0, m1, l1, max0

    neg = jnp.full((1, tq), NEG_BIG, F32)
    zeros = jnp.zeros((1, tq), F32)
    max0 = scores(0, k_tile(i), True)
    carry = pair(i, 0, (neg, zeros, neg, zeros, max0), True)
    carry = lax.fori_loop(0, i, lambda j, cr: pair(j, j + 1, cr, False), carry)
    _, l0, _, l1, _ = carry

    lp = lamp_ref[...]
    lam = (jnp.exp(jnp.sum(lp[0:1] * lp[1:2], axis=-1, keepdims=True))
           - jnp.exp(jnp.sum(lp[2:3] * lp[3:4], axis=-1, keepdims=True)) + lam_init)
    o = acc_ref[0] / l0 - lam * (acc_ref[1] / l1)
    ms = jnp.mean(o * o, axis=0, keepdims=True)
    y = o * lax.rsqrt(ms + NORM_EPS) * g_ref[...] * (1.0 - lam_init)
    o_ref[...] = y.T.astype(BF16)


def _diff_attn(qd, kd, vdt, lam_params, g_diff, lam_init, bsz, seq):
    tq = DIFF_TILE
    nq = seq // tq
    return pl.pallas_call(
        functools.partial(_diff_kernel, lam_init),
        out_shape=jax.ShapeDtypeStruct(qd.shape, BF16),
        grid=(bsz, DIFF_HEADS, nq),
        in_specs=[
            pl.BlockSpec((tq, DIFF_VDIM), lambda b, h, i: (b * nq + i, h)),
            pl.BlockSpec((seq, DIFF_VDIM), lambda b, h, i: (b, h)),
            pl.BlockSpec((DIFF_VDIM, seq), lambda b, h, i: (h, b)),
            _const_spec((4, HEAD_DIM)),
            _const_spec((DIFF_VDIM, 1)),
        ],
        out_specs=pl.BlockSpec((tq, DIFF_VDIM), lambda b, h, i: (b * nq + i, h)),
        scratch_shapes=[pltpu.VMEM((tq, tq), F32), pltpu.VMEM((tq, tq), F32),
                        pltpu.VMEM((2, DIFF_VDIM, tq), F32)],
        compiler_params=_cparams(3),
        name="diff_attn",
    )(qd, kd, vdt, lam_params, g_diff.reshape(DIFF_VDIM, 1))


def _out_proj_kernel(a_ref, b_ref, x_ref, woa_ref, wob_ref, g_ref, wq_ref, x1_ref, qc_ref):
    y = (jnp.dot(a_ref[...], woa_ref[...], preferred_element_type=F32)
         + jnp.dot(b_ref[...], wob_ref[...], preferred_element_type=F32))
    x1 = x_ref[...] + y
    x1_ref[...] = x1
    hc = _rms(x1, g_ref[...]).astype(BF16)
    qc = jnp.dot(hc, wq_ref[...], preferred_element_type=F32) * (CROSS_HEAD_DIM ** -0.5)
    qc_ref[...] = qc.astype(BF16)


def _out_proj(out_a, out_b, x2d, w_out, g_cross, w_cq):
    n_tok = x2d.shape[0]
    ts = TOK_TILE
    tok = lambda width: pl.BlockSpec((ts, width), lambda i: (i, 0))
    half = SWA_Q_COLS
    return pl.pallas_call(
        _out_proj_kernel,
        out_shape=[jax.ShapeDtypeStruct((n_tok, D_MODEL), F32),
                   jax.ShapeDtypeStruct((n_tok, D_MODEL), BF16)],
        grid=(n_tok // ts,),
        in_specs=[tok(half), tok(half), tok(D_MODEL), _const_spec((half, D_MODEL)),
                  _const_spec((half, D_MODEL)), _const_spec((1, D_MODEL)),
                  _const_spec((D_MODEL, D_MODEL))],
        out_specs=[tok(D_MODEL), tok(D_MODEL)],
        compiler_params=_cparams(1),
        name="out_proj",
    )(out_a, out_b, x2d, w_out[:half].astype(BF16), w_out[half:].astype(BF16),
      g_cross.reshape(1, D_MODEL), w_cq.astype(BF16))


def _mem_kv_kernel(m_ref, g_ref, w_ref, k_ref, v_ref):
    hm = _rms(m_ref[...], g_ref[...]).astype(BF16)
    k_ref[...] = jnp.dot(hm, w_ref[:, :D_MODEL], preferred_element_type=F32).astype(BF16)
    v_ref[...] = jnp.dot(hm, w_ref[:, D_MODEL:], preferred_element_type=F32).astype(BF16)


def _mem_kv(mem2d, g_mem, w_ckv, bsz, mem_len):
    blk = pl.BlockSpec((mem_len, D_MODEL), lambda b: (b, 0))
    return pl.pallas_call(
        _mem_kv_kernel,
        out_shape=[jax.ShapeDtypeStruct(mem2d.shape, BF16)] * 2,
        grid=(bsz,),
        in_specs=[blk, _const_spec((1, D_MODEL)), _const_spec((D_MODEL, 2 * D_MODEL))],
        out_specs=[blk, blk],
        compiler_params=_cparams(1),
        name="mem_kv",
    )(mem2d, g_mem.reshape(1, D_MODEL), w_ckv.astype(BF16))


def _cross_kernel(q_ref, k_ref, v_ref, wo_ref, x_ref, o_ref):
    heads = []
    for h in range(CROSS_HEADS):
        cols = slice(h * CROSS_HEAD_DIM, (h + 1) * CROSS_HEAD_DIM)
        s = lax.dot_general(q_ref[:, cols], k_ref[:, cols], (((1,), (1,)), ((), ())),
                            preferred_element_type=F32)
        m = jnp.max(s, axis=-1, keepdims=True)
        e = jnp.exp(s - m)
        p = (e / jnp.sum(e, axis=-1, keepdims=True)).astype(BF16)
        heads.append(jnp.dot(p, v_ref[:, cols], preferred_element_type=F32).astype(BF16))
    o = jnp.concatenate(heads, axis=-1)
    o_ref[...] = x_ref[...] + jnp.dot(o, wo_ref[...], preferred_element_type=F32)


def _cross_attn(qc, kc, vc, w_co, x1, seq, mem_len):
    n_tok = x1.shape[0]
    ts = TOK_TILE
    per_b = seq // ts
    tok = pl.BlockSpec((ts, D_MODEL), lambda i: (i, 0))
    memblk = pl.BlockSpec((mem_len, D_MODEL), lambda i: (i // per_b, 0))
    return pl.pallas_call(
        _cross_kernel,
        out_shape=jax.ShapeDtypeStruct(x1.shape, F32),
        grid=(n_tok // ts,),
        in_specs=[tok, memblk, memblk, _const_spec((D_MODEL, D_MODEL)), tok],
        out_specs=tok,
        compiler_params=_cparams(1),
        name="cross_attn",
    )(qc, kc, vc, w_co.astype(BF16), x1)


def _mlp_kernel(final_norm, x_ref, g_ref, wu_ref, wd_ref, gf_ref, o_ref):
    x = x_ref[...]
    hb = _rms(x, g_ref[...]).astype(BF16)
    acc = x
    d_ff = wu_ref.shape[1]
    for c in range(d_ff // FF_CHUNK):
        cols = slice(c * FF_CHUNK, (c + 1) * FF_CHUNK)
        u = jnp.dot(hb, wu_ref[:, cols], preferred_element_type=F32)
        r = jnp.maximum(u, 0.0)
        acc = acc + jnp.dot((r * r).astype(BF16), wd_ref[cols, :], preferred_element_type=F32)
    o_ref[...] = _rms(acc, gf_ref[...]) if final_norm else acc


def _mlp(x2, g_mlp, w_up, w_down, g_final, final_norm):
    n_tok = x2.shape[0]
    ts = TOK_TILE
    tok = pl.BlockSpec((ts, D_MODEL), lambda i: (i, 0))
    return pl.pallas_call(
        functools.partial(_mlp_kernel, final_norm),
        out_shape=jax.ShapeDtypeStruct(x2.shape, F32),
        grid=(n_tok // ts,),
        in_specs=[tok, _const_spec((1, D_MODEL)), _const_spec(w_up.shape),
                  _const_spec(w_down.shape), _const_spec((1, D_MODEL))],
        out_specs=tok,
        compiler_params=_cparams(1),
        name="mlp",
    )(x2, g_mlp.reshape(1, D_MODEL), w_up.astype(BF16), w_down.astype(BF16),
      g_final.reshape(1, D_MODEL))


def kernel(x, mem, positions, g_mix, w_in, sinks, lambda_q1, lambda_k1, lambda_q2, lambda_k2,
           g_diff, w_out, g_cross, g_mem, w_cq, w_ckv, w_co, g_mlp, w_up, w_down, g_final):
    bsz, seq, _ = x.shape
    mem_len = mem.shape[1]
    depth = w_in.shape[0]
    cos_t, sin_t = _rope_tables(positions)
    xf = x.reshape(bsz * seq, D_MODEL)
    memf = mem.reshape(bsz * mem_len, D_MODEL)
    for l in range(depth):
        lam_init = 0.8 - 0.6 * math.exp(-0.3 * l)
        qa, kaw, vat, qd, kd, vdt = _in_proj(xf, g_mix[l], w_in[l], cos_t, sin_t)
        out_a = _swa(qa, kaw, vat, sinks[l], bsz, seq)
        lam_params = jnp.stack([lambda_q1[l], lambda_k1[l], lambda_q2[l], lambda_k2[l]])
        out_b = _diff_attn(qd, kd, vdt, lam_params, g_diff[l], lam_init, bsz, seq)
        x1, qc = _out_proj(out_a, out_b, xf, w_out[l], g_cross[l], w_cq[l])
        kc, vc = _mem_kv(memf, g_mem[l], w_ckv[l], bsz, mem_len)
        x2 = _cross_attn(qc, kc, vc, w_co[l], x1, seq, mem_len)
        xf = _mlp(x2, g_mlp[l], w_up[l], w_down[l], g_final, l == depth - 1)
    return xf.reshape(bsz, seq, D_MODEL)
```

```python
import functools
import math

import jax
import jax.numpy as jnp
from jax import lax
from jax.experimental import pallas as pl
from jax.experimental.pallas import tpu as pltpu

D_MODEL = 1024
HEAD_DIM = 64
SWA_Q_HEADS = 8
SWA_KV_HEADS = 2
SWA_GROUP = SWA_Q_HEADS // SWA_KV_HEADS
WINDOW = 128
DIFF_HEADS = 4
DIFF_VDIM = 2 * HEAD_DIM
SWA_Q_COLS = SWA_Q_HEADS * HEAD_DIM
SWA_KV_COLS = SWA_KV_HEADS * HEAD_DIM
DIFF_QK_COLS = DIFF_HEADS * 2 * HEAD_DIM
DIFF_V_COLS = DIFF_HEADS * DIFF_VDIM
CROSS_HEADS = 4
CROSS_HEAD_DIM = D_MODEL // CROSS_HEADS
ROPE_THETA = 10000.0
NORM_EPS = 1e-5

LANES = 128
TOK_TILE = 512
DIFF_TILE = 512
FF_CHUNK = 512
SWA_AHEAD = 10
VMEM_LIMIT = 56 * 1024 * 1024
NEG_BIG = -1e30
LOG2E = math.log2(math.e)

BF16 = jnp.bfloat16
F32 = jnp.float32


def _cparams(n_axes):
    return pltpu.CompilerParams(dimension_semantics=("arbitrary",) * n_axes,
                                vmem_limit_bytes=VMEM_LIMIT)


def _rms(x, g):
    ms = jnp.mean(x * x, axis=-1, keepdims=True)
    return x * lax.rsqrt(ms + NORM_EPS) * g


def _const_spec(shape):
    return pl.BlockSpec(shape, lambda *_: (0,) * len(shape))


ROPE_HALF = HEAD_DIM // 2
ROPE_GROUPS = LANES // ROPE_HALF


def _rope_table_kernel(pos_ref, invf_ref, sign_ref, cos_ref, sin_ref):
    pos = pos_ref[...].astype(F32)
    lane_group = lax.broadcasted_iota(jnp.int32, (pos.shape[0], LANES), 1) // ROPE_HALF
    pos_l = pos[:, ROPE_GROUPS - 1:ROPE_GROUPS]
    for g in range(ROPE_GROUPS - 2, -1, -1):
        pos_l = jnp.where(lane_group == g, pos[:, g:g + 1], pos_l)
    ang = pos_l * invf_ref[...]
    tables = ((jnp.cos(ang), None, cos_ref), (jnp.sin(ang), sign_ref[...], sin_ref))
    for g in range(ROPE_GROUPS):
        for tab, sign, out_ref in tables:
            t = jnp.where(lane_group == g, tab, 0.0)
            pair = t + pltpu.roll(t, ROPE_HALF, 1)
            tiled = pair + pltpu.roll(pair, 2 * ROPE_HALF, 1)
            out_ref[g] = tiled if sign is None else tiled * sign


def _rope_tables(positions):
    n_tok = positions.size
    inv_freq = ROPE_THETA ** (-jnp.arange(0, HEAD_DIM, 2, dtype=F32) / HEAD_DIM)
    quarter = n_tok // ROPE_GROUPS
    pos_cols = positions.reshape(ROPE_GROUPS, quarter).T
    invf = jnp.tile(inv_freq, ROPE_GROUPS).reshape(1, LANES)
    sign = jnp.tile(jnp.repeat(jnp.array([-1.0, 1.0], F32), ROPE_HALF), 2).reshape(1, LANES)
    blk = 1024
    out_spec = pl.BlockSpec((ROPE_GROUPS, blk, LANES), lambda i: (0, i, 0))
    cos_t, sin_t = pl.pallas_call(
        _rope_table_kernel,
        out_shape=[jax.ShapeDtypeStruct((ROPE_GROUPS, quarter, LANES), F32)] * 2,
        grid=(quarter // blk,),
        in_specs=[pl.BlockSpec((blk, ROPE_GROUPS), lambda i: (i, 0)), _const_spec((1, LANES)),
                  _const_spec((1, LANES))],
        out_specs=[out_spec, out_spec],
        compiler_params=_cparams(1),
        name="rope_tables",
    )(pos_cols, invf, sign)
    return cos_t.reshape(n_tok, LANES), sin_t.reshape(n_tok, LANES)


def _rope(p, cos, sin):
    width = p.shape[-1]
    lane = lax.broadcasted_iota(jnp.int32, p.shape, 1)
    first_half = (lane % HEAD_DIM) < (HEAD_DIM // 2)
    swapped = jnp.where(first_half,
                        pltpu.roll(p, width - HEAD_DIM // 2, 1),
                        pltpu.roll(p, HEAD_DIM // 2, 1))
    return p * cos + swapped * sin


def _in_proj_kernel(x_ref, g_ref, w_ref, wvt_ref, cos_ref, sin_ref,
                    qa_ref, kaw_ref, vat_ref, qd_ref, kd_ref, vdt_ref):
    hb = _rms(x_ref[...], g_ref[...]).astype(BF16)
    cos1 = cos_ref[...]
    sin1 = sin_ref[...]
    cos2 = jnp.concatenate([cos1, cos1], axis=-1)
    sin2 = jnp.concatenate([sin1, sin1], axis=-1)
    q_scale = HEAD_DIM ** -0.5 * LOG2E

    def proj(c0, width):
        return jnp.dot(hb, w_ref[:, c0:c0 + width], preferred_element_type=F32)

    for c in range(SWA_Q_COLS // 256):
        p = _rope(proj(c * 256, 256), cos2, sin2) * q_scale
        qa_ref[:, c * 256:(c + 1) * 256] = p.astype(BF16)
    off = SWA_Q_COLS
    ka = _rope(proj(off, SWA_KV_COLS), cos1, sin1)
    ka_swapped = pltpu.roll(ka, HEAD_DIM, 1)
    lane = lax.broadcasted_iota(jnp.int32, ka.shape, 1)
    for kv, rep in enumerate((jnp.where(lane < HEAD_DIM, ka, ka_swapped),
                              jnp.where(lane < HEAD_DIM, ka_swapped, ka))):
        rep = rep.astype(BF16)
        for half in range(2):
            c0 = kv * SWA_GROUP * HEAD_DIM + half * LANES
            kaw_ref[:, c0:c0 + LANES] = rep
    off += SWA_KV_COLS
    for c in range(DIFF_QK_COLS // 256):
        p = _rope(proj(off + c * 256, 256), cos2, sin2) * q_scale
        qd_ref[:, c * 256:(c + 1) * 256] = p.astype(BF16)
    off += DIFF_QK_COLS
    for c in range(DIFF_QK_COLS // 256):
        p = _rope(proj(off + c * 256, 256), cos2, sin2)
        kd_ref[:, c * 256:(c + 1) * 256] = p.astype(BF16)
    vt = lax.dot_general(wvt_ref[...], hb, (((1,), (1,)), ((), ())), preferred_element_type=F32)
    vat_ref[...] = vt[:SWA_KV_COLS].astype(BF16)
    vdt_ref[...] = vt[SWA_KV_COLS:].astype(BF16)


def _in_proj(x2d, g_mix, w_in, cos_t, sin_t):
    n_tok = x2d.shape[0]
    ts = TOK_TILE
    va_off = SWA_Q_COLS + SWA_KV_COLS
    qd_off = va_off + SWA_KV_COLS
    vd_off = qd_off + 2 * DIFF_QK_COLS
    w_main = jnp.concatenate([w_in[:, :va_off], w_in[:, qd_off:vd_off]], axis=1).astype(BF16)
    w_vt = jnp.concatenate([w_in[:, va_off:qd_off], w_in[:, vd_off:]], axis=1).T.astype(BF16)
    tok = lambda width: pl.BlockSpec((ts, width), lambda i: (i, 0))
    tok_t = lambda rows: pl.BlockSpec((rows, ts), lambda i: (0, i))
    kaw_cols = SWA_KV_HEADS * SWA_GROUP * HEAD_DIM
    out_shape = [
        jax.ShapeDtypeStruct((n_tok, SWA_Q_COLS), BF16),
        jax.ShapeDtypeStruct((n_tok, kaw_cols), BF16),
        jax.ShapeDtypeStruct((SWA_KV_COLS, n_tok), BF16),
        jax.ShapeDtypeStruct((n_tok, DIFF_QK_COLS), BF16),
        jax.ShapeDtypeStruct((n_tok, DIFF_QK_COLS), BF16),
        jax.ShapeDtypeStruct((DIFF_V_COLS, n_tok), BF16),
    ]
    return pl.pallas_call(
        _in_proj_kernel,
        out_shape=out_shape,
        grid=(n_tok // ts,),
        in_specs=[tok(D_MODEL), _const_spec((1, D_MODEL)), _const_spec(w_main.shape),
                  _const_spec(w_vt.shape), tok(LANES), tok(LANES)],
        out_specs=[tok(SWA_Q_COLS), tok(kaw_cols), tok_t(SWA_KV_COLS), tok(DIFF_QK_COLS),
                   tok(DIFF_QK_COLS), tok_t(DIFF_V_COLS)],
        compiler_params=_cparams(1),
        name="in_proj",
    )(x2d, g_mix.reshape(1, D_MODEL), w_main, w_vt, cos_t, sin_t)


def _swa_kernel(q_ref, kc_ref, kp_ref, vc_ref, vp_ref, sink_ref, o_ref):
    n_sub = q_ref.shape[0] // WINDOW
    group_cols = SWA_GROUP * HEAD_DIM
    kj = lax.broadcasted_iota(jnp.int32, (WINDOW, 2 * WINDOW), 0)
    qi = lax.broadcasted_iota(jnp.int32, (WINDOW, 2 * WINDOW), 1) % WINDOW
    from_prev = kj > qi
    no_prev_bias = jnp.where(pl.program_id(1) == 0, NEG_BIG, 0.0)
    qlane = lax.broadcasted_iota(jnp.int32, (WINDOW, group_cols), 1) // HEAD_DIM
    sinks = sink_ref[...] * LOG2E
    pairs = SWA_GROUP // 2
    units = [(r, kv, pair) for r in range(n_sub) for kv in range(SWA_KV_HEADS)
             for pair in range(pairs)]

    def key_rows(r):
        if r == 0:
            return jnp.concatenate([kp_ref[...], kc_ref[0:WINDOW, :]], axis=0)
        return kc_ref[(r - 1) * WINDOW:(r + 1) * WINDOW, :]

    def value_cols(r):
        if r == 0:
            return jnp.concatenate([vp_ref[...], vc_ref[:, 0:WINDOW]], axis=1)
        return vc_ref[:, (r - 1) * WINDOW:(r + 1) * WINDOW]

    def scores(unit):
        r, kv, pair = unit
        qg = q_ref[r * WINDOW:(r + 1) * WINDOW, kv * group_cols:(kv + 1) * group_cols]
        zero = jnp.zeros_like(qg)
        q2 = jnp.concatenate([jnp.where(qlane == 2 * pair + g, qg, zero) for g in range(2)],
                             axis=0)
        kw = key_rows(r)[:, kv * group_cols:(kv + 1) * group_cols]
        s = lax.dot_general(kw, q2, (((1,), (1,)), ((), ())), preferred_element_type=F32)
        s_prev = s[:WINDOW] + no_prev_bias if r == 0 else s[:WINDOW]
        return jnp.where(from_prev, s_prev, s[WINDOW:])

    def attend(unit, s):
        r, kv, pair = unit
        tile = kv * pairs + pair
        sink = sinks[tile:tile + 1, :]
        m = jnp.maximum(jnp.max(s, axis=0, keepdims=True), sink)
        e = jnp.exp2(s - m)
        denom = jnp.sum(e, axis=0, keepdims=True) + jnp.exp2(sink - m)
        eb = e.astype(BF16)
        ezero = jnp.zeros_like(eb)
        e2 = jnp.concatenate([jnp.where(from_prev, eb, ezero),
                              jnp.where(from_prev, ezero, eb)], axis=0)
        vt = value_cols(r)[kv * HEAD_DIM:(kv + 1) * HEAD_DIM, :]
        o = jnp.dot(vt, e2, preferred_element_type=F32) / denom
        return [o[:, :WINDOW], o[:, WINDOW:]]

    pieces = []
    pending = [scores(u) for u in units[:SWA_AHEAD]]
    for idx, unit in enumerate(units):
        if idx + SWA_AHEAD < len(units):
            pending.append(scores(units[idx + SWA_AHEAD]))
        pieces += attend(unit, pending.pop(0))
        if len(pieces) == SWA_Q_HEADS:
            r = unit[0]
            o_ref[r * WINDOW:(r + 1) * WINDOW, :] = (
                jnp.concatenate(pieces, axis=0).T.astype(BF16))
            pieces = []


def _swa(qa, kaw, vat, sinks, bsz, seq):
    ts = TOK_TILE
    per_b = seq // ts
    sub = ts // WINDOW
    prev_blk = lambda b, t: jnp.maximum((b * per_b + t) * sub - 1, 0)
    cur = lambda width: pl.BlockSpec((ts, width), lambda b, t: (b * per_b + t, 0))
    sink_rows = jnp.repeat(sinks.reshape(SWA_Q_HEADS // 2, 2), WINDOW, axis=1)
    return pl.pallas_call(
        _swa_kernel,
        out_shape=jax.ShapeDtypeStruct(qa.shape, BF16),
        grid=(bsz, per_b),
        in_specs=[cur(SWA_Q_COLS), cur(kaw.shape[1]),
                  pl.BlockSpec((WINDOW, kaw.shape[1]), lambda b, t: (prev_blk(b, t), 0)),
                  pl.BlockSpec((SWA_KV_COLS, ts), lambda b, t: (0, b * per_b + t)),
                  pl.BlockSpec((SWA_KV_COLS, WINDOW), lambda b, t: (0, prev_blk(b, t))),
                  _const_spec(sink_rows.shape)],
        out_specs=cur(SWA_Q_COLS),
        compiler_params=_cparams(2),
        name="swa",
    )(qa, kaw, kaw, vat, vat, sink_rows)


def _diff_kernel(lam_init, q_ref, k_ref, vt_ref, lamp_ref, g_ref, o_ref,
                 s0_ref, s1_ref, acc_ref):
    i = pl.program_id(2)
    tq = q_ref.shape[0]
    tk = tq
    q = q_ref[...]
    lane = lax.broadcasted_iota(jnp.int32, q.shape, 1)
    zero = jnp.zeros_like(q)
    qs = (jnp.where(lane < HEAD_DIM, q, zero), jnp.where(lane >= HEAD_DIM, q, zero))
    s_refs = (s0_ref, s1_ref)
    acc_ref[...] = jnp.zeros_like(acc_ref)

    def k_tile(j):
        return k_ref[pl.ds(pl.multiple_of(j * tk, tk), tk), :]

    def vt_tile(j):
        return vt_ref[:, pl.ds(pl.multiple_of(j * tk, tk), tk)]

    def scores(c, k, masked):
        s = lax.dot_general(k, qs[c], (((1,), (1,)), ((), ())), preferred_element_type=F32)
        if masked:
            key = lax.broadcasted_iota(jnp.int32, s.shape, 0)
            qry = lax.broadcasted_iota(jnp.int32, s.shape, 1)
            s = jnp.where(key <= qry, s, NEG_BIG)
        s_refs[c][...] = s
        return jnp.max(s, axis=0, keepdims=True)

    def consume(c, tile_max, m, l, vt):
        m_new = jnp.maximum(m, tile_max)
        alpha = jnp.exp2(m - m_new)
        p = jnp.exp2(s_refs[c][...] - m_new)
        l_new = alpha * l + jnp.sum(p, axis=0, keepdims=True)
        pv = jnp.dot(vt, p.astype(BF16), preferred_element_type=F32)
        acc_ref[c] = alpha * acc_ref[c] + pv
        return m_new, l_new

    def pair(j, j_next, carry, masked):
        m0, l0, m1, l1, max0 = carry
        k = k_tile(j)
        vt = vt_tile(j)
        max1 = scores(1, k, masked)
        m0, l0 = consume(0, max0, m0, l0, vt)
        max0 = scores(0, k_tile(j_next), False)
        m1, l1 = consume(1, max1, m1, l1, vt)
        return m0, l0, m1, l1, max0

    neg = jnp.full((1, tq), NEG_BIG, F32)
    zeros = jnp.zeros((1, tq), F32)
    max0 = scores(0, k_tile(i), True)
    carry = pair(i, 0, (neg, zeros, neg, zeros, max0), True)
    carry = lax.fori_loop(0, i, lambda j, cr: pair(j, j + 1, cr, False), carry)
    _, l0, _, l1, _ = carry

    lp = lamp_ref[...]
    lam = (jnp.exp(jnp.sum(lp[0:1] * lp[1:2], axis=-1, keepdims=True))
           - jnp.exp(jnp.sum(lp[2:3] * lp[3:4], axis=-1, keepdims=True)) + lam_init)
    o = acc_ref[0] / l0 - lam * (acc_ref[1] / l1)
    ms = jnp.mean(o * o, axis=0, keepdims=True)
    y = o * lax.rsqrt(ms + NORM_EPS) * g_ref[...] * (1.0 - lam_init)
    o_ref[...] = y.T.astype(BF16)


def _diff_attn(qd, kd, vdt, lam_params, g_diff, lam_init, bsz, seq):
    tq = DIFF_TILE
    nq = seq // tq
    return pl.pallas_call(
        functools.partial(_diff_kernel, lam_init),
        out_shape=jax.ShapeDtypeStruct(qd.shape, BF16),
        grid=(bsz, DIFF_HEADS, nq),
        in_specs=[
            pl.BlockSpec((tq, DIFF_VDIM), lambda b, h, i: (b * nq + i, h)),
            pl.BlockSpec((seq, DIFF_VDIM), lambda b, h, i: (b, h)),
            pl.BlockSpec((DIFF_VDIM, seq), lambda b, h, i: (h, b)),
            _const_spec((4, HEAD_DIM)),
            _const_spec((DIFF_VDIM, 1)),
        ],
        out_specs=pl.BlockSpec((tq, DIFF_VDIM), lambda b, h, i: (b * nq + i, h)),
        scratch_shapes=[pltpu.VMEM((tq, tq), F32), pltpu.VMEM((tq, tq), F32),
                        pltpu.VMEM((2, DIFF_VDIM, tq), F32)],
        compiler_params=_cparams(3),
        name="diff_attn",
    )(qd, kd, vdt, lam_params, g_diff.reshape(DIFF_VDIM, 1))


def _out_proj_kernel(a_ref, b_ref, x_ref, woa_ref, wob_ref, g_ref, wq_ref, x1_ref, qc_ref):
    y = (jnp.dot(a_ref[...], woa_ref[...], preferred_element_type=F32)
         + jnp.dot(b_ref[...], wob_ref[...], preferred_element_type=F32))
    x1 = x_ref[...] + y
    x1_ref[...] = x1
    hc = _rms(x1, g_ref[...]).astype(BF16)
    qc = jnp.dot(hc, wq_ref[...], preferred_element_type=F32) * (CROSS_HEAD_DIM ** -0.5 * LOG2E)
    qc_ref[...] = qc.astype(BF16)


def _out_proj(out_a, out_b, x2d, w_out, g_cross, w_cq):
    n_tok = x2d.shape[0]
    ts = TOK_TILE
    tok = lambda width: pl.BlockSpec((ts, width), lambda i: (i, 0))
    half = SWA_Q_COLS
    return pl.pallas_call(
        _out_proj_kernel,
        out_shape=[jax.ShapeDtypeStruct((n_tok, D_MODEL), F32),
                   jax.ShapeDtypeStruct((n_tok, D_MODEL), BF16)],
        grid=(n_tok // ts,),
        in_specs=[tok(half), tok(half), tok(D_MODEL), _const_spec((half, D_MODEL)),
                  _const_spec((half, D_MODEL)), _const_spec((1, D_MODEL)),
                  _const_spec((D_MODEL, D_MODEL))],
        out_specs=[tok(D_MODEL), tok(D_MODEL)],
        compiler_params=_cparams(1),
        name="out_proj",
    )(out_a, out_b, x2d, w_out[:half].astype(BF16), w_out[half:].astype(BF16),
      g_cross.reshape(1, D_MODEL), w_cq.astype(BF16))


def _mem_kv_kernel(m_ref, g_ref, w_ref, k_ref, v_ref):
    hm = _rms(m_ref[...], g_ref[...]).astype(BF16)
    k_ref[...] = jnp.dot(hm, w_ref[:, :D_MODEL], preferred_element_type=F32).astype(BF16)
    v_ref[...] = jnp.dot(hm, w_ref[:, D_MODEL:], preferred_element_type=F32).astype(BF16)


def _mem_kv(mem2d, g_mem, w_ckv, bsz, mem_len):
    blk = pl.BlockSpec((mem_len, D_MODEL), lambda b: (b, 0))
    return pl.pallas_call(
        _mem_kv_kernel,
        out_shape=[jax.ShapeDtypeStruct(mem2d.shape, BF16)] * 2,
        grid=(bsz,),
        in_specs=[blk, _const_spec((1, D_MODEL)), _const_spec((D_MODEL, 2 * D_MODEL))],
        out_specs=[blk, blk],
        compiler_params=_cparams(1),
        name="mem_kv",
    )(mem2d, g_mem.reshape(1, D_MODEL), w_ckv.astype(BF16))


def _cross_kernel(q_ref, k_ref, v_ref, wo_ref, x_ref, o_ref):
    def head_cols(h):
        return slice(h * CROSS_HEAD_DIM, (h + 1) * CROSS_HEAD_DIM)

    def scores(h):
        return lax.dot_general(q_ref[:, head_cols(h)], k_ref[:, head_cols(h)],
                               (((1,), (1,)), ((), ())), preferred_element_type=F32)

    def attend(h, s):
        m = jnp.max(s, axis=-1, keepdims=True)
        e = jnp.exp2(s - m)
        inv = 1.0 / jnp.sum(e, axis=-1, keepdims=True)
        pv = jnp.dot(e.astype(BF16), v_ref[:, head_cols(h)], preferred_element_type=F32)
        return (pv * inv).astype(BF16)

    heads = []
    s_next = scores(0)
    for h in range(CROSS_HEADS):
        s_cur = s_next
        if h + 1 < CROSS_HEADS:
            s_next = scores(h + 1)
        heads.append(attend(h, s_cur))
    o = jnp.concatenate(heads, axis=-1)
    o_ref[...] = x_ref[...] + jnp.dot(o, wo_ref[...], preferred_element_type=F32)


def _cross_attn(qc, kc, vc, w_co, x1, seq, mem_len):
    n_tok = x1.shape[0]
    ts = TOK_TILE
    per_b = seq // ts
    tok = pl.BlockSpec((ts, D_MODEL), lambda i: (i, 0))
    memblk = pl.BlockSpec((mem_len, D_MODEL), lambda i: (i // per_b, 0))
    return pl.pallas_call(
        _cross_kernel,
        out_shape=jax.ShapeDtypeStruct(x1.shape, F32),
        grid=(n_tok // ts,),
        in_specs=[tok, memblk, memblk, _const_spec((D_MODEL, D_MODEL)), tok],
        out_specs=tok,
        compiler_params=_cparams(1),
        name="cross_attn",
    )(qc, kc, vc, w_co.astype(BF16), x1)


def _mlp_kernel(final_norm, x_ref, g_ref, wu_ref, wd_ref, gf_ref, o_ref):
    x = x_ref[...]
    hb = _rms(x, g_ref[...]).astype(BF16)
    acc = x
    d_ff = wu_ref.shape[1]
    for c in range(d_ff // FF_CHUNK):
        cols = slice(c * FF_CHUNK, (c + 1) * FF_CHUNK)
        u = jnp.dot(hb, wu_ref[:, cols], preferred_element_type=F32)
        r = jnp.maximum(u, 0.0)
        acc = acc + jnp.dot((r * r).astype(BF16), wd_ref[cols, :], preferred_element_type=F32)
    o_ref[...] = _rms(acc, gf_ref[...]) if final_norm else acc


def _mlp(x2, g_mlp, w_up, w_down, g_final, final_norm):
    n_tok = x2.shape[0]
    ts = TOK_TILE
    tok = pl.BlockSpec((ts, D_MODEL), lambda i: (i, 0))
    return pl.pallas_call(
        functools.partial(_mlp_kernel, final_norm),
        out_shape=jax.ShapeDtypeStruct(x2.shape, F32),
        grid=(n_tok // ts,),
        in_specs=[tok, _const_spec((1, D_MODEL)), _const_spec(w_up.shape),
                  _const_spec(w_down.shape), _const_spec((1, D_MODEL))],
        out_specs=tok,
        compiler_params=_cparams(1),
        name="mlp",
    )(x2, g_mlp.reshape(1, D_MODEL), w_up.astype(BF16), w_down.astype(BF16),
      g_final.reshape(1, D_MODEL))


def kernel(x, mem, positions, g_mix, w_in, sinks, lambda_q1, lambda_k1, lambda_q2, lambda_k2,
           g_diff, w_out, g_cross, g_mem, w_cq, w_ckv, w_co, g_mlp, w_up, w_down, g_final):
    bsz, seq, _ = x.shape
    mem_len = mem.shape[1]
    depth = w_in.shape[0]
    cos_t, sin_t = _rope_tables(positions)
    xf = x.reshape(bsz * seq, D_MODEL)
    memf = mem.reshape(bsz * mem_len, D_MODEL)
    for l in range(depth):
        lam_init = 0.8 - 0.6 * math.exp(-0.3 * l)
        qa, kaw, vat, qd, kd, vdt = _in_proj(xf, g_mix[l], w_in[l], cos_t, sin_t)
        out_a = _swa(qa, kaw, vat, sinks[l], bsz, seq)
        lam_params = jnp.stack([lambda_q1[l], lambda_k1[l], lambda_q2[l], lambda_k2[l]])
        out_b = _diff_attn(qd, kd, vdt, lam_params, g_diff[l], lam_init, bsz, seq)
        x1, qc = _out_proj(out_a, out_b, xf, w_out[l], g_cross[l], w_cq[l])
        kc, vc = _mem_kv(memf, g_mem[l], w_ckv[l], bsz, mem_len)
        x2 = _cross_attn(qc, kc, vc, w_co[l], x1, seq, mem_len)
        xf = _mlp(x2, g_mlp[l], w_up[l], w_down[l], g_final, l == depth - 1)
    return xf.reshape(bsz, seq, D_MODEL)
```

```python
import functools
import math

import jax
import jax.numpy as jnp
from jax import lax
from jax.experimental import pallas as pl
from jax.experimental.pallas import tpu as pltpu

D_MODEL = 1024
HEAD_DIM = 64
SWA_Q_HEADS = 8
SWA_KV_HEADS = 2
SWA_GROUP = SWA_Q_HEADS // SWA_KV_HEADS
WINDOW = 128
DIFF_HEADS = 4
DIFF_VDIM = 2 * HEAD_DIM
SWA_Q_COLS = SWA_Q_HEADS * HEAD_DIM
SWA_KV_COLS = SWA_KV_HEADS * HEAD_DIM
DIFF_QK_COLS = DIFF_HEADS * 2 * HEAD_DIM
DIFF_V_COLS = DIFF_HEADS * DIFF_VDIM
CROSS_HEADS = 4
CROSS_HEAD_DIM = D_MODEL // CROSS_HEADS
ROPE_THETA = 10000.0
NORM_EPS = 1e-5

LANES = 128
TOK_TILE = 512
DIFF_TILE = 512
FF_CHUNK = 512
SWA_AHEAD = 10
VMEM_LIMIT = 56 * 1024 * 1024
NEG_BIG = -1e30
LOG2E = math.log2(math.e)

BF16 = jnp.bfloat16
F32 = jnp.float32


def _cparams(n_axes):
    return pltpu.CompilerParams(dimension_semantics=("arbitrary",) * n_axes,
                                vmem_limit_bytes=VMEM_LIMIT)


def _rms(x, g):
    ms = jnp.mean(x * x, axis=-1, keepdims=True)
    return x * lax.rsqrt(ms + NORM_EPS) * g


def _const_spec(shape):
    return pl.BlockSpec(shape, lambda *_: (0,) * len(shape))


ROPE_HALF = HEAD_DIM // 2
ROPE_GROUPS = LANES // ROPE_HALF


def _rope_table_kernel(pos_ref, invf_ref, sign_ref, cos_ref, sin_ref):
    pos = pos_ref[...].astype(F32)
    lane_group = lax.broadcasted_iota(jnp.int32, (pos.shape[0], LANES), 1) // ROPE_HALF
    pos_l = pos[:, ROPE_GROUPS - 1:ROPE_GROUPS]
    for g in range(ROPE_GROUPS - 2, -1, -1):
        pos_l = jnp.where(lane_group == g, pos[:, g:g + 1], pos_l)
    ang = pos_l * invf_ref[...]
    tables = ((jnp.cos(ang), None, cos_ref), (jnp.sin(ang), sign_ref[...], sin_ref))
    for g in range(ROPE_GROUPS):
        for tab, sign, out_ref in tables:
            t = jnp.where(lane_group == g, tab, 0.0)
            pair = t + pltpu.roll(t, ROPE_HALF, 1)
            tiled = pair + pltpu.roll(pair, 2 * ROPE_HALF, 1)
            out_ref[g] = tiled if sign is None else tiled * sign


def _rope_tables(positions):
    n_tok = positions.size
    inv_freq = ROPE_THETA ** (-jnp.arange(0, HEAD_DIM, 2, dtype=F32) / HEAD_DIM)
    quarter = n_tok // ROPE_GROUPS
    pos_cols = positions.reshape(ROPE_GROUPS, quarter).T
    invf = jnp.tile(inv_freq, ROPE_GROUPS).reshape(1, LANES)
    sign = jnp.tile(jnp.repeat(jnp.array([-1.0, 1.0], F32), ROPE_HALF), 2).reshape(1, LANES)
    blk = 1024
    out_spec = pl.BlockSpec((ROPE_GROUPS, blk, LANES), lambda i: (0, i, 0))
    cos_t, sin_t = pl.pallas_call(
        _rope_table_kernel,
        out_shape=[jax.ShapeDtypeStruct((ROPE_GROUPS, quarter, LANES), F32)] * 2,
        grid=(quarter // blk,),
        in_specs=[pl.BlockSpec((blk, ROPE_GROUPS), lambda i: (i, 0)), _const_spec((1, LANES)),
                  _const_spec((1, LANES))],
        out_specs=[out_spec, out_spec],
        compiler_params=_cparams(1),
        name="rope_tables",
    )(pos_cols, invf, sign)
    return cos_t.reshape(n_tok, LANES), sin_t.reshape(n_tok, LANES)


def _rope(p, cos, sin):
    width = p.shape[-1]
    lane = lax.broadcasted_iota(jnp.int32, p.shape, 1)
    first_half = (lane % HEAD_DIM) < (HEAD_DIM // 2)
    swapped = jnp.where(first_half,
                        pltpu.roll(p, width - HEAD_DIM // 2, 1),
                        pltpu.roll(p, HEAD_DIM // 2, 1))
    return p * cos + swapped * sin


def _in_proj_kernel(x_ref, g_ref, w_ref, wvt_ref, cos_ref, sin_ref,
                    qa_ref, kaw_ref, vat_ref, qd_ref, kd_ref, vdt_ref):
    hb = _rms(x_ref[...], g_ref[...]).astype(BF16)
    cos1 = cos_ref[...]
    sin1 = sin_ref[...]
    cos2 = jnp.concatenate([cos1, cos1], axis=-1)
    sin2 = jnp.concatenate([sin1, sin1], axis=-1)
    q_scale = HEAD_DIM ** -0.5 * LOG2E

    def proj(c0, width):
        return jnp.dot(hb, w_ref[:, c0:c0 + width], preferred_element_type=F32)

    for c in range(SWA_Q_COLS // 256):
        p = _rope(proj(c * 256, 256), cos2, sin2) * q_scale
        qa_ref[:, c * 256:(c + 1) * 256] = p.astype(BF16)
    off = SWA_Q_COLS
    ka = _rope(proj(off, SWA_KV_COLS), cos1, sin1)
    ka_swapped = pltpu.roll(ka, HEAD_DIM, 1)
    lane = lax.broadcasted_iota(jnp.int32, ka.shape, 1)
    for kv, rep in enumerate((jnp.where(lane < HEAD_DIM, ka, ka_swapped),
                              jnp.where(lane < HEAD_DIM, ka_swapped, ka))):
        rep = rep.astype(BF16)
        for half in range(2):
            c0 = kv * SWA_GROUP * HEAD_DIM + half * LANES
            kaw_ref[:, c0:c0 + LANES] = rep
    off += SWA_KV_COLS
    for c in range(DIFF_QK_COLS // 256):
        p = _rope(proj(off + c * 256, 256), cos2, sin2) * q_scale
        qd_ref[:, c * 256:(c + 1) * 256] = p.astype(BF16)
    off += DIFF_QK_COLS
    for c in range(DIFF_QK_COLS // 256):
        p = _rope(proj(off + c * 256, 256), cos2, sin2)
        kd_ref[:, c * 256:(c + 1) * 256] = p.astype(BF16)
    vt = lax.dot_general(wvt_ref[...], hb, (((1,), (1,)), ((), ())), preferred_element_type=F32)
    vat_ref[...] = vt[:SWA_KV_COLS].astype(BF16)
    vdt_ref[...] = vt[SWA_KV_COLS:].astype(BF16)


def _in_proj(x2d, g_mix, w_in, cos_t, sin_t):
    n_tok = x2d.shape[0]
    ts = TOK_TILE
    va_off = SWA_Q_COLS + SWA_KV_COLS
    qd_off = va_off + SWA_KV_COLS
    vd_off = qd_off + 2 * DIFF_QK_COLS
    w_main = jnp.concatenate([w_in[:, :va_off], w_in[:, qd_off:vd_off]], axis=1).astype(BF16)
    w_vt = jnp.concatenate([w_in[:, va_off:qd_off], w_in[:, vd_off:]], axis=1).T.astype(BF16)
    tok = lambda width: pl.BlockSpec((ts, width), lambda i: (i, 0))
    tok_t = lambda rows: pl.BlockSpec((rows, ts), lambda i: (0, i))
    kaw_cols = SWA_KV_HEADS * SWA_GROUP * HEAD_DIM
    out_shape = [
        jax.ShapeDtypeStruct((n_tok, SWA_Q_COLS), BF16),
        jax.ShapeDtypeStruct((n_tok, kaw_cols), BF16),
        jax.ShapeDtypeStruct((SWA_KV_COLS, n_tok), BF16),
        jax.ShapeDtypeStruct((n_tok, DIFF_QK_COLS), BF16),
        jax.ShapeDtypeStruct((n_tok, DIFF_QK_COLS), BF16),
        jax.ShapeDtypeStruct((DIFF_V_COLS, n_tok), BF16),
    ]
    return pl.pallas_call(
        _in_proj_kernel,
        out_shape=out_shape,
        grid=(n_tok // ts,),
        in_specs=[tok(D_MODEL), _const_spec((1, D_MODEL)), _const_spec(w_main.shape),
                  _const_spec(w_vt.shape), tok(LANES), tok(LANES)],
        out_specs=[tok(SWA_Q_COLS), tok(kaw_cols), tok_t(SWA_KV_COLS), tok(DIFF_QK_COLS),
                   tok(DIFF_QK_COLS), tok_t(DIFF_V_COLS)],
        compiler_params=_cparams(1),
        name="in_proj",
    )(x2d, g_mix.reshape(1, D_MODEL), w_main, w_vt, cos_t, sin_t)


def _swa_kernel(q_ref, kc_ref, kp_ref, vc_ref, vp_ref, sink_ref, o_ref):
    n_sub = q_ref.shape[0] // WINDOW
    group_cols = SWA_GROUP * HEAD_DIM
    kj = lax.broadcasted_iota(jnp.int32, (WINDOW, 2 * WINDOW), 0)
    qi = lax.broadcasted_iota(jnp.int32, (WINDOW, 2 * WINDOW), 1) % WINDOW
    from_prev = kj > qi
    no_prev_bias = jnp.where(pl.program_id(1) == 0, NEG_BIG, 0.0)
    qlane = lax.broadcasted_iota(jnp.int32, (WINDOW, group_cols), 1) // HEAD_DIM
    sinks = sink_ref[...] * LOG2E
    pairs = SWA_GROUP // 2
    units = [(r, kv, pair) for r in range(n_sub) for kv in range(SWA_KV_HEADS)
             for pair in range(pairs)]

    def key_rows(r):
        if r == 0:
            return jnp.concatenate([kp_ref[...], kc_ref[0:WINDOW, :]], axis=0)
        return kc_ref[(r - 1) * WINDOW:(r + 1) * WINDOW, :]

    def value_cols(r):
        if r == 0:
            return jnp.concatenate([vp_ref[...], vc_ref[:, 0:WINDOW]], axis=1)
        return vc_ref[:, (r - 1) * WINDOW:(r + 1) * WINDOW]

    def scores(unit):
        r, kv, pair = unit
        qg = q_ref[r * WINDOW:(r + 1) * WINDOW, kv * group_cols:(kv + 1) * group_cols]
        zero = jnp.zeros_like(qg)
        q2 = jnp.concatenate([jnp.where(qlane == 2 * pair + g, qg, zero) for g in range(2)],
                             axis=0)
        kw = key_rows(r)[:, kv * group_cols:(kv + 1) * group_cols]
        s = lax.dot_general(kw, q2, (((1,), (1,)), ((), ())), preferred_element_type=F32)
        s_prev = s[:WINDOW] + no_prev_bias if r == 0 else s[:WINDOW]
        return jnp.where(from_prev, s_prev, s[WINDOW:])

    def attend(unit, s):
        r, kv, pair = unit
        tile = kv * pairs + pair
        sink = sinks[tile:tile + 1, :]
        m = jnp.maximum(jnp.max(s, axis=0, keepdims=True), sink)
        e = jnp.exp2(s - m)
        denom = jnp.sum(e, axis=0, keepdims=True) + jnp.exp2(sink - m)
        eb = e.astype(BF16)
        ezero = jnp.zeros_like(eb)
        e2 = jnp.concatenate([jnp.where(from_prev, eb, ezero),
                              jnp.where(from_prev, ezero, eb)], axis=0)
        vt = value_cols(r)[kv * HEAD_DIM:(kv + 1) * HEAD_DIM, :]
        o = jnp.dot(vt, e2, preferred_element_type=F32) / denom
        return [o[:, :WINDOW], o[:, WINDOW:]]

    pieces = []
    pending = [scores(u) for u in units[:SWA_AHEAD]]
    for idx, unit in enumerate(units):
        if idx + SWA_AHEAD < len(units):
            pending.append(scores(units[idx + SWA_AHEAD]))
        pieces += attend(unit, pending.pop(0))
        if len(pieces) == SWA_Q_HEADS:
            r = unit[0]
            o_ref[r * WINDOW:(r + 1) * WINDOW, :] = (
                jnp.concatenate(pieces, axis=0).T.astype(BF16))
            pieces = []


def _swa(qa, kaw, vat, sinks, bsz, seq):
    ts = TOK_TILE
    per_b = seq // ts
    sub = ts // WINDOW
    prev_blk = lambda b, t: jnp.maximum((b * per_b + t) * sub - 1, 0)
    cur = lambda width: pl.BlockSpec((ts, width), lambda b, t: (b * per_b + t, 0))
    sink_rows = jnp.repeat(sinks.reshape(SWA_Q_HEADS // 2, 2), WINDOW, axis=1)
    return pl.pallas_call(
        _swa_kernel,
        out_shape=jax.ShapeDtypeStruct(qa.shape, BF16),
        grid=(bsz, per_b),
        in_specs=[cur(SWA_Q_COLS), cur(kaw.shape[1]),
                  pl.BlockSpec((WINDOW, kaw.shape[1]), lambda b, t: (prev_blk(b, t), 0)),
                  pl.BlockSpec((SWA_KV_COLS, ts), lambda b, t: (0, b * per_b + t)),
                  pl.BlockSpec((SWA_KV_COLS, WINDOW), lambda b, t: (0, prev_blk(b, t))),
                  _const_spec(sink_rows.shape)],
        out_specs=cur(SWA_Q_COLS),
        compiler_params=_cparams(2),
        name="swa",
    )(qa, kaw, kaw, vat, vat, sink_rows)


def _diff_kernel(lam_init, q_ref, k_ref, vt_ref, lamp_ref, g_ref, o_ref,
                 s0_ref, s1_ref, acc_ref):
    pair_idx = pl.program_id(2)
    th = DIFF_TILE
    s_refs = (s0_ref, s1_ref)
    acc_ref[...] = jnp.zeros_like(acc_ref)

    lane = lax.broadcasted_iota(jnp.int32, (th, DIFF_VDIM), 1)
    qs = []
    for half in range(2):
        q = q_ref[half * th:(half + 1) * th, :]
        zero = jnp.zeros_like(q)
        qs.append((jnp.where(lane < HEAD_DIM, q, zero), jnp.where(lane >= HEAD_DIM, q, zero)))

    def k_tile(j):
        return k_ref[pl.ds(pl.multiple_of(j * th, th), th), :]

    def vt_tile(j):
        return vt_ref[:, pl.ds(pl.multiple_of(j * th, th), th)]

    def start(unit, slot):
        j, half, c, masked = unit
        s = lax.dot_general(k_tile(j), qs[half][c], (((1,), (1,)), ((), ())),
                            preferred_element_type=F32)
        if masked:
            key = lax.broadcasted_iota(jnp.int32, s.shape, 0)
            qry = lax.broadcasted_iota(jnp.int32, s.shape, 1)
            s = jnp.where(key <= qry, s, NEG_BIG)
        s_refs[slot][...] = s
        return jnp.max(s, axis=0, keepdims=True)

    def consume(unit, slot, tile_max, stats):
        j, half, c, _ = unit
        idx = 2 * half + c
        m, l = stats[idx]
        m_new = jnp.maximum(m, tile_max)
        alpha = jnp.exp2(m - m_new)
        p = jnp.exp2(s_refs[slot][...] - m_new)
        l_new = alpha * l + jnp.sum(p, axis=0, keepdims=True)
        pv = jnp.dot(vt_tile(j), p.astype(BF16), preferred_element_type=F32)
        acc_ref[idx] = alpha * acc_ref[idx] + pv
        return stats[:idx] + ((m_new, l_new),) + stats[idx + 1:]

    def run(units, following, carry):
        stats, tile_max = carry
        for n, unit in enumerate(units):
            nxt = units[n + 1] if n + 1 < len(units) else following
            next_max = start(nxt, (n + 1) % 2) if nxt is not None else None
            stats = consume(unit, n % 2, tile_max, stats)
            tile_max = next_max
        return stats, tile_max

    def key_tile_units(j, masked_lower, masked_upper):
        units = [(j, 1, 0, masked_upper), (j, 1, 1, masked_upper)]
        if masked_lower is not None:
            units += [(j, 0, 0, masked_lower), (j, 0, 1, masked_lower)]
        return units

    neg = jnp.full((1, th), NEG_BIG, F32)
    zeros = jnp.zeros((1, th), F32)
    first = (0, 1, 0, False)
    carry = (((neg, zeros),) * 4, start(first, 0))
    carry = lax.fori_loop(
        0, 2 * pair_idx,
        lambda j, cr: run(key_tile_units(j, False, False), (j + 1, 1, 0, False), cr), carry)
    j0 = 2 * pair_idx
    tail = key_tile_units(j0, True, False) + key_tile_units(j0 + 1, None, True)
    stats, _ = run(tail, None, carry)

    lp = lamp_ref[...]
    lam = (jnp.exp(jnp.sum(lp[0:1] * lp[1:2], axis=-1, keepdims=True))
           - jnp.exp(jnp.sum(lp[2:3] * lp[3:4], axis=-1, keepdims=True)) + lam_init)
    for half in range(2):
        (_, l0), (_, l1) = stats[2 * half], stats[2 * half + 1]
        o = acc_ref[2 * half] * (1.0 / l0) - acc_ref[2 * half + 1] * (lam / l1)
        ms = jnp.mean(o * o, axis=0, keepdims=True)
        y = o * lax.rsqrt(ms + NORM_EPS) * g_ref[...] * (1.0 - lam_init)
        o_ref[half * th:(half + 1) * th, :] = y.T.astype(BF16)


def _diff_attn(qd, kd, vdt, lam_params, g_diff, lam_init, bsz, seq):
    th = DIFF_TILE
    tq = 2 * th
    nq = seq // tq
    return pl.pallas_call(
        functools.partial(_diff_kernel, lam_init),
        out_shape=jax.ShapeDtypeStruct(qd.shape, BF16),
        grid=(bsz, DIFF_HEADS, nq),
        in_specs=[
            pl.BlockSpec((tq, DIFF_VDIM), lambda b, h, i: (b * nq + i, h)),
            pl.BlockSpec((seq, DIFF_VDIM), lambda b, h, i: (b, h)),
            pl.BlockSpec((DIFF_VDIM, seq), lambda b, h, i: (h, b)),
            _const_spec((4, HEAD_DIM)),
            _const_spec((DIFF_VDIM, 1)),
        ],
        out_specs=pl.BlockSpec((tq, DIFF_VDIM), lambda b, h, i: (b * nq + i, h)),
        scratch_shapes=[pltpu.VMEM((th, th), F32), pltpu.VMEM((th, th), F32),
                        pltpu.VMEM((4, DIFF_VDIM, th), F32)],
        compiler_params=_cparams(3),
        name="diff_attn",
    )(qd, kd, vdt, lam_params, g_diff.reshape(DIFF_VDIM, 1))


def _out_proj_kernel(a_ref, b_ref, x_ref, woa_ref, wob_ref, g_ref, wq_ref, x1_ref, qc_ref):
    y = (jnp.dot(a_ref[...], woa_ref[...], preferred_element_type=F32)
         + jnp.dot(b_ref[...], wob_ref[...], preferred_element_type=F32))
    x1 = x_ref[...] + y
    x1_ref[...] = x1
    hc = _rms(x1, g_ref[...]).astype(BF16)
    qc = jnp.dot(hc, wq_ref[...], preferred_element_type=F32) * (CROSS_HEAD_DIM ** -0.5 * LOG2E)
    qc_ref[...] = qc.astype(BF16)


def _out_proj(out_a, out_b, x2d, w_out, g_cross, w_cq):
    n_tok = x2d.shape[0]
    ts = TOK_TILE
    tok = lambda width: pl.BlockSpec((ts, width), lambda i: (i, 0))
    half = SWA_Q_COLS
    return pl.pallas_call(
        _out_proj_kernel,
        out_shape=[jax.ShapeDtypeStruct((n_tok, D_MODEL), F32),
                   jax.ShapeDtypeStruct((n_tok, D_MODEL), BF16)],
        grid=(n_tok // ts,),
        in_specs=[tok(half), tok(half), tok(D_MODEL), _const_spec((half, D_MODEL)),
                  _const_spec((half, D_MODEL)), _const_spec((1, D_MODEL)),
                  _const_spec((D_MODEL, D_MODEL))],
        out_specs=[tok(D_MODEL), tok(D_MODEL)],
        compiler_params=_cparams(1),
        name="out_proj",
    )(out_a, out_b, x2d, w_out[:half].astype(BF16), w_out[half:].astype(BF16),
      g_cross.reshape(1, D_MODEL), w_cq.astype(BF16))


def _mem_kv_kernel(m_ref, g_ref, w_ref, k_ref, v_ref):
    hm = _rms(m_ref[...], g_ref[...]).astype(BF16)
    k_ref[...] = jnp.dot(hm, w_ref[:, :D_MODEL], preferred_element_type=F32).astype(BF16)
    v_ref[...] = jnp.dot(hm, w_ref[:, D_MODEL:], preferred_element_type=F32).astype(BF16)


def _mem_kv(mem2d, g_mem, w_ckv, bsz, mem_len):
    blk = pl.BlockSpec((mem_len, D_MODEL), lambda b: (b, 0))
    return pl.pallas_call(
        _mem_kv_kernel,
        out_shape=[jax.ShapeDtypeStruct(mem2d.shape, BF16)] * 2,
        grid=(bsz,),
        in_specs=[blk, _const_spec((1, D_MODEL)), _const_spec((D_MODEL, 2 * D_MODEL))],
        out_specs=[blk, blk],
        compiler_params=_cparams(1),
        name="mem_kv",
    )(mem2d, g_mem.reshape(1, D_MODEL), w_ckv.astype(BF16))


def _cross_kernel(q_ref, k_ref, v_ref, wo_ref, x_ref, o_ref):
    def head_cols(h):
        return slice(h * CROSS_HEAD_DIM, (h + 1) * CROSS_HEAD_DIM)

    def scores(h):
        return lax.dot_general(q_ref[:, head_cols(h)], k_ref[:, head_cols(h)],
                               (((1,), (1,)), ((), ())), preferred_element_type=F32)

    def attend(h, s):
        m = jnp.max(s, axis=-1, keepdims=True)
        e = jnp.exp2(s - m)
        inv = 1.0 / jnp.sum(e, axis=-1, keepdims=True)
        pv = jnp.dot(e.astype(BF16), v_ref[:, head_cols(h)], preferred_element_type=F32)
        return (pv * inv).astype(BF16)

    heads = []
    s_next = scores(0)
    for h in range(CROSS_HEADS):
        s_cur = s_next
        if h + 1 < CROSS_HEADS:
            s_next = scores(h + 1)
        heads.append(attend(h, s_cur))
    o = jnp.concatenate(heads, axis=-1)
    o_ref[...] = x_ref[...] + jnp.dot(o, wo_ref[...], preferred_element_type=F32)


def _cross_attn(qc, kc, vc, w_co, x1, seq, mem_len):
    n_tok = x1.shape[0]
    ts = TOK_TILE
    per_b = seq // ts
    tok = pl.BlockSpec((ts, D_MODEL), lambda i: (i, 0))
    memblk = pl.BlockSpec((mem_len, D_MODEL), lambda i: (i // per_b, 0))
    return pl.pallas_call(
        _cross_kernel,
        out_shape=jax.ShapeDtypeStruct(x1.shape, F32),
        grid=(n_tok // ts,),
        in_specs=[tok, memblk, memblk, _const_spec((D_MODEL, D_MODEL)), tok],
        out_specs=tok,
        compiler_params=_cparams(1),
        name="cross_attn",
    )(qc, kc, vc, w_co.astype(BF16), x1)


def _mlp_kernel(final_norm, x_ref, g_ref, wu_ref, wd_ref, gf_ref, o_ref):
    x = x_ref[...]
    hb = _rms(x, g_ref[...]).astype(BF16)
    acc = x
    d_ff = wu_ref.shape[1]
    for c in range(d_ff // FF_CHUNK):
        cols = slice(c * FF_CHUNK, (c + 1) * FF_CHUNK)
        u = jnp.dot(hb, wu_ref[:, cols], preferred_element_type=F32)
        r = jnp.maximum(u, 0.0)
        acc = acc + jnp.dot((r * r).astype(BF16), wd_ref[cols, :], preferred_element_type=F32)
    o_ref[...] = _rms(acc, gf_ref[...]) if final_norm else acc


def _mlp(x2, g_mlp, w_up, w_down, g_final, final_norm):
    n_tok = x2.shape[0]
    ts = TOK_TILE
    tok = pl.BlockSpec((ts, D_MODEL), lambda i: (i, 0))
    return pl.pallas_call(
        functools.partial(_mlp_kernel, final_norm),
        out_shape=jax.ShapeDtypeStruct(x2.shape, F32),
        grid=(n_tok // ts,),
        in_specs=[tok, _const_spec((1, D_MODEL)), _const_spec(w_up.shape),
                  _const_spec(w_down.shape), _const_spec((1, D_MODEL))],
        out_specs=tok,
        compiler_params=_cparams(1),
        name="mlp",
    )(x2, g_mlp.reshape(1, D_MODEL), w_up.astype(BF16), w_down.astype(BF16),
      g_final.reshape(1, D_MODEL))


def kernel(x, mem, positions, g_mix, w_in, sinks, lambda_q1, lambda_k1, lambda_q2, lambda_k2,
           g_diff, w_out, g_cross, g_mem, w_cq, w_ckv, w_co, g_mlp, w_up, w_down, g_final):
    bsz, seq, _ = x.shape
    mem_len = mem.shape[1]
    depth = w_in.shape[0]
    cos_t, sin_t = _rope_tables(positions)
    xf = x.reshape(bsz * seq, D_MODEL)
    memf = mem.reshape(bsz * mem_len, D_MODEL)
    for l in range(depth):
        lam_init = 0.8 - 0.6 * math.exp(-0.3 * l)
        qa, kaw, vat, qd, kd, vdt = _in_proj(xf, g_mix[l], w_in[l], cos_t, sin_t)
        out_a = _swa(qa, kaw, vat, sinks[l], bsz, seq)
        lam_params = jnp.stack([lambda_q1[l], lambda_k1[l], lambda_q2[l], lambda_k2[l]])
        out_b = _diff_attn(qd, kd, vdt, lam_params, g_diff[l], lam_init, bsz, seq)
        x1, qc = _out_proj(out_a, out_b, xf, w_out[l], g_cross[l], w_cq[l])
        kc, vc = _mem_kv(memf, g_mem[l], w_ckv[l], bsz, mem_len)
        x2 = _cross_attn(qc, kc, vc, w_co[l], x1, seq, mem_len)
        xf = _mlp(x2, g_mlp[l], w_up[l], w_down[l], g_final, l == depth - 1)
    return xf.reshape(bsz, seq, D_MODEL)
```

```python
import functools
import math

import jax
import jax.numpy as jnp
from jax import lax
from jax.experimental import pallas as pl
from jax.experimental.pallas import tpu as pltpu

D_MODEL = 1024
HEAD_DIM = 64
SWA_Q_HEADS = 8
SWA_KV_HEADS = 2
SWA_GROUP = SWA_Q_HEADS // SWA_KV_HEADS
WINDOW = 128
DIFF_HEADS = 4
DIFF_VDIM = 2 * HEAD_DIM
DIFF_VPAD = DIFF_VDIM + 16
SWA_Q_COLS = SWA_Q_HEADS * HEAD_DIM
SWA_KV_COLS = SWA_KV_HEADS * HEAD_DIM
DIFF_QK_COLS = DIFF_HEADS * 2 * HEAD_DIM
DIFF_V_COLS = DIFF_HEADS * DIFF_VDIM
CROSS_HEADS = 4
CROSS_HEAD_DIM = D_MODEL // CROSS_HEADS
ROPE_THETA = 10000.0
NORM_EPS = 1e-5

LANES = 128
TOK_TILE = 1024
SWA_TILE = 512
DIFF_TILE = 512
DIFF_GROUP = 4
FF_CHUNK = 512
SWA_AHEAD = 10
VMEM_LIMIT = 56 * 1024 * 1024
NEG_BIG = -1e30
LOG2E = math.log2(math.e)

BF16 = jnp.bfloat16
F32 = jnp.float32


def _cparams(n_axes):
    return pltpu.CompilerParams(dimension_semantics=("arbitrary",) * n_axes,
                                vmem_limit_bytes=VMEM_LIMIT)


def _rms(x, g):
    ms = jnp.mean(x * x, axis=-1, keepdims=True)
    return x * lax.rsqrt(ms + NORM_EPS) * g


def _const_spec(shape):
    return pl.BlockSpec(shape, lambda *_: (0,) * len(shape), pipeline_mode=pl.Buffered(1))


ROPE_HALF = HEAD_DIM // 2
ROPE_GROUPS = LANES // ROPE_HALF


def _rope_table_kernel(pos_ref, invf_ref, sign_ref, cos_ref, sin_ref):
    pos = pos_ref[...].astype(F32)
    lane_group = lax.broadcasted_iota(jnp.int32, (pos.shape[0], LANES), 1) // ROPE_HALF
    pos_l = pos[:, ROPE_GROUPS - 1:ROPE_GROUPS]
    for g in range(ROPE_GROUPS - 2, -1, -1):
        pos_l = jnp.where(lane_group == g, pos[:, g:g + 1], pos_l)
    ang = pos_l * invf_ref[...]
    tables = ((jnp.cos(ang), None, cos_ref), (jnp.sin(ang), sign_ref[...], sin_ref))
    for g in range(ROPE_GROUPS):
        for tab, sign, out_ref in tables:
            t = jnp.where(lane_group == g, tab, 0.0)
            pair = t + pltpu.roll(t, ROPE_HALF, 1)
            tiled = pair + pltpu.roll(pair, 2 * ROPE_HALF, 1)
            out_ref[g] = tiled if sign is None else tiled * sign


def _rope_tables(positions):
    n_tok = positions.size
    inv_freq = ROPE_THETA ** (-jnp.arange(0, HEAD_DIM, 2, dtype=F32) / HEAD_DIM)
    quarter = n_tok // ROPE_GROUPS
    pos_cols = positions.reshape(ROPE_GROUPS, quarter).T
    invf = jnp.tile(inv_freq, ROPE_GROUPS).reshape(1, LANES)
    sign = jnp.tile(jnp.repeat(jnp.array([-1.0, 1.0], F32), ROPE_HALF), 2).reshape(1, LANES)
    blk = 1024
    out_spec = pl.BlockSpec((ROPE_GROUPS, blk, LANES), lambda i: (0, i, 0))
    cos_t, sin_t = pl.pallas_call(
        _rope_table_kernel,
        out_shape=[jax.ShapeDtypeStruct((ROPE_GROUPS, quarter, LANES), F32)] * 2,
        grid=(quarter // blk,),
        in_specs=[pl.BlockSpec((blk, ROPE_GROUPS), lambda i: (i, 0)), _const_spec((1, LANES)),
                  _const_spec((1, LANES))],
        out_specs=[out_spec, out_spec],
        compiler_params=_cparams(1),
        name="rope_tables",
    )(pos_cols, invf, sign)
    return cos_t.reshape(n_tok, LANES), sin_t.reshape(n_tok, LANES)


def _rope(p, cos, sin):
    width = p.shape[-1]
    lane = lax.broadcasted_iota(jnp.int32, p.shape, 1)
    first_half = (lane % HEAD_DIM) < (HEAD_DIM // 2)
    swapped = jnp.where(first_half,
                        pltpu.roll(p, width - HEAD_DIM // 2, 1),
                        pltpu.roll(p, HEAD_DIM // 2, 1))
    return p * cos + swapped * sin


def _in_proj_kernel(x_ref, g_ref, w_ref, wvt_ref, cos_ref, sin_ref,
                    qa_ref, kaw_ref, vat_ref, qd_ref, kd_ref, vdt_ref):
    hb = _rms(x_ref[...], g_ref[...]).astype(BF16)
    cos1 = cos_ref[...]
    sin1 = sin_ref[...]
    cos2 = jnp.concatenate([cos1, cos1], axis=-1)
    sin2 = jnp.concatenate([sin1, sin1], axis=-1)
    q_scale = HEAD_DIM ** -0.5 * LOG2E

    def proj(c0, width):
        return jnp.dot(hb, w_ref[:, c0:c0 + width], preferred_element_type=F32)

    for c in range(SWA_Q_COLS // 256):
        p = _rope(proj(c * 256, 256), cos2, sin2) * q_scale
        qa_ref[:, c * 256:(c + 1) * 256] = p.astype(BF16)
    off = SWA_Q_COLS
    ka = _rope(proj(off, SWA_KV_COLS), cos1, sin1)
    ka_swapped = pltpu.roll(ka, HEAD_DIM, 1)
    lane = lax.broadcasted_iota(jnp.int32, ka.shape, 1)
    for kv, rep in enumerate((jnp.where(lane < HEAD_DIM, ka, ka_swapped),
                              jnp.where(lane < HEAD_DIM, ka_swapped, ka))):
        rep = rep.astype(BF16)
        for half in range(2):
            c0 = kv * SWA_GROUP * HEAD_DIM + half * LANES
            kaw_ref[:, c0:c0 + LANES] = rep
    off += SWA_KV_COLS
    for c in range(DIFF_QK_COLS // 256):
        p = _rope(proj(off + c * 256, 256), cos2, sin2) * q_scale
        qd_ref[:, c * 256:(c + 1) * 256] = p.astype(BF16)
    off += DIFF_QK_COLS
    for c in range(DIFF_QK_COLS // 256):
        p = _rope(proj(off + c * 256, 256), cos2, sin2)
        kd_ref[:, c * 256:(c + 1) * 256] = p.astype(BF16)
    vt = lax.dot_general(wvt_ref[...], hb, (((1,), (1,)), ((), ())), preferred_element_type=F32)
    vat_ref[...] = vt[:SWA_KV_COLS].astype(BF16)
    pad_row = lax.broadcasted_iota(jnp.int32, (DIFF_VPAD - DIFF_VDIM, vt.shape[1]), 0)
    ones_rows = jnp.where(pad_row == 0, 1.0, 0.0).astype(BF16)
    for h in range(DIFF_HEADS):
        src = SWA_KV_COLS + h * DIFF_VDIM
        vdt_ref[h * DIFF_VPAD:h * DIFF_VPAD + DIFF_VDIM, :] = vt[src:src + DIFF_VDIM].astype(BF16)
        vdt_ref[h * DIFF_VPAD + DIFF_VDIM:(h + 1) * DIFF_VPAD, :] = ones_rows


def _in_proj(x2d, g_mix, w_in, cos_t, sin_t):
    n_tok = x2d.shape[0]
    ts = TOK_TILE
    va_off = SWA_Q_COLS + SWA_KV_COLS
    qd_off = va_off + SWA_KV_COLS
    vd_off = qd_off + 2 * DIFF_QK_COLS
    w_main = jnp.concatenate([w_in[:, :va_off], w_in[:, qd_off:vd_off]], axis=1).astype(BF16)
    w_vt = jnp.concatenate([w_in[:, va_off:qd_off], w_in[:, vd_off:]], axis=1).T.astype(BF16)
    tok = lambda width: pl.BlockSpec((ts, width), lambda i: (i, 0))
    tok_t = lambda rows: pl.BlockSpec((rows, ts), lambda i: (0, i))
    kaw_cols = SWA_KV_HEADS * SWA_GROUP * HEAD_DIM
    out_shape = [
        jax.ShapeDtypeStruct((n_tok, SWA_Q_COLS), BF16),
        jax.ShapeDtypeStruct((n_tok, kaw_cols), BF16),
        jax.ShapeDtypeStruct((SWA_KV_COLS, n_tok), BF16),
        jax.ShapeDtypeStruct((n_tok, DIFF_QK_COLS), BF16),
        jax.ShapeDtypeStruct((n_tok, DIFF_QK_COLS), BF16),
        jax.ShapeDtypeStruct((DIFF_HEADS * DIFF_VPAD, n_tok), BF16),
    ]
    return pl.pallas_call(
        _in_proj_kernel,
        out_shape=out_shape,
        grid=(n_tok // ts,),
        in_specs=[tok(D_MODEL), _const_spec((1, D_MODEL)), _const_spec(w_main.shape),
                  _const_spec(w_vt.shape), tok(LANES), tok(LANES)],
        out_specs=[tok(SWA_Q_COLS), tok(kaw_cols), tok_t(SWA_KV_COLS), tok(DIFF_QK_COLS),
                   tok(DIFF_QK_COLS), tok_t(DIFF_HEADS * DIFF_VPAD)],
        compiler_params=_cparams(1),
        name="in_proj",
    )(x2d, g_mix.reshape(1, D_MODEL), w_main, w_vt, cos_t, sin_t)


def _swa_kernel(q_ref, kc_ref, kp_ref, vc_ref, vp_ref, sink_ref, o_ref):
    n_sub = q_ref.shape[0] // WINDOW
    group_cols = SWA_GROUP * HEAD_DIM
    kj = lax.broadcasted_iota(jnp.int32, (WINDOW, 2 * WINDOW), 0)
    qi = lax.broadcasted_iota(jnp.int32, (WINDOW, 2 * WINDOW), 1) % WINDOW
    from_prev = kj > qi
    no_prev_bias = jnp.where(pl.program_id(1) == 0, NEG_BIG, 0.0)
    qlane = lax.broadcasted_iota(jnp.int32, (WINDOW, group_cols), 1) // HEAD_DIM
    sinks = sink_ref[...] * LOG2E
    pairs = SWA_GROUP // 2
    units = [(r, kv, pair) for r in range(n_sub) for kv in range(SWA_KV_HEADS)
             for pair in range(pairs)]

    def key_rows(r):
        if r == 0:
            return jnp.concatenate([kp_ref[...], kc_ref[0:WINDOW, :]], axis=0)
        return kc_ref[(r - 1) * WINDOW:(r + 1) * WINDOW, :]

    def value_cols(r):
        if r == 0:
            return jnp.concatenate([vp_ref[...], vc_ref[:, 0:WINDOW]], axis=1)
        return vc_ref[:, (r - 1) * WINDOW:(r + 1) * WINDOW]

    def scores(unit):
        r, kv, pair = unit
        qg = q_ref[r * WINDOW:(r + 1) * WINDOW, kv * group_cols:(kv + 1) * group_cols]
        zero = jnp.zeros_like(qg)
        q2 = jnp.concatenate([jnp.where(qlane == 2 * pair + g, qg, zero) for g in range(2)],
                             axis=0)
        kw = key_rows(r)[:, kv * group_cols:(kv + 1) * group_cols]
        s = lax.dot_general(kw, q2, (((1,), (1,)), ((), ())), preferred_element_type=F32)
        s_prev = s[:WINDOW] + no_prev_bias if r == 0 else s[:WINDOW]
        return jnp.where(from_prev, s_prev, s[WINDOW:])

    def attend(unit, s):
        r, kv, pair = unit
        tile = kv * pairs + pair
        sink = sinks[tile:tile + 1, :]
        m = jnp.maximum(jnp.max(s, axis=0, keepdims=True), sink)
        e = jnp.exp2(s - m)
        denom = jnp.sum(e, axis=0, keepdims=True) + jnp.exp2(sink - m)
        eb = e.astype(BF16)
        ezero = jnp.zeros_like(eb)
        e2 = jnp.concatenate([jnp.where(from_prev, eb, ezero),
                              jnp.where(from_prev, ezero, eb)], axis=0)
        vt = value_cols(r)[kv * HEAD_DIM:(kv + 1) * HEAD_DIM, :]
        o = jnp.dot(vt, e2, preferred_element_type=F32) / denom
        return [o[:, :WINDOW], o[:, WINDOW:]]

    pieces = []
    pending = [scores(u) for u in units[:SWA_AHEAD]]
    for idx, unit in enumerate(units):
        if idx + SWA_AHEAD < len(units):
            pending.append(scores(units[idx + SWA_AHEAD]))
        pieces += attend(unit, pending.pop(0))
        if len(pieces) == SWA_Q_HEADS:
            r = unit[0]
            o_ref[r * WINDOW:(r + 1) * WINDOW, :] = (
                jnp.concatenate(pieces, axis=0).T.astype(BF16))
            pieces = []


def _swa(qa, kaw, vat, sinks, bsz, seq):
    ts = SWA_TILE
    per_b = seq // ts
    sub = ts // WINDOW
    prev_blk = lambda b, t: jnp.maximum((b * per_b + t) * sub - 1, 0)
    cur = lambda width: pl.BlockSpec((ts, width), lambda b, t: (b * per_b + t, 0))
    sink_rows = jnp.repeat(sinks.reshape(SWA_Q_HEADS // 2, 2), WINDOW, axis=1)
    return pl.pallas_call(
        _swa_kernel,
        out_shape=jax.ShapeDtypeStruct(qa.shape, BF16),
        grid=(bsz, per_b),
        in_specs=[cur(SWA_Q_COLS), cur(kaw.shape[1]),
                  pl.BlockSpec((WINDOW, kaw.shape[1]), lambda b, t: (prev_blk(b, t), 0)),
                  pl.BlockSpec((SWA_KV_COLS, ts), lambda b, t: (0, b * per_b + t)),
                  pl.BlockSpec((SWA_KV_COLS, WINDOW), lambda b, t: (0, prev_blk(b, t))),
                  _const_spec(sink_rows.shape)],
        out_specs=cur(SWA_Q_COLS),
        compiler_params=_cparams(2),
        name="swa",
    )(qa, kaw, kaw, vat, vat, sink_rows)


def _diff_kernel(lam_init, q_ref, k_ref, vt_ref, bias_ref, lamp_ref, g_ref, o_ref,
                 s0_ref, s1_ref, acc_ref, max_ref, qm_ref):
    group_idx = pl.program_id(2)
    th = DIFF_TILE
    top = DIFF_GROUP - 1
    s_refs = (s0_ref, s1_ref)
    lane = lax.broadcasted_iota(jnp.int32, (th, DIFF_VDIM), 1)
    for t in range(top, -1, -1):
        q = q_ref[t * th:(t + 1) * th, :]
        zero = jnp.zeros_like(q)
        qm_ref[2 * t] = jnp.where(lane < HEAD_DIM, q, zero)
        qm_ref[2 * t + 1] = jnp.where(lane >= HEAD_DIM, q, zero)

    def k_tile(j):
        return k_ref[pl.ds(pl.multiple_of(j * th, th), th), :]

    def vt_tile(j):
        return vt_ref[:, pl.ds(pl.multiple_of(j * th, th), th)]

    def start(unit, slot):
        j, t, c, masked = unit
        s = lax.dot_general(k_tile(j), qm_ref[2 * t + c], (((1,), (1,)), ((), ())),
                            preferred_element_type=F32)
        if masked:
            s = s + bias_ref[...]
        s_refs[slot][...] = s
        return jnp.max(s, axis=0, keepdims=True)

    def consume(unit, slot, tile_max):
        j, t, c, _ = unit
        idx = 2 * t + c
        m = max_ref[idx:idx + 1, :]
        m_new = jnp.maximum(m, tile_max)
        alpha = jnp.exp2(m - m_new)
        p = jnp.exp2(s_refs[slot][...] - m_new)
        max_ref[idx:idx + 1, :] = m_new
        pv = jnp.dot(vt_tile(j), p.astype(BF16), preferred_element_type=F32)
        acc_ref[idx] = alpha * acc_ref[idx] + pv

    def finish(t):
        lp = lamp_ref[...]
        lam = (jnp.exp(jnp.sum(lp[0:1] * lp[1:2], axis=-1, keepdims=True))
               - jnp.exp(jnp.sum(lp[2:3] * lp[3:4], axis=-1, keepdims=True)) + lam_init)
        l0 = acc_ref[2 * t, DIFF_VDIM:DIFF_VDIM + 1, :]
        l1 = acc_ref[2 * t + 1, DIFF_VDIM:DIFF_VDIM + 1, :]
        o = (acc_ref[2 * t, :DIFF_VDIM, :] * (1.0 / l0)
             - acc_ref[2 * t + 1, :DIFF_VDIM, :] * (lam / l1))
        ms = jnp.mean(o * o, axis=0, keepdims=True)
        y = o * lax.rsqrt(ms + NORM_EPS) * g_ref[...] * (1.0 - lam_init)
        o_ref[t * th:(t + 1) * th, :] = y.T.astype(BF16)

    def run(units, following, tile_max):
        for n, unit in enumerate(units):
            nxt = units[n + 1] if n + 1 < len(units) else following
            next_max = start(nxt, (n + 1) % 2) if nxt is not None else None
            consume(unit, n % 2, tile_max)
            tile_max = next_max
            _, t, c, masked = unit
            if masked and c == 1:
                finish(t)
        return tile_max

    def key_tile_units(j, lowest, lowest_masked):
        return [(j, t, c, lowest_masked and t == lowest)
                for t in range(top, lowest - 1, -1) for c in range(2)]

    tile_max = start((0, top, 0, False), 0)
    acc_ref[...] = jnp.zeros_like(acc_ref)
    max_ref[...] = jnp.full(max_ref.shape, NEG_BIG, F32)
    tile_max = lax.fori_loop(
        0, DIFF_GROUP * group_idx,
        lambda j, mx: run(key_tile_units(j, 0, False), (j + 1, top, 0, False), mx), tile_max)
    j0 = DIFF_GROUP * group_idx
    tail = [u for d in range(DIFF_GROUP) for u in key_tile_units(j0 + d, d, True)]
    run(tail, None, tile_max)


def _diff_attn(qd, kd, vdt, lam_params, g_diff, lam_init, bsz, seq):
    th = DIFF_TILE
    tq = DIFF_GROUP * th
    nq = seq // tq
    key = lax.broadcasted_iota(jnp.int32, (th, th), 0)
    qry = lax.broadcasted_iota(jnp.int32, (th, th), 1)
    causal_bias = jnp.where(key <= qry, 0.0, NEG_BIG).astype(F32)
    return pl.pallas_call(
        functools.partial(_diff_kernel, lam_init),
        out_shape=jax.ShapeDtypeStruct(qd.shape, BF16),
        grid=(bsz, DIFF_HEADS, nq),
        in_specs=[
            pl.BlockSpec((tq, DIFF_VDIM), lambda b, h, i: (b * nq + i, h)),
            pl.BlockSpec((seq, DIFF_VDIM), lambda b, h, i: (b, h)),
            pl.BlockSpec((DIFF_VPAD, seq), lambda b, h, i: (h, b)),
            _const_spec((th, th)),
            _const_spec((4, HEAD_DIM)),
            _const_spec((DIFF_VDIM, 1)),
        ],
        out_specs=pl.BlockSpec((tq, DIFF_VDIM), lambda b, h, i: (b * nq + i, h)),
        scratch_shapes=[pltpu.VMEM((th, th), F32), pltpu.VMEM((th, th), F32),
                        pltpu.VMEM((2 * DIFF_GROUP, DIFF_VPAD, th), F32),
                        pltpu.VMEM((2 * DIFF_GROUP, th), F32),
                        pltpu.VMEM((2 * DIFF_GROUP, th, DIFF_VDIM), BF16)],
        compiler_params=_cparams(3),
        name="diff_attn",
    )(qd, kd, vdt, causal_bias, lam_params, g_diff.reshape(DIFF_VDIM, 1))


def _out_proj_kernel(a_ref, b_ref, x_ref, woa_ref, wob_ref, g_ref, wq_ref, x1_ref, qc_ref):
    y = (jnp.dot(a_ref[...], woa_ref[...], preferred_element_type=F32)
         + jnp.dot(b_ref[...], wob_ref[...], preferred_element_type=F32))
    x1 = x_ref[...] + y
    x1_ref[...] = x1
    hc = _rms(x1, g_ref[...]).astype(BF16)
    qc = jnp.dot(hc, wq_ref[...], preferred_element_type=F32) * (CROSS_HEAD_DIM ** -0.5 * LOG2E)
    qc_ref[...] = qc.astype(BF16)


def _out_proj(out_a, out_b, x2d, w_out, g_cross, w_cq):
    n_tok = x2d.shape[0]
    ts = TOK_TILE
    tok = lambda width: pl.BlockSpec((ts, width), lambda i: (i, 0))
    half = SWA_Q_COLS
    return pl.pallas_call(
        _out_proj_kernel,
        out_shape=[jax.ShapeDtypeStruct((n_tok, D_MODEL), F32),
                   jax.ShapeDtypeStruct((n_tok, D_MODEL), BF16)],
        grid=(n_tok // ts,),
        in_specs=[tok(half), tok(half), tok(D_MODEL), _const_spec((half, D_MODEL)),
                  _const_spec((half, D_MODEL)), _const_spec((1, D_MODEL)),
                  _const_spec((D_MODEL, D_MODEL))],
        out_specs=[tok(D_MODEL), tok(D_MODEL)],
        compiler_params=_cparams(1),
        name="out_proj",
    )(out_a, out_b, x2d, w_out[:half].astype(BF16), w_out[half:].astype(BF16),
      g_cross.reshape(1, D_MODEL), w_cq.astype(BF16))


def _mem_kv_kernel(m_ref, g_ref, w_ref, k_ref, v_ref):
    hm = _rms(m_ref[...], g_ref[...]).astype(BF16)
    k_ref[...] = jnp.dot(hm, w_ref[:, :D_MODEL], preferred_element_type=F32).astype(BF16)
    v_ref[...] = jnp.dot(hm, w_ref[:, D_MODEL:], preferred_element_type=F32).astype(BF16)


def _mem_kv(mem2d, g_mem, w_ckv, bsz, mem_len):
    blk = pl.BlockSpec((mem_len, D_MODEL), lambda b: (b, 0))
    return pl.pallas_call(
        _mem_kv_kernel,
        out_shape=[jax.ShapeDtypeStruct(mem2d.shape, BF16)] * 2,
        grid=(bsz,),
        in_specs=[blk, _const_spec((1, D_MODEL)), _const_spec((D_MODEL, 2 * D_MODEL))],
        out_specs=[blk, blk],
        compiler_params=_cparams(1),
        name="mem_kv",
    )(mem2d, g_mem.reshape(1, D_MODEL), w_ckv.astype(BF16))


def _cross_kernel(q_ref, k_ref, v_ref, wo_ref, x_ref, o_ref):
    def head_cols(h):
        return slice(h * CROSS_HEAD_DIM, (h + 1) * CROSS_HEAD_DIM)

    def scores(h):
        return lax.dot_general(q_ref[:, head_cols(h)], k_ref[:, head_cols(h)],
                               (((1,), (1,)), ((), ())), preferred_element_type=F32)

    def attend(h, s):
        m = jnp.max(s, axis=-1, keepdims=True)
        e = jnp.exp2(s - m)
        inv = 1.0 / jnp.sum(e, axis=-1, keepdims=True)
        pv = jnp.dot(e.astype(BF16), v_ref[:, head_cols(h)], preferred_element_type=F32)
        return (pv * inv).astype(BF16)

    heads = []
    s_next = scores(0)
    for h in range(CROSS_HEADS):
        s_cur = s_next
        if h + 1 < CROSS_HEADS:
            s_next = scores(h + 1)
        heads.append(attend(h, s_cur))
    o = jnp.concatenate(heads, axis=-1)
    o_ref[...] = x_ref[...] + jnp.dot(o, wo_ref[...], preferred_element_type=F32)


def _cross_attn(qc, kc, vc, w_co, x1, seq, mem_len):
    n_tok = x1.shape[0]
    ts = TOK_TILE
    per_b = seq // ts
    tok = pl.BlockSpec((ts, D_MODEL), lambda i: (i, 0))
    memblk = pl.BlockSpec((mem_len, D_MODEL), lambda i: (i // per_b, 0))
    return pl.pallas_call(
        _cross_kernel,
        out_shape=jax.ShapeDtypeStruct(x1.shape, F32),
        grid=(n_tok // ts,),
        in_specs=[tok, memblk, memblk, _const_spec((D_MODEL, D_MODEL)), tok],
        out_specs=tok,
        compiler_params=_cparams(1),
        name="cross_attn",
    )(qc, kc, vc, w_co.astype(BF16), x1)


def _mlp_kernel(final_norm, x_ref, g_ref, wu_ref, wd_ref, gf_ref, o_ref):
    x = x_ref[...]
    hb = _rms(x, g_ref[...]).astype(BF16)
    acc = x
    d_ff = wu_ref.shape[1]
    for c in range(d_ff // FF_CHUNK):
        cols = slice(c * FF_CHUNK, (c + 1) * FF_CHUNK)
        u = jnp.dot(hb, wu_ref[:, cols], preferred_element_type=F32)
        r = jnp.maximum(u, 0.0)
        acc = acc + jnp.dot((r * r).astype(BF16), wd_ref[cols, :], preferred_element_type=F32)
    o_ref[...] = _rms(acc, gf_ref[...]) if final_norm else acc


def _mlp(x2, g_mlp, w_up, w_down, g_final, final_norm):
    n_tok = x2.shape[0]
    ts = TOK_TILE
    tok = pl.BlockSpec((ts, D_MODEL), lambda i: (i, 0))
    return pl.pallas_call(
        functools.partial(_mlp_kernel, final_norm),
        out_shape=jax.ShapeDtypeStruct(x2.shape, F32),
        grid=(n_tok // ts,),
        in_specs=[tok, _const_spec((1, D_MODEL)), _const_spec(w_up.shape),
                  _const_spec(w_down.shape), _const_spec((1, D_MODEL))],
        out_specs=tok,
        compiler_params=_cparams(1),
        name="mlp",
    )(x2, g_mlp.reshape(1, D_MODEL), w_up.astype(BF16), w_down.astype(BF16),
      g_final.reshape(1, D_MODEL))


def kernel(x, mem, positions, g_mix, w_in, sinks, lambda_q1, lambda_k1, lambda_q2, lambda_k2,
           g_diff, w_out, g_cross, g_mem, w_cq, w_ckv, w_co, g_mlp, w_up, w_down, g_final):
    bsz, seq, _ = x.shape
    mem_len = mem.shape[1]
    depth = w_in.shape[0]
    cos_t, sin_t = _rope_tables(positions)
    xf = x.reshape(bsz * seq, D_MODEL)
    memf = mem.reshape(bsz * mem_len, D_MODEL)
    for l in range(depth):
        lam_init = 0.8 - 0.6 * math.exp(-0.3 * l)
        qa, kaw, vat, qd, kd, vdt = _in_proj(xf, g_mix[l], w_in[l], cos_t, sin_t)
        out_a = _swa(qa, kaw, vat, sinks[l], bsz, seq)
        lam_params = jnp.stack([lambda_q1[l], lambda_k1[l], lambda_q2[l], lambda_k2[l]])
        out_b = _diff_attn(qd, kd, vdt, lam_params, g_diff[l], lam_init, bsz, seq)
        x1, qc = _out_proj(out_a, out_b, xf, w_out[l], g_cross[l], w_cq[l])
        kc, vc = _mem_kv(memf, g_mem[l], w_ckv[l], bsz, mem_len)
        x2 = _cross_attn(qc, kc, vc, w_co[l], x1, seq, mem_len)
        xf = _mlp(x2, g_mlp[l], w_up[l], w_down[l], g_final, l == depth - 1)
    return xf.reshape(bsz, seq, D_MODEL)
```

```python
import functools
import math

import jax
import jax.numpy as jnp
from jax import lax
from jax.experimental import pallas as pl
from jax.experimental.pallas import tpu as pltpu

D_MODEL = 1024
HEAD_DIM = 64
SWA_Q_HEADS = 8
SWA_KV_HEADS = 2
SWA_GROUP = SWA_Q_HEADS // SWA_KV_HEADS
WINDOW = 128
DIFF_HEADS = 4
DIFF_VDIM = 2 * HEAD_DIM
DIFF_VPAD = DIFF_VDIM + 16
SWA_Q_COLS = SWA_Q_HEADS * HEAD_DIM
SWA_KV_COLS = SWA_KV_HEADS * HEAD_DIM
DIFF_QK_COLS = DIFF_HEADS * 2 * HEAD_DIM
DIFF_V_COLS = DIFF_HEADS * DIFF_VDIM
CROSS_HEADS = 4
CROSS_HEAD_DIM = D_MODEL // CROSS_HEADS
ROPE_THETA = 10000.0
NORM_EPS = 1e-5

LANES = 128
TOK_TILE = 1024
SWA_TILE = 512
DIFF_TILE = 512
DIFF_GROUP = 4
FF_CHUNK = 512
SWA_AHEAD = 10
VMEM_LIMIT = 56 * 1024 * 1024
NEG_BIG = -1e30
LOG2E = math.log2(math.e)

BF16 = jnp.bfloat16
F32 = jnp.float32


def _cparams(n_axes):
    return pltpu.CompilerParams(dimension_semantics=("arbitrary",) * n_axes,
                                vmem_limit_bytes=VMEM_LIMIT)


def _rms(x, g):
    ms = jnp.mean(x * x, axis=-1, keepdims=True)
    return x * lax.rsqrt(ms + NORM_EPS) * g


def _const_spec(shape):
    return pl.BlockSpec(shape, lambda *_: (0,) * len(shape), pipeline_mode=pl.Buffered(1))


ROPE_HALF = HEAD_DIM // 2
ROPE_GROUPS = LANES // ROPE_HALF


def _rope_table_kernel(pos_ref, invf_ref, sign_ref, cos_ref, sin_ref):
    pos = pos_ref[...].astype(F32)
    lane_group = lax.broadcasted_iota(jnp.int32, (pos.shape[0], LANES), 1) // ROPE_HALF
    pos_l = pos[:, ROPE_GROUPS - 1:ROPE_GROUPS]
    for g in range(ROPE_GROUPS - 2, -1, -1):
        pos_l = jnp.where(lane_group == g, pos[:, g:g + 1], pos_l)
    ang = pos_l * invf_ref[...]
    for tab, sign, out_ref in ((jnp.cos(ang), None, cos_ref),
                               (jnp.sin(ang), sign_ref[...], sin_ref)):
        rolled = [tab] + [pltpu.roll(tab, k * ROPE_HALF, 1) for k in range(1, ROPE_GROUPS)]
        for g in range(ROPE_GROUPS):
            shift = (lane_group - g) % ROPE_GROUPS
            tiled = rolled[ROPE_GROUPS - 1]
            for k in range(ROPE_GROUPS - 2, -1, -1):
                tiled = jnp.where(shift == k, rolled[k], tiled)
            out_ref[g] = tiled if sign is None else tiled * sign


def _rope_tables(positions):
    n_tok = positions.size
    inv_freq = ROPE_THETA ** (-jnp.arange(0, HEAD_DIM, 2, dtype=F32) / HEAD_DIM)
    quarter = n_tok // ROPE_GROUPS
    pos_cols = positions.reshape(ROPE_GROUPS, quarter).T
    invf = jnp.tile(inv_freq, ROPE_GROUPS).reshape(1, LANES)
    sign = jnp.tile(jnp.repeat(jnp.array([-1.0, 1.0], F32), ROPE_HALF), 2).reshape(1, LANES)
    blk = 1024
    out_spec = pl.BlockSpec((ROPE_GROUPS, blk, LANES), lambda i: (0, i, 0))
    cos_t, sin_t = pl.pallas_call(
        _rope_table_kernel,
        out_shape=[jax.ShapeDtypeStruct((ROPE_GROUPS, quarter, LANES), F32)] * 2,
        grid=(quarter // blk,),
        in_specs=[pl.BlockSpec((blk, ROPE_GROUPS), lambda i: (i, 0)), _const_spec((1, LANES)),
                  _const_spec((1, LANES))],
        out_specs=[out_spec, out_spec],
        compiler_params=_cparams(1),
        name="rope_tables",
    )(pos_cols, invf, sign)
    return cos_t.reshape(n_tok, LANES), sin_t.reshape(n_tok, LANES)


def _rope(p, cos, sin):
    width = p.shape[-1]
    lane = lax.broadcasted_iota(jnp.int32, p.shape, 1)
    first_half = (lane % HEAD_DIM) < (HEAD_DIM // 2)
    swapped = jnp.where(first_half,
                        pltpu.roll(p, width - HEAD_DIM // 2, 1),
                        pltpu.roll(p, HEAD_DIM // 2, 1))
    return p * cos + swapped * sin


def _in_proj_kernel(x_ref, g_ref, w_ref, wvt_ref, cos_ref, sin_ref,
                    qa_ref, kaw_ref, vat_ref, qd_ref, kd_ref, vdt_ref):
    hb = _rms(x_ref[...], g_ref[...]).astype(BF16)
    cos1 = cos_ref[...]
    sin1 = sin_ref[...]
    cos2 = jnp.concatenate([cos1, cos1], axis=-1)
    sin2 = jnp.concatenate([sin1, sin1], axis=-1)
    q_scale = HEAD_DIM ** -0.5 * LOG2E

    def proj(c0, width):
        return jnp.dot(hb, w_ref[:, c0:c0 + width], preferred_element_type=F32)

    lane = lax.broadcasted_iota(jnp.int32, cos1.shape, 1)
    low_half = lane < HEAD_DIM

    def split_halves(p, out_ref, first_group):
        for g in range(p.shape[1] // LANES):
            v = p[:, g * LANES:(g + 1) * LANES]
            c0 = (first_group + 2 * g) * LANES
            out_ref[:, c0:c0 + LANES] = jnp.where(low_half, v, 0.0).astype(BF16)
            out_ref[:, c0 + LANES:c0 + 2 * LANES] = jnp.where(low_half, 0.0, v).astype(BF16)

    for c in range(SWA_Q_COLS // 256):
        p = _rope(proj(c * 256, 256), cos2, sin2) * q_scale
        split_halves(p, qa_ref, 4 * c)
    off = SWA_Q_COLS
    ka = _rope(proj(off, SWA_KV_COLS), cos1, sin1)
    ka_swapped = pltpu.roll(ka, HEAD_DIM, 1)
    kaw_ref[:, :LANES] = jnp.where(low_half, ka, ka_swapped).astype(BF16)
    kaw_ref[:, LANES:] = jnp.where(low_half, ka_swapped, ka).astype(BF16)
    off += SWA_KV_COLS
    for c in range(DIFF_QK_COLS // 256):
        p = _rope(proj(off + c * 256, 256), cos2, sin2) * q_scale
        split_halves(p, qd_ref, 4 * c)
    off += DIFF_QK_COLS
    for c in range(DIFF_QK_COLS // 256):
        p = _rope(proj(off + c * 256, 256), cos2, sin2)
        kd_ref[:, c * 256:(c + 1) * 256] = p.astype(BF16)
    vt = lax.dot_general(wvt_ref[...], hb, (((1,), (1,)), ((), ())), preferred_element_type=F32)
    vat_ref[...] = vt[:SWA_KV_COLS].astype(BF16)
    pad_row = lax.broadcasted_iota(jnp.int32, (DIFF_VPAD - DIFF_VDIM, vt.shape[1]), 0)
    ones_rows = jnp.where(pad_row == 0, 1.0, 0.0).astype(BF16)
    for h in range(DIFF_HEADS):
        src = SWA_KV_COLS + h * DIFF_VDIM
        vdt_ref[h * DIFF_VPAD:h * DIFF_VPAD + DIFF_VDIM, :] = vt[src:src + DIFF_VDIM].astype(BF16)
        vdt_ref[h * DIFF_VPAD + DIFF_VDIM:(h + 1) * DIFF_VPAD, :] = ones_rows


def _in_proj(x2d, g_mix, w_in, cos_t, sin_t):
    n_tok = x2d.shape[0]
    ts = TOK_TILE
    va_off = SWA_Q_COLS + SWA_KV_COLS
    qd_off = va_off + SWA_KV_COLS
    vd_off = qd_off + 2 * DIFF_QK_COLS
    w_main = jnp.concatenate([w_in[:, :va_off], w_in[:, qd_off:vd_off]], axis=1).astype(BF16)
    w_vt = jnp.concatenate([w_in[:, va_off:qd_off], w_in[:, vd_off:]], axis=1).T.astype(BF16)
    tok = lambda width: pl.BlockSpec((ts, width), lambda i: (i, 0))
    tok_t = lambda rows: pl.BlockSpec((rows, ts), lambda i: (0, i))
    qa_cols = SWA_Q_HEADS * LANES
    kaw_cols = SWA_KV_HEADS * LANES
    qd_cols = DIFF_HEADS * 2 * LANES
    out_shape = [
        jax.ShapeDtypeStruct((n_tok, qa_cols), BF16),
        jax.ShapeDtypeStruct((n_tok, kaw_cols), BF16),
        jax.ShapeDtypeStruct((SWA_KV_COLS, n_tok), BF16),
        jax.ShapeDtypeStruct((n_tok, qd_cols), BF16),
        jax.ShapeDtypeStruct((n_tok, DIFF_QK_COLS), BF16),
        jax.ShapeDtypeStruct((DIFF_HEADS * DIFF_VPAD, n_tok), BF16),
    ]
    return pl.pallas_call(
        _in_proj_kernel,
        out_shape=out_shape,
        grid=(n_tok // ts,),
        in_specs=[tok(D_MODEL), _const_spec((1, D_MODEL)), _const_spec(w_main.shape),
                  _const_spec(w_vt.shape), tok(LANES), tok(LANES)],
        out_specs=[tok(qa_cols), tok(kaw_cols), tok_t(SWA_KV_COLS), tok(qd_cols),
                   tok(DIFF_QK_COLS), tok_t(DIFF_HEADS * DIFF_VPAD)],
        compiler_params=_cparams(1),
        name="in_proj",
    )(x2d, g_mix.reshape(1, D_MODEL), w_main, w_vt, cos_t, sin_t)


def _swa_kernel(q_ref, kc_ref, kp_ref, vc_ref, vp_ref, sink_ref, tri_ref, o_ref):
    n_sub = q_ref.shape[0] // WINDOW
    kj = lax.broadcasted_iota(jnp.int32, (WINDOW, 2 * WINDOW), 0)
    qi = lax.broadcasted_iota(jnp.int32, (WINDOW, 2 * WINDOW), 1) % WINDOW
    from_prev = kj > qi
    no_prev_bias = jnp.where(pl.program_id(1) == 0, NEG_BIG, 0.0)
    sinks = sink_ref[...] * LOG2E
    pairs = SWA_GROUP // 2
    units = [(r, kv, pair) for r in range(n_sub) for kv in range(SWA_KV_HEADS)
             for pair in range(pairs)]

    def key_rows(r):
        if r == 0:
            return jnp.concatenate([kp_ref[...], kc_ref[0:WINDOW, :]], axis=0)
        return kc_ref[(r - 1) * WINDOW:(r + 1) * WINDOW, :]

    def value_cols(r):
        if r == 0:
            return jnp.concatenate([vp_ref[...], vc_ref[:, 0:WINDOW]], axis=1)
        return vc_ref[:, (r - 1) * WINDOW:(r + 1) * WINDOW]

    def scores(unit):
        r, kv, pair = unit
        head = kv * SWA_GROUP + 2 * pair
        q2 = jnp.concatenate(
            [q_ref[r * WINDOW:(r + 1) * WINDOW, (head + g) * LANES:(head + g + 1) * LANES]
             for g in range(2)], axis=0)
        kw = key_rows(r)[:, kv * LANES:(kv + 1) * LANES]
        s = lax.dot_general(kw, q2, (((1,), (1,)), ((), ())), preferred_element_type=F32)
        s_prev = s[:WINDOW] + no_prev_bias if r == 0 else s[:WINDOW]
        return jnp.where(from_prev, s_prev, s[WINDOW:])

    def attend(unit, s):
        r, kv, pair = unit
        tile = kv * pairs + pair
        sink = sinks[tile:tile + 1, :]
        m = jnp.maximum(jnp.max(s, axis=0, keepdims=True), sink)
        e = jnp.exp2(s - m)
        denom = jnp.sum(e, axis=0, keepdims=True) + jnp.exp2(sink - m)
        eb = e.astype(BF16)
        e2 = jnp.concatenate([eb, eb], axis=0) * tri_ref[...]
        vt = value_cols(r)[kv * HEAD_DIM:(kv + 1) * HEAD_DIM, :]
        o = jnp.dot(vt, e2, preferred_element_type=F32) / denom
        return [o[:, :WINDOW], o[:, WINDOW:]]

    pieces = []
    pending = [scores(u) for u in units[:SWA_AHEAD]]
    for idx, unit in enumerate(units):
        if idx + SWA_AHEAD < len(units):
            pending.append(scores(units[idx + SWA_AHEAD]))
        pieces += attend(unit, pending.pop(0))
        if len(pieces) == SWA_Q_HEADS:
            r = unit[0]
            o_ref[r * WINDOW:(r + 1) * WINDOW, :] = (
                jnp.concatenate(pieces, axis=0).T.astype(BF16))
            pieces = []


def _swa(qa, kaw, vat, sinks, bsz, seq):
    ts = SWA_TILE
    per_b = seq // ts
    sub = ts // WINDOW
    prev_blk = lambda b, t: jnp.maximum((b * per_b + t) * sub - 1, 0)
    cur = lambda width: pl.BlockSpec((ts, width), lambda b, t: (b * per_b + t, 0))
    sink_rows = jnp.repeat(sinks.reshape(SWA_Q_HEADS // 2, 2), WINDOW, axis=1)
    kj = lax.broadcasted_iota(jnp.int32, (WINDOW, 2 * WINDOW), 0)
    qi = lax.broadcasted_iota(jnp.int32, (WINDOW, 2 * WINDOW), 1) % WINDOW
    triangles = jnp.concatenate([kj > qi, kj <= qi], axis=0).astype(BF16)
    return pl.pallas_call(
        _swa_kernel,
        out_shape=jax.ShapeDtypeStruct((qa.shape[0], SWA_Q_COLS), BF16),
        grid=(bsz, per_b),
        in_specs=[cur(qa.shape[1]), cur(kaw.shape[1]),
                  pl.BlockSpec((WINDOW, kaw.shape[1]), lambda b, t: (prev_blk(b, t), 0)),
                  pl.BlockSpec((SWA_KV_COLS, ts), lambda b, t: (0, b * per_b + t)),
                  pl.BlockSpec((SWA_KV_COLS, WINDOW), lambda b, t: (0, prev_blk(b, t))),
                  _const_spec(sink_rows.shape), _const_spec(triangles.shape)],
        out_specs=cur(SWA_Q_COLS),
        compiler_params=_cparams(2),
        name="swa",
    )(qa, kaw, kaw, vat, vat, sink_rows, triangles)


def _diff_kernel(lam_init, q_ref, k_ref, vt_ref, bias_ref, lamp_ref, g_ref, o_ref,
                 s0_ref, s1_ref, acc_ref, max_ref):
    group_idx = pl.program_id(2)
    th = DIFF_TILE
    top = DIFF_GROUP - 1
    n_groups = k_ref.shape[0] // q_ref.shape[0]
    s_refs = (s0_ref, s1_ref)

    def k_tile(j):
        return k_ref[pl.ds(pl.multiple_of(j * th, th), th), :]

    def vt_tile(j):
        return vt_ref[:, pl.ds(pl.multiple_of(j * th, th), th)]

    hh = th // 2
    nt_dims = (((1,), (1,)), ((), ()))

    def start(unit, slot):
        j, t, c, masked = unit
        s_ref = s_refs[slot]
        q = q_ref[t * th:(t + 1) * th, c * LANES:(c + 1) * LANES]
        k = k_tile(j)
        if not masked:
            s = lax.dot_general(k, q, nt_dims, preferred_element_type=F32)
            s_ref[...] = s
            return jnp.max(s, axis=0, keepdims=True)
        s_early = lax.dot_general(k[:hh], q, nt_dims, preferred_element_type=F32)
        s_early = s_early + bias_ref[:hh, :]
        s_late = lax.dot_general(k[hh:], q[hh:], nt_dims, preferred_element_type=F32)
        s_late = s_late + bias_ref[hh:, hh:]
        s_ref[:hh, :] = s_early
        s_ref[hh:, hh:] = s_late
        max_early = jnp.max(s_early, axis=0, keepdims=True)
        max_late = jnp.max(s_late, axis=0, keepdims=True)
        return jnp.concatenate(
            [max_early[:, :hh], jnp.maximum(max_early[:, hh:], max_late)], axis=1)

    def consume(unit, slot, tile_max):
        j, t, c, masked = unit
        s_ref = s_refs[slot]
        idx = 2 * t + c
        m = max_ref[idx:idx + 1, :]
        m_new = jnp.maximum(m, tile_max)
        alpha = jnp.exp2(m - m_new)
        max_ref[idx:idx + 1, :] = m_new
        vt = vt_tile(j)
        if not masked:
            p = jnp.exp2(s_ref[...] - m_new)
            pv = jnp.dot(vt, p.astype(BF16), preferred_element_type=F32)
            acc_ref[idx] = alpha * acc_ref[idx] + pv
            return
        p_early = jnp.exp2(s_ref[:hh, :] - m_new).astype(BF16)
        p_late = jnp.exp2(s_ref[hh:, hh:] - m_new[:, hh:]).astype(BF16)
        pv_lo = jnp.dot(vt[:, :hh], p_early[:, :hh], preferred_element_type=F32)
        pv_hi = (jnp.dot(vt[:, :hh], p_early[:, hh:], preferred_element_type=F32)
                 + jnp.dot(vt[:, hh:], p_late, preferred_element_type=F32))
        acc_ref[idx, :, :hh] = alpha[:, :hh] * acc_ref[idx, :, :hh] + pv_lo
        acc_ref[idx, :, hh:] = alpha[:, hh:] * acc_ref[idx, :, hh:] + pv_hi

    def finish(t):
        lp = lamp_ref[...]
        lam = (jnp.exp(jnp.sum(lp[0:1] * lp[1:2], axis=-1, keepdims=True))
               - jnp.exp(jnp.sum(lp[2:3] * lp[3:4], axis=-1, keepdims=True)) + lam_init)
        l0 = acc_ref[2 * t, DIFF_VDIM:DIFF_VDIM + 1, :]
        l1 = acc_ref[2 * t + 1, DIFF_VDIM:DIFF_VDIM + 1, :]
        o = (acc_ref[2 * t, :DIFF_VDIM, :] * (1.0 / l0)
             - acc_ref[2 * t + 1, :DIFF_VDIM, :] * (lam / l1))
        ms = jnp.mean(o * o, axis=0, keepdims=True)
        y = o * lax.rsqrt(ms + NORM_EPS) * g_ref[...] * (1.0 - lam_init)
        o_ref[t * th:(t + 1) * th, :] = y.T.astype(BF16)

    def run(units, following, tile_max):
        for n, unit in enumerate(units):
            nxt = units[n + 1] if n + 1 < len(units) else following
            next_max = start(nxt, (n + 1) % 2) if nxt is not None else None
            consume(unit, n % 2, tile_max)
            tile_max = next_max
            _, t, c, masked = unit
            if masked and c == 1:
                finish(t)
        return tile_max

    def key_tile_units(j, lowest, lowest_masked):
        return [(j, t, c, lowest_masked and t == lowest)
                for t in range(top, lowest - 1, -1) for c in range(2)]

    tile_max = start((0, top, 0, False), 0)
    acc_ref[...] = jnp.zeros_like(acc_ref)
    max_ref[...] = jnp.full(max_ref.shape, NEG_BIG, F32)
    if n_groups > 1:
        tile_max = lax.fori_loop(
            0, DIFF_GROUP * group_idx,
            lambda j, mx: run(key_tile_units(j, 0, False), (j + 1, top, 0, False), mx),
            tile_max)
    j0 = DIFF_GROUP * group_idx
    tail = [u for d in range(DIFF_GROUP) for u in key_tile_units(j0 + d, d, True)]
    run(tail, None, tile_max)


def _diff_attn(qd, kd, vdt, lam_params, g_diff, lam_init, bsz, seq):
    th = DIFF_TILE
    tq = DIFF_GROUP * th
    nq = seq // tq
    key = lax.broadcasted_iota(jnp.int32, (th, th), 0)
    qry = lax.broadcasted_iota(jnp.int32, (th, th), 1)
    causal_bias = jnp.where(key <= qry, 0.0, NEG_BIG).astype(F32)
    return pl.pallas_call(
        functools.partial(_diff_kernel, lam_init),
        out_shape=jax.ShapeDtypeStruct(kd.shape, BF16),
        grid=(bsz, DIFF_HEADS, nq),
        in_specs=[
            pl.BlockSpec((tq, 2 * LANES), lambda b, h, i: (b * nq + i, h)),
            pl.BlockSpec((seq, DIFF_VDIM), lambda b, h, i: (b, h)),
            pl.BlockSpec((DIFF_VPAD, seq), lambda b, h, i: (h, b)),
            _const_spec((th, th)),
            _const_spec((4, HEAD_DIM)),
            _const_spec((DIFF_VDIM, 1)),
        ],
        out_specs=pl.BlockSpec((tq, DIFF_VDIM), lambda b, h, i: (b * nq + i, h)),
        scratch_shapes=[pltpu.VMEM((th, th), F32), pltpu.VMEM((th, th), F32),
                        pltpu.VMEM((2 * DIFF_GROUP, DIFF_VPAD, th), F32),
                        pltpu.VMEM((2 * DIFF_GROUP, th), F32)],
        compiler_params=_cparams(3),
        name="diff_attn",
    )(qd, kd, vdt, causal_bias, lam_params, g_diff.reshape(DIFF_VDIM, 1))


def _out_proj_kernel(a_ref, b_ref, x_ref, woa_ref, wob_ref, g_ref, wq_ref, x1_ref, qc_ref):
    y = (jnp.dot(a_ref[...], woa_ref[...], preferred_element_type=F32)
         + jnp.dot(b_ref[...], wob_ref[...], preferred_element_type=F32))
    x1 = x_ref[...] + y
    x1_ref[...] = x1
    hc = _rms(x1, g_ref[...]).astype(BF16)
    qc = jnp.dot(hc, wq_ref[...], preferred_element_type=F32) * (CROSS_HEAD_DIM ** -0.5 * LOG2E)
    qc_ref[...] = qc.astype(BF16)


def _out_proj(out_a, out_b, x2d, w_out, g_cross, w_cq):
    n_tok = x2d.shape[0]
    ts = TOK_TILE
    tok = lambda width: pl.BlockSpec((ts, width), lambda i: (i, 0))
    half = SWA_Q_COLS
    return pl.pallas_call(
        _out_proj_kernel,
        out_shape=[jax.ShapeDtypeStruct((n_tok, D_MODEL), F32),
                   jax.ShapeDtypeStruct((n_tok, D_MODEL), BF16)],
        grid=(n_tok // ts,),
        in_specs=[tok(half), tok(half), tok(D_MODEL), _const_spec((half, D_MODEL)),
                  _const_spec((half, D_MODEL)), _const_spec((1, D_MODEL)),
                  _const_spec((D_MODEL, D_MODEL))],
        out_specs=[tok(D_MODEL), tok(D_MODEL)],
        compiler_params=_cparams(1),
        name="out_proj",
    )(out_a, out_b, x2d, w_out[:half].astype(BF16), w_out[half:].astype(BF16),
      g_cross.reshape(1, D_MODEL), w_cq.astype(BF16))


def _mem_kv_kernel(m_ref, g_ref, w_ref, k_ref, v_ref):
    hm = _rms(m_ref[...], g_ref[...]).astype(BF16)
    k_ref[...] = jnp.dot(hm, w_ref[:, :D_MODEL], preferred_element_type=F32).astype(BF16)
    v_ref[...] = jnp.dot(hm, w_ref[:, D_MODEL:], preferred_element_type=F32).astype(BF16)


def _mem_kv(mem2d, g_mem, w_ckv, bsz, mem_len):
    blk = pl.BlockSpec((mem_len, D_MODEL), lambda b: (b, 0))
    return pl.pallas_call(
        _mem_kv_kernel,
        out_shape=[jax.ShapeDtypeStruct(mem2d.shape, BF16)] * 2,
        grid=(bsz,),
        in_specs=[blk, _const_spec((1, D_MODEL)), _const_spec((D_MODEL, 2 * D_MODEL))],
        out_specs=[blk, blk],
        compiler_params=_cparams(1),
        name="mem_kv",
    )(mem2d, g_mem.reshape(1, D_MODEL), w_ckv.astype(BF16))


def _cross_kernel(q_ref, k_ref, v_ref, wo_ref, x_ref, o_ref):
    def head_cols(h):
        return slice(h * CROSS_HEAD_DIM, (h + 1) * CROSS_HEAD_DIM)

    def scores(h):
        return lax.dot_general(q_ref[:, head_cols(h)], k_ref[:, head_cols(h)],
                               (((1,), (1,)), ((), ())), preferred_element_type=F32)

    def attend(h, s):
        m = jnp.max(s, axis=-1, keepdims=True)
        e = jnp.exp2(s - m)
        inv = 1.0 / jnp.sum(e, axis=-1, keepdims=True)
        pv = jnp.dot(e.astype(BF16), v_ref[:, head_cols(h)], preferred_element_type=F32)
        return (pv * inv).astype(BF16)

    heads = []
    s_next = scores(0)
    for h in range(CROSS_HEADS):
        s_cur = s_next
        if h + 1 < CROSS_HEADS:
            s_next = scores(h + 1)
        heads.append(attend(h, s_cur))
    o = jnp.concatenate(heads, axis=-1)
    o_ref[...] = x_ref[...] + jnp.dot(o, wo_ref[...], preferred_element_type=F32)


def _cross_attn(qc, kc, vc, w_co, x1, seq, mem_len):
    n_tok = x1.shape[0]
    ts = TOK_TILE
    per_b = seq // ts
    tok = pl.BlockSpec((ts, D_MODEL), lambda i: (i, 0))
    memblk = pl.BlockSpec((mem_len, D_MODEL), lambda i: (i // per_b, 0))
    return pl.pallas_call(
        _cross_kernel,
        out_shape=jax.ShapeDtypeStruct(x1.shape, F32),
        grid=(n_tok // ts,),
        in_specs=[tok, memblk, memblk, _const_spec((D_MODEL, D_MODEL)), tok],
        out_specs=tok,
        compiler_params=_cparams(1),
        name="cross_attn",
    )(qc, kc, vc, w_co.astype(BF16), x1)


def _mlp_kernel(final_norm, x_ref, g_ref, wu_ref, wd_ref, gf_ref, o_ref):
    x = x_ref[...]
    hb = _rms(x, g_ref[...]).astype(BF16)
    acc = x
    d_ff = wu_ref.shape[1]
    for c in range(d_ff // FF_CHUNK):
        cols = slice(c * FF_CHUNK, (c + 1) * FF_CHUNK)
        u = jnp.dot(hb, wu_ref[:, cols], preferred_element_type=F32)
        r = jnp.maximum(u, 0.0)
        acc = acc + jnp.dot((r * r).astype(BF16), wd_ref[cols, :], preferred_element_type=F32)
    o_ref[...] = _rms(acc, gf_ref[...]) if final_norm else acc


def _mlp(x2, g_mlp, w_up, w_down, g_final, final_norm):
    n_tok = x2.shape[0]
    ts = TOK_TILE
    tok = pl.BlockSpec((ts, D_MODEL), lambda i: (i, 0))
    return pl.pallas_call(
        functools.partial(_mlp_kernel, final_norm),
        out_shape=jax.ShapeDtypeStruct(x2.shape, F32),
        grid=(n_tok // ts,),
        in_specs=[tok, _const_spec((1, D_MODEL)), _const_spec(w_up.shape),
                  _const_spec(w_down.shape), _const_spec((1, D_MODEL))],
        out_specs=tok,
        compiler_params=_cparams(1),
        name="mlp",
    )(x2, g_mlp.reshape(1, D_MODEL), w_up.astype(BF16), w_down.astype(BF16),
      g_final.reshape(1, D_MODEL))


def kernel(x, mem, positions, g_mix, w_in, sinks, lambda_q1, lambda_k1, lambda_q2, lambda_k2,
           g_diff, w_out, g_cross, g_mem, w_cq, w_ckv, w_co, g_mlp, w_up, w_down, g_final):
    bsz, seq, _ = x.shape
    mem_len = mem.shape[1]
    depth = w_in.shape[0]
    cos_t, sin_t = _rope_tables(positions)
    xf = x.reshape(bsz * seq, D_MODEL)
    memf = mem.reshape(bsz * mem_len, D_MODEL)
    for l in range(depth):
        lam_init = 0.8 - 0.6 * math.exp(-0.3 * l)
        qa, kaw, vat, qd, kd, vdt = _in_proj(xf, g_mix[l], w_in[l], cos_t, sin_t)
        out_a = _swa(qa, kaw, vat, sinks[l], bsz, seq)
        lam_params = jnp.stack([lambda_q1[l], lambda_k1[l], lambda_q2[l], lambda_k2[l]])
        out_b = _diff_attn(qd, kd, vdt, lam_params, g_diff[l], lam_init, bsz, seq)
        x1, qc = _out_proj(out_a, out_b, xf, w_out[l], g_cross[l], w_cq[l])
        kc, vc = _mem_kv(memf, g_mem[l], w_ckv[l], bsz, mem_len)
        x2 = _cross_attn(qc, kc, vc, w_co[l], x1, seq, mem_len)
        xf = _mlp(x2, g_mlp[l], w_up[l], w_down[l], g_final, l == depth - 1)
    return xf.reshape(bsz, seq, D_MODEL)
```

```python
import functools
import math

import jax
import jax.numpy as jnp
from jax import lax
from jax.experimental import pallas as pl
from jax.experimental.pallas import tpu as pltpu

D_MODEL = 1024
HEAD_DIM = 64
SWA_Q_HEADS = 8
SWA_KV_HEADS = 2
SWA_GROUP = SWA_Q_HEADS // SWA_KV_HEADS
WINDOW = 128
DIFF_HEADS = 4
DIFF_VDIM = 2 * HEAD_DIM
DIFF_VPAD = DIFF_VDIM + 16
SWA_Q_COLS = SWA_Q_HEADS * HEAD_DIM
SWA_KV_COLS = SWA_KV_HEADS * HEAD_DIM
DIFF_QK_COLS = DIFF_HEADS * 2 * HEAD_DIM
DIFF_V_COLS = DIFF_HEADS * DIFF_VDIM
CROSS_HEADS = 4
CROSS_HEAD_DIM = D_MODEL // CROSS_HEADS
ROPE_THETA = 10000.0
NORM_EPS = 1e-5

LANES = 128
TOK_TILE = 1024
SWA_TILE = 1024
DIFF_TILE = 512
DIFF_GROUP = 4
FF_CHUNK = 512
SWA_AHEAD = 10
VMEM_LIMIT = 56 * 1024 * 1024
NEG_BIG = -1e30
LOG2E = math.log2(math.e)

BF16 = jnp.bfloat16
F32 = jnp.float32


def _cparams(n_axes):
    return pltpu.CompilerParams(dimension_semantics=("arbitrary",) * n_axes,
                                vmem_limit_bytes=VMEM_LIMIT)


def _rms(x, g):
    ms = jnp.mean(x * x, axis=-1, keepdims=True)
    return x * lax.rsqrt(ms + NORM_EPS) * g


def _const_spec(shape):
    return pl.BlockSpec(shape, lambda *_: (0,) * len(shape), pipeline_mode=pl.Buffered(1))


ROPE_HALF = HEAD_DIM // 2
ROPE_GROUPS = LANES // ROPE_HALF


def _rope_tables(pos, invf, sign):
    pos = pos.astype(F32)
    lane_group = lax.broadcasted_iota(jnp.int32, (pos.shape[0], LANES), 1) // ROPE_HALF
    pos_l = pos[:, ROPE_GROUPS - 1:ROPE_GROUPS]
    for g in range(ROPE_GROUPS - 2, -1, -1):
        pos_l = jnp.where(lane_group == g, pos[:, g:g + 1], pos_l)
    ang = pos_l * invf
    tables = []
    for tab in (jnp.cos(ang), jnp.sin(ang)):
        rolled = [tab] + [pltpu.roll(tab, k * ROPE_HALF, 1) for k in range(1, ROPE_GROUPS)]
        quarters = []
        for g in range(ROPE_GROUPS):
            shift = (lane_group - g) % ROPE_GROUPS
            tiled = rolled[ROPE_GROUPS - 1]
            for k in range(ROPE_GROUPS - 2, -1, -1):
                tiled = jnp.where(shift == k, rolled[k], tiled)
            quarters.append(tiled)
        tables.append(jnp.concatenate(quarters, axis=0))
    return tables[0], tables[1] * sign


def _rope_inputs(positions, tile):
    n_tok = positions.size
    inv_freq = ROPE_THETA ** (-jnp.arange(0, HEAD_DIM, 2, dtype=F32) / HEAD_DIM)
    quarter = tile // ROPE_GROUPS
    pos_cols = positions.reshape(n_tok // tile, ROPE_GROUPS, quarter)
    pos_cols = pos_cols.transpose(0, 2, 1).reshape(n_tok // ROPE_GROUPS, ROPE_GROUPS)
    invf = jnp.tile(inv_freq, ROPE_GROUPS).reshape(1, LANES)
    sign = jnp.tile(jnp.repeat(jnp.array([-1.0, 1.0], F32), ROPE_HALF), 2).reshape(1, LANES)
    return pos_cols, invf, sign


def _rope(p, cos, sin):
    width = p.shape[-1]
    lane = lax.broadcasted_iota(jnp.int32, p.shape, 1)
    first_half = (lane % HEAD_DIM) < (HEAD_DIM // 2)
    swapped = jnp.where(first_half,
                        pltpu.roll(p, width - HEAD_DIM // 2, 1),
                        pltpu.roll(p, HEAD_DIM // 2, 1))
    return p * cos + swapped * sin


def _in_proj_kernel(x_ref, g_ref, w_ref, wvt_ref, pos_ref, invf_ref, sign_ref,
                    qa_ref, kaw_ref, vat_ref, qd_ref, kd_ref, vdt_ref):
    hb = _rms(x_ref[...], g_ref[...]).astype(BF16)
    cos1, sin1 = _rope_tables(pos_ref[...], invf_ref[...], sign_ref[...])
    cos2 = jnp.concatenate([cos1, cos1], axis=-1)
    sin2 = jnp.concatenate([sin1, sin1], axis=-1)
    q_scale = HEAD_DIM ** -0.5 * LOG2E

    def proj(c0, width):
        return jnp.dot(hb, w_ref[:, c0:c0 + width], preferred_element_type=F32)

    lane = lax.broadcasted_iota(jnp.int32, cos1.shape, 1)
    low_half = lane < HEAD_DIM

    def split_halves(p, out_ref, first_group):
        for g in range(p.shape[1] // LANES):
            v = p[:, g * LANES:(g + 1) * LANES]
            c0 = (first_group + 2 * g) * LANES
            out_ref[:, c0:c0 + LANES] = jnp.where(low_half, v, 0.0).astype(BF16)
            out_ref[:, c0 + LANES:c0 + 2 * LANES] = jnp.where(low_half, 0.0, v).astype(BF16)

    for c in range(SWA_Q_COLS // 256):
        p = _rope(proj(c * 256, 256), cos2, sin2) * q_scale
        split_halves(p, qa_ref, 4 * c)
    off = SWA_Q_COLS
    ka = _rope(proj(off, SWA_KV_COLS), cos1, sin1)
    ka_swapped = pltpu.roll(ka, HEAD_DIM, 1)
    kaw_ref[:, :LANES] = jnp.where(low_half, ka, ka_swapped).astype(BF16)
    kaw_ref[:, LANES:] = jnp.where(low_half, ka_swapped, ka).astype(BF16)
    off += SWA_KV_COLS
    for c in range(DIFF_QK_COLS // 256):
        p = _rope(proj(off + c * 256, 256), cos2, sin2) * q_scale
        split_halves(p, qd_ref, 4 * c)
    off += DIFF_QK_COLS
    for c in range(DIFF_QK_COLS // 256):
        p = _rope(proj(off + c * 256, 256), cos2, sin2)
        kd_ref[:, c * 256:(c + 1) * 256] = p.astype(BF16)
    vt = lax.dot_general(wvt_ref[...], hb, (((1,), (1,)), ((), ())), preferred_element_type=F32)
    vat_ref[...] = vt[:SWA_KV_COLS].astype(BF16)
    pad_row = lax.broadcasted_iota(jnp.int32, (DIFF_VPAD - DIFF_VDIM, vt.shape[1]), 0)
    ones_rows = jnp.where(pad_row == 0, 1.0, 0.0).astype(BF16)
    for h in range(DIFF_HEADS):
        src = SWA_KV_COLS + h * DIFF_VDIM
        vdt_ref[h * DIFF_VPAD:h * DIFF_VPAD + DIFF_VDIM, :] = vt[src:src + DIFF_VDIM].astype(BF16)
        vdt_ref[h * DIFF_VPAD + DIFF_VDIM:(h + 1) * DIFF_VPAD, :] = ones_rows


def _in_proj(x2d, g_mix, w_in, positions):
    n_tok = x2d.shape[0]
    ts = TOK_TILE
    pos_cols, invf, sign = _rope_inputs(positions, ts)
    va_off = SWA_Q_COLS + SWA_KV_COLS
    qd_off = va_off + SWA_KV_COLS
    vd_off = qd_off + 2 * DIFF_QK_COLS
    w_main = jnp.concatenate([w_in[:, :va_off], w_in[:, qd_off:vd_off]], axis=1).astype(BF16)
    w_vt = jnp.concatenate([w_in[:, va_off:qd_off], w_in[:, vd_off:]], axis=1).T.astype(BF16)
    tok = lambda width: pl.BlockSpec((ts, width), lambda i: (i, 0))
    tok_t = lambda rows: pl.BlockSpec((rows, ts), lambda i: (0, i))
    qa_cols = SWA_Q_HEADS * LANES
    kaw_cols = SWA_KV_HEADS * LANES
    qd_cols = DIFF_HEADS * 2 * LANES
    out_shape = [
        jax.ShapeDtypeStruct((n_tok, qa_cols), BF16),
        jax.ShapeDtypeStruct((n_tok, kaw_cols), BF16),
        jax.ShapeDtypeStruct((SWA_KV_COLS, n_tok), BF16),
        jax.ShapeDtypeStruct((n_tok, qd_cols), BF16),
        jax.ShapeDtypeStruct((n_tok, DIFF_QK_COLS), BF16),
        jax.ShapeDtypeStruct((DIFF_HEADS * DIFF_VPAD, n_tok), BF16),
    ]
    return pl.pallas_call(
        _in_proj_kernel,
        out_shape=out_shape,
        grid=(n_tok // ts,),
        in_specs=[tok(D_MODEL), _const_spec((1, D_MODEL)), _const_spec(w_main.shape),
                  _const_spec(w_vt.shape),
                  pl.BlockSpec((ts // ROPE_GROUPS, ROPE_GROUPS), lambda i: (i, 0)),
                  _const_spec((1, LANES)), _const_spec((1, LANES))],
        out_specs=[tok(qa_cols), tok(kaw_cols), tok_t(SWA_KV_COLS), tok(qd_cols),
                   tok(DIFF_QK_COLS), tok_t(DIFF_HEADS * DIFF_VPAD)],
        compiler_params=_cparams(1),
        name="in_proj",
    )(x2d, g_mix.reshape(1, D_MODEL), w_main, w_vt, pos_cols, invf, sign)


def _swa_kernel(q_ref, kc_ref, kp_ref, vc_ref, vp_ref, sink_ref, tri_ref, o_ref):
    n_sub = q_ref.shape[0] // WINDOW
    kj = lax.broadcasted_iota(jnp.int32, (WINDOW, 2 * WINDOW), 0)
    qi = lax.broadcasted_iota(jnp.int32, (WINDOW, 2 * WINDOW), 1) % WINDOW
    from_prev = kj > qi
    no_prev_bias = jnp.where(pl.program_id(1) == 0, NEG_BIG, 0.0)
    sinks = sink_ref[...] * LOG2E
    pairs = SWA_GROUP // 2
    units = [(r, kv, pair) for r in range(n_sub) for kv in range(SWA_KV_HEADS)
             for pair in range(pairs)]

    def key_rows(r):
        if r == 0:
            return jnp.concatenate([kp_ref[...], kc_ref[0:WINDOW, :]], axis=0)
        return kc_ref[(r - 1) * WINDOW:(r + 1) * WINDOW, :]

    def value_cols(r):
        if r == 0:
            return jnp.concatenate([vp_ref[...], vc_ref[:, 0:WINDOW]], axis=1)
        return vc_ref[:, (r - 1) * WINDOW:(r + 1) * WINDOW]

    def scores(unit):
        r, kv, pair = unit
        head = kv * SWA_GROUP + 2 * pair
        q2 = jnp.concatenate(
            [q_ref[r * WINDOW:(r + 1) * WINDOW, (head + g) * LANES:(head + g + 1) * LANES]
             for g in range(2)], axis=0)
        kw = key_rows(r)[:, kv * LANES:(kv + 1) * LANES]
        s = lax.dot_general(kw, q2, (((1,), (1,)), ((), ())), preferred_element_type=F32)
        s_prev = s[:WINDOW] + no_prev_bias if r == 0 else s[:WINDOW]
        return jnp.where(from_prev, s_prev, s[WINDOW:])

    def attend(unit, s):
        r, kv, pair = unit
        tile = kv * pairs + pair
        sink = sinks[tile:tile + 1, :]
        m = jnp.maximum(jnp.max(s, axis=0, keepdims=True), sink)
        e = jnp.exp2(s - m)
        denom = jnp.sum(e, axis=0, keepdims=True) + jnp.exp2(sink - m)
        eb = e.astype(BF16)
        e2 = jnp.concatenate([eb, eb], axis=0) * tri_ref[...]
        vt = value_cols(r)[kv * HEAD_DIM:(kv + 1) * HEAD_DIM, :]
        o = jnp.dot(vt, e2, preferred_element_type=F32) / denom
        return [o[:, :WINDOW], o[:, WINDOW:]]

    pieces = []
    pending = [scores(u) for u in units[:SWA_AHEAD]]
    for idx, unit in enumerate(units):
        if idx + SWA_AHEAD < len(units):
            pending.append(scores(units[idx + SWA_AHEAD]))
        pieces += attend(unit, pending.pop(0))
        if len(pieces) == SWA_Q_HEADS:
            r = unit[0]
            o_ref[r * WINDOW:(r + 1) * WINDOW, :] = (
                jnp.concatenate(pieces, axis=0).T.astype(BF16))
            pieces = []


def _swa(qa, kaw, vat, sinks, bsz, seq):
    ts = SWA_TILE
    per_b = seq // ts
    sub = ts // WINDOW
    prev_blk = lambda b, t: jnp.maximum((b * per_b + t) * sub - 1, 0)
    cur = lambda width: pl.BlockSpec((ts, width), lambda b, t: (b * per_b + t, 0))
    sink_rows = jnp.repeat(sinks.reshape(SWA_Q_HEADS // 2, 2), WINDOW, axis=1)
    kj = lax.broadcasted_iota(jnp.int32, (WINDOW, 2 * WINDOW), 0)
    qi = lax.broadcasted_iota(jnp.int32, (WINDOW, 2 * WINDOW), 1) % WINDOW
    triangles = jnp.concatenate([kj > qi, kj <= qi], axis=0).astype(BF16)
    return pl.pallas_call(
        _swa_kernel,
        out_shape=jax.ShapeDtypeStruct((qa.shape[0], SWA_Q_COLS), BF16),
        grid=(bsz, per_b),
        in_specs=[cur(qa.shape[1]), cur(kaw.shape[1]),
                  pl.BlockSpec((WINDOW, kaw.shape[1]), lambda b, t: (prev_blk(b, t), 0)),
                  pl.BlockSpec((SWA_KV_COLS, ts), lambda b, t: (0, b * per_b + t)),
                  pl.BlockSpec((SWA_KV_COLS, WINDOW), lambda b, t: (0, prev_blk(b, t))),
                  _const_spec(sink_rows.shape), _const_spec(triangles.shape)],
        out_specs=cur(SWA_Q_COLS),
        compiler_params=_cparams(2),
        name="swa",
    )(qa, kaw, kaw, vat, vat, sink_rows, triangles)


def _diff_kernel(lam_init, q_ref, k_ref, vt_ref, bias_ref, lamp_ref, g_ref, o_ref,
                 s0_ref, s1_ref, acc_ref, max_ref):
    group_idx = pl.program_id(2)
    th = DIFF_TILE
    top = DIFF_GROUP - 1
    n_groups = k_ref.shape[0] // q_ref.shape[0]
    s_refs = (s0_ref, s1_ref)

    def k_tile(j):
        return k_ref[pl.ds(pl.multiple_of(j * th, th), th), :]

    def vt_tile(j):
        return vt_ref[:, pl.ds(pl.multiple_of(j * th, th), th)]

    hh = th // 2
    nt_dims = (((1,), (1,)), ((), ()))

    def start(unit, slot):
        j, t, c, masked = unit
        s_ref = s_refs[slot]
        q = q_ref[t * th:(t + 1) * th, c * LANES:(c + 1) * LANES]
        k = k_tile(j)
        if not masked:
            s = lax.dot_general(k, q, nt_dims, preferred_element_type=F32)
            s_ref[...] = s
            return jnp.max(s, axis=0, keepdims=True)
        s_early = lax.dot_general(k[:hh], q, nt_dims, preferred_element_type=F32)
        s_early = s_early + bias_ref[:hh, :]
        s_late = lax.dot_general(k[hh:], q[hh:], nt_dims, preferred_element_type=F32)
        s_late = s_late + bias_ref[hh:, hh:]
        s_ref[:hh, :] = s_early
        s_ref[hh:, hh:] = s_late
        max_early = jnp.max(s_early, axis=0, keepdims=True)
        max_late = jnp.max(s_late, axis=0, keepdims=True)
        return jnp.concatenate(
            [max_early[:, :hh], jnp.maximum(max_early[:, hh:], max_late)], axis=1)

    def consume(unit, slot, tile_max):
        j, t, c, masked = unit
        s_ref = s_refs[slot]
        idx = 2 * t + c
        m = max_ref[idx:idx + 1, :]
        m_new = jnp.maximum(m, tile_max)
        alpha = jnp.exp2(m - m_new)
        max_ref[idx:idx + 1, :] = m_new
        vt = vt_tile(j)
        if not masked:
            p = jnp.exp2(s_ref[...] - m_new)
            pv = jnp.dot(vt, p.astype(BF16), preferred_element_type=F32)
            acc_ref[idx] = alpha * acc_ref[idx] + pv
            return
        p_early = jnp.exp2(s_ref[:hh, :] - m_new).astype(BF16)
        p_late = jnp.exp2(s_ref[hh:, hh:] - m_new[:, hh:]).astype(BF16)
        pv_lo = jnp.dot(vt[:, :hh], p_early[:, :hh], preferred_element_type=F32)
        pv_hi = (jnp.dot(vt[:, :hh], p_early[:, hh:], preferred_element_type=F32)
                 + jnp.dot(vt[:, hh:], p_late, preferred_element_type=F32))
        acc_ref[idx, :, :hh] = alpha[:, :hh] * acc_ref[idx, :, :hh] + pv_lo
        acc_ref[idx, :, hh:] = alpha[:, hh:] * acc_ref[idx, :, hh:] + pv_hi

    def finish(t):
        lp = lamp_ref[...]
        lam = (jnp.exp(jnp.sum(lp[0:1] * lp[1:2], axis=-1, keepdims=True))
               - jnp.exp(jnp.sum(lp[2:3] * lp[3:4], axis=-1, keepdims=True)) + lam_init)
        l0 = acc_ref[2 * t, DIFF_VDIM:DIFF_VDIM + 1, :]
        l1 = acc_ref[2 * t + 1, DIFF_VDIM:DIFF_VDIM + 1, :]
        o = (acc_ref[2 * t, :DIFF_VDIM, :] * (1.0 / l0)
             - acc_ref[2 * t + 1, :DIFF_VDIM, :] * (lam / l1))
        ms = jnp.mean(o * o, axis=0, keepdims=True)
        y = o * lax.rsqrt(ms + NORM_EPS) * g_ref[...] * (1.0 - lam_init)
        o_ref[t * th:(t + 1) * th, :] = y.T.astype(BF16)

    def run(units, following, tile_max):
        for n, unit in enumerate(units):
            nxt = units[n + 1] if n + 1 < len(units) else following
            next_max = start(nxt, (n + 1) % 2) if nxt is not None else None
            consume(unit, n % 2, tile_max)
            tile_max = next_max
            _, t, c, masked = unit
            if masked and c == 1:
                finish(t)
        return tile_max

    def key_tile_units(j, lowest, lowest_masked):
        return [(j, t, c, lowest_masked and t == lowest)
                for t in range(top, lowest - 1, -1) for c in range(2)]

    tile_max = start((0, top, 0, False), 0)
    acc_ref[...] = jnp.zeros_like(acc_ref)
    max_ref[...] = jnp.full(max_ref.shape, NEG_BIG, F32)
    if n_groups > 1:
        tile_max = lax.fori_loop(
            0, DIFF_GROUP * group_idx,
            lambda j, mx: run(key_tile_units(j, 0, False), (j + 1, top, 0, False), mx),
            tile_max)
    j0 = DIFF_GROUP * group_idx
    tail = [u for d in range(DIFF_GROUP) for u in key_tile_units(j0 + d, d, True)]
    run(tail, None, tile_max)


def _diff_attn(qd, kd, vdt, lam_params, g_diff, lam_init, bsz, seq):
    th = DIFF_TILE
    tq = DIFF_GROUP * th
    nq = seq // tq
    key = lax.broadcasted_iota(jnp.int32, (th, th), 0)
    qry = lax.broadcasted_iota(jnp.int32, (th, th), 1)
    causal_bias = jnp.where(key <= qry, 0.0, NEG_BIG).astype(F32)
    return pl.pallas_call(
        functools.partial(_diff_kernel, lam_init),
        out_shape=jax.ShapeDtypeStruct(kd.shape, BF16),
        grid=(bsz, DIFF_HEADS, nq),
        in_specs=[
            pl.BlockSpec((tq, 2 * LANES), lambda b, h, i: (b * nq + i, h)),
            pl.BlockSpec((seq, DIFF_VDIM), lambda b, h, i: (b, h)),
            pl.BlockSpec((DIFF_VPAD, seq), lambda b, h, i: (h, b)),
            _const_spec((th, th)),
            _const_spec((4, HEAD_DIM)),
            _const_spec((DIFF_VDIM, 1)),
        ],
        out_specs=pl.BlockSpec((tq, DIFF_VDIM), lambda b, h, i: (b * nq + i, h)),
        scratch_shapes=[pltpu.VMEM((th, th), F32), pltpu.VMEM((th, th), F32),
                        pltpu.VMEM((2 * DIFF_GROUP, DIFF_VPAD, th), F32),
                        pltpu.VMEM((2 * DIFF_GROUP, th), F32)],
        compiler_params=_cparams(3),
        name="diff_attn",
    )(qd, kd, vdt, causal_bias, lam_params, g_diff.reshape(DIFF_VDIM, 1))


def _mem_kv_kernel(m_ref, g_ref, w_ref, k_ref, v_ref):
    hm = _rms(m_ref[...], g_ref[...]).astype(BF16)
    k_ref[...] = jnp.dot(hm, w_ref[:, :D_MODEL], preferred_element_type=F32).astype(BF16)
    v_ref[...] = jnp.dot(hm, w_ref[:, D_MODEL:], preferred_element_type=F32).astype(BF16)


def _mem_kv(mem2d, g_mem, w_ckv, bsz, mem_len):
    blk = pl.BlockSpec((mem_len, D_MODEL), lambda b: (b, 0))
    return pl.pallas_call(
        _mem_kv_kernel,
        out_shape=[jax.ShapeDtypeStruct(mem2d.shape, BF16)] * 2,
        grid=(bsz,),
        in_specs=[blk, _const_spec((1, D_MODEL)), _const_spec((D_MODEL, 2 * D_MODEL))],
        out_specs=[blk, blk],
        compiler_params=_cparams(1),
        name="mem_kv",
    )(mem2d, g_mem.reshape(1, D_MODEL), w_ckv.astype(BF16))


def _mix_cross_kernel(a_ref, b_ref, x_ref, woa_ref, wob_ref, g_ref, wq_ref, k_ref, v_ref, wo_ref,
                      o_ref):
    y = (jnp.dot(a_ref[...], woa_ref[...], preferred_element_type=F32)
         + jnp.dot(b_ref[...], wob_ref[...], preferred_element_type=F32))
    x1 = x_ref[...] + y
    hc = _rms(x1, g_ref[...]).astype(BF16)
    qc = jnp.dot(hc, wq_ref[...], preferred_element_type=F32) * (CROSS_HEAD_DIM ** -0.5 * LOG2E)
    qc = qc.astype(BF16)

    def head_cols(h):
        return slice(h * CROSS_HEAD_DIM, (h + 1) * CROSS_HEAD_DIM)

    def scores(h):
        return lax.dot_general(qc[:, head_cols(h)], k_ref[:, head_cols(h)],
                               (((1,), (1,)), ((), ())), preferred_element_type=F32)

    def attend(h, s):
        m = jnp.max(s, axis=-1, keepdims=True)
        e = jnp.exp2(s - m)
        inv = 1.0 / jnp.sum(e, axis=-1, keepdims=True)
        pv = jnp.dot(e.astype(BF16), v_ref[:, head_cols(h)], preferred_element_type=F32)
        return (pv * inv).astype(BF16)

    heads = []
    s_next = scores(0)
    for h in range(CROSS_HEADS):
        s_cur = s_next
        if h + 1 < CROSS_HEADS:
            s_next = scores(h + 1)
        heads.append(attend(h, s_cur))
    o = jnp.concatenate(heads, axis=-1)
    o_ref[...] = x1 + jnp.dot(o, wo_ref[...], preferred_element_type=F32)


def _mix_cross(out_a, out_b, x2d, w_out, g_cross, w_cq, kc, vc, w_co, seq, mem_len):
    n_tok = x2d.shape[0]
    ts = TOK_TILE
    per_b = seq // ts
    half = SWA_Q_COLS
    tok = lambda width: pl.BlockSpec((ts, width), lambda i: (i, 0))
    memblk = pl.BlockSpec((mem_len, D_MODEL), lambda i: (i // per_b, 0))
    square = _const_spec((D_MODEL, D_MODEL))
    return pl.pallas_call(
        _mix_cross_kernel,
        out_shape=jax.ShapeDtypeStruct(x2d.shape, F32),
        grid=(n_tok // ts,),
        in_specs=[tok(half), tok(half), tok(D_MODEL), _const_spec((half, D_MODEL)),
                  _const_spec((half, D_MODEL)), _const_spec((1, D_MODEL)), square,
                  memblk, memblk, square],
        out_specs=tok(D_MODEL),
        compiler_params=_cparams(1),
        name="mix_cross",
    )(out_a, out_b, x2d, w_out[:half].astype(BF16), w_out[half:].astype(BF16),
      g_cross.reshape(1, D_MODEL), w_cq.astype(BF16), kc, vc, w_co.astype(BF16))


def _mlp_kernel(final_norm, x_ref, g_ref, wu_ref, wd_ref, gf_ref, o_ref):
    x = x_ref[...]
    hb = _rms(x, g_ref[...]).astype(BF16)
    acc = x
    d_ff = wu_ref.shape[1]
    for c in range(d_ff // FF_CHUNK):
        cols = slice(c * FF_CHUNK, (c + 1) * FF_CHUNK)
        u = jnp.dot(hb, wu_ref[:, cols], preferred_element_type=F32)
        r = jnp.maximum(u, 0.0)
        acc = acc + jnp.dot((r * r).astype(BF16), wd_ref[cols, :], preferred_element_type=F32)
    o_ref[...] = _rms(acc, gf_ref[...]) if final_norm else acc


def _mlp(x2, g_mlp, w_up, w_down, g_final, final_norm):
    n_tok = x2.shape[0]
    ts = TOK_TILE
    tok = pl.BlockSpec((ts, D_MODEL), lambda i: (i, 0))
    return pl.pallas_call(
        functools.partial(_mlp_kernel, final_norm),
        out_shape=jax.ShapeDtypeStruct(x2.shape, F32),
        grid=(n_tok // ts,),
        in_specs=[tok, _const_spec((1, D_MODEL)), _const_spec(w_up.shape),
                  _const_spec(w_down.shape), _const_spec((1, D_MODEL))],
        out_specs=tok,
        compiler_params=_cparams(1),
        name="mlp",
    )(x2, g_mlp.reshape(1, D_MODEL), w_up.astype(BF16), w_down.astype(BF16),
      g_final.reshape(1, D_MODEL))


def kernel(x, mem, positions, g_mix, w_in, sinks, lambda_q1, lambda_k1, lambda_q2, lambda_k2,
           g_diff, w_out, g_cross, g_mem, w_cq, w_ckv, w_co, g_mlp, w_up, w_down, g_final):
    bsz, seq, _ = x.shape
    mem_len = mem.shape[1]
    depth = w_in.shape[0]
    xf = x.reshape(bsz * seq, D_MODEL)
    memf = mem.reshape(bsz * mem_len, D_MODEL)
    for l in range(depth):
        lam_init = 0.8 - 0.6 * math.exp(-0.3 * l)
        qa, kaw, vat, qd, kd, vdt = _in_proj(xf, g_mix[l], w_in[l], positions)
        out_a = _swa(qa, kaw, vat, sinks[l], bsz, seq)
        lam_params = jnp.stack([lambda_q1[l], lambda_k1[l], lambda_q2[l], lambda_k2[l]])
        out_b = _diff_attn(qd, kd, vdt, lam_params, g_diff[l], lam_init, bsz, seq)
        kc, vc = _mem_kv(memf, g_mem[l], w_ckv[l], bsz, mem_len)
        x2 = _mix_cross(out_a, out_b, xf, w_out[l], g_cross[l], w_cq[l], kc, vc, w_co[l],
                        seq, mem_len)
        xf = _mlp(x2, g_mlp[l], w_up[l], w_down[l], g_final, l == depth - 1)
    return xf.reshape(bsz, seq, D_MODEL)
```

```python
import functools
import math

import jax
import jax.numpy as jnp
from jax import lax
from jax.experimental import pallas as pl
from jax.experimental.pallas import tpu as pltpu

D_MODEL = 1024
HEAD_DIM = 64
SWA_Q_HEADS = 8
SWA_KV_HEADS = 2
SWA_GROUP = SWA_Q_HEADS // SWA_KV_HEADS
WINDOW = 128
DIFF_HEADS = 4
DIFF_VDIM = 2 * HEAD_DIM
DIFF_VPAD = DIFF_VDIM + 16
SWA_Q_COLS = SWA_Q_HEADS * HEAD_DIM
SWA_KV_COLS = SWA_KV_HEADS * HEAD_DIM
DIFF_QK_COLS = DIFF_HEADS * 2 * HEAD_DIM
DIFF_V_COLS = DIFF_HEADS * DIFF_VDIM
CROSS_HEADS = 4
CROSS_HEAD_DIM = D_MODEL // CROSS_HEADS
ROPE_THETA = 10000.0
NORM_EPS = 1e-5

LANES = 128
TOK_TILE = 1024
SWA_TILE = 1024
DIFF_TILE = 512
DIFF_GROUP = 4
FF_CHUNK = 512
SWA_AHEAD = 10
VMEM_LIMIT = 56 * 1024 * 1024
NEG_BIG = -1e30
LOG2E = math.log2(math.e)

BF16 = jnp.bfloat16
F32 = jnp.float32


def _cparams(n_axes):
    return pltpu.CompilerParams(dimension_semantics=("arbitrary",) * n_axes,
                                vmem_limit_bytes=VMEM_LIMIT)


def _rms(x, g):
    ms = jnp.mean(x * x, axis=-1, keepdims=True)
    return x * lax.rsqrt(ms + NORM_EPS) * g


def _const_spec(shape):
    return pl.BlockSpec(shape, lambda *_: (0,) * len(shape), pipeline_mode=pl.Buffered(1))


ROPE_HALF = HEAD_DIM // 2
ROPE_GROUPS = LANES // ROPE_HALF


def _rope_tables(pos, invf, sign):
    pos = pos.astype(F32)
    lane_group = lax.broadcasted_iota(jnp.int32, (pos.shape[0], LANES), 1) // ROPE_HALF
    pos_l = pos[:, ROPE_GROUPS - 1:ROPE_GROUPS]
    for g in range(ROPE_GROUPS - 2, -1, -1):
        pos_l = jnp.where(lane_group == g, pos[:, g:g + 1], pos_l)
    ang = pos_l * invf
    tables = []
    for tab in (jnp.cos(ang), jnp.sin(ang)):
        rolled = [tab] + [pltpu.roll(tab, k * ROPE_HALF, 1) for k in range(1, ROPE_GROUPS)]
        quarters = []
        for g in range(ROPE_GROUPS):
            shift = (lane_group - g) % ROPE_GROUPS
            tiled = rolled[ROPE_GROUPS - 1]
            for k in range(ROPE_GROUPS - 2, -1, -1):
                tiled = jnp.where(shift == k, rolled[k], tiled)
            quarters.append(tiled)
        tables.append(jnp.concatenate(quarters, axis=0))
    return tables[0], tables[1] * sign


def _rope_inputs(positions, tile):
    n_tok = positions.size
    inv_freq = ROPE_THETA ** (-jnp.arange(0, HEAD_DIM, 2, dtype=F32) / HEAD_DIM)
    quarter = tile // ROPE_GROUPS
    pos_cols = positions.reshape(n_tok // tile, ROPE_GROUPS, quarter)
    pos_cols = pos_cols.transpose(0, 2, 1).reshape(n_tok // ROPE_GROUPS, ROPE_GROUPS)
    invf = jnp.tile(inv_freq, ROPE_GROUPS).reshape(1, LANES)
    sign = jnp.tile(jnp.repeat(jnp.array([-1.0, 1.0], F32), ROPE_HALF), 2).reshape(1, LANES)
    return pos_cols, invf, sign


def _rope(p, cos, sin):
    width = p.shape[-1]
    lane = lax.broadcasted_iota(jnp.int32, p.shape, 1)
    first_half = (lane % HEAD_DIM) < (HEAD_DIM // 2)
    swapped = jnp.where(first_half,
                        pltpu.roll(p, width - HEAD_DIM // 2, 1),
                        pltpu.roll(p, HEAD_DIM // 2, 1))
    return p * cos + swapped * sin


IN_VA_OFF = SWA_Q_COLS + SWA_KV_COLS
IN_QD_OFF = IN_VA_OFF + SWA_KV_COLS
IN_VD_OFF = IN_QD_OFF + 2 * DIFF_QK_COLS


def _in_proj_kernel(x_ref, g_ref, win_ref, pos_ref, invf_ref, sign_ref,
                    qa_ref, kaw_ref, vat_ref, qd_ref, kd_ref, vdt_ref, w_ref, wvt_ref):
    @pl.when(pl.program_id(0) == 0)
    def _prepare_weights():
        w_ref[:, :IN_VA_OFF] = win_ref[:, :IN_VA_OFF].astype(BF16)
        w_ref[:, IN_VA_OFF:] = win_ref[:, IN_QD_OFF:IN_VD_OFF].astype(BF16)
        wvt_ref[:SWA_KV_COLS, :] = win_ref[:, IN_VA_OFF:IN_QD_OFF].T.astype(BF16)
        wvt_ref[SWA_KV_COLS:, :] = win_ref[:, IN_VD_OFF:].T.astype(BF16)

    hb = _rms(x_ref[...], g_ref[...]).astype(BF16)
    cos1, sin1 = _rope_tables(pos_ref[...], invf_ref[...], sign_ref[...])
    cos2 = jnp.concatenate([cos1, cos1], axis=-1)
    sin2 = jnp.concatenate([sin1, sin1], axis=-1)
    q_scale = HEAD_DIM ** -0.5 * LOG2E

    def proj(c0, width):
        return jnp.dot(hb, w_ref[:, c0:c0 + width], preferred_element_type=F32)

    lane = lax.broadcasted_iota(jnp.int32, cos1.shape, 1)
    low_half = lane < HEAD_DIM

    def split_halves(p, out_ref, first_group):
        for g in range(p.shape[1] // LANES):
            v = p[:, g * LANES:(g + 1) * LANES]
            c0 = (first_group + 2 * g) * LANES
            out_ref[:, c0:c0 + LANES] = jnp.where(low_half, v, 0.0).astype(BF16)
            out_ref[:, c0 + LANES:c0 + 2 * LANES] = jnp.where(low_half, 0.0, v).astype(BF16)

    for c in range(SWA_Q_COLS // 256):
        p = _rope(proj(c * 256, 256), cos2, sin2) * q_scale
        split_halves(p, qa_ref, 4 * c)
    off = SWA_Q_COLS
    ka = _rope(proj(off, SWA_KV_COLS), cos1, sin1)
    ka_swapped = pltpu.roll(ka, HEAD_DIM, 1)
    kaw_ref[:, :LANES] = jnp.where(low_half, ka, ka_swapped).astype(BF16)
    kaw_ref[:, LANES:] = jnp.where(low_half, ka_swapped, ka).astype(BF16)
    off += SWA_KV_COLS
    for c in range(DIFF_QK_COLS // 256):
        p = _rope(proj(off + c * 256, 256), cos2, sin2) * q_scale
        split_halves(p, qd_ref, 4 * c)
    off += DIFF_QK_COLS
    for c in range(DIFF_QK_COLS // 256):
        p = _rope(proj(off + c * 256, 256), cos2, sin2)
        kd_ref[:, c * 256:(c + 1) * 256] = p.astype(BF16)
    vt = lax.dot_general(wvt_ref[...], hb, (((1,), (1,)), ((), ())), preferred_element_type=F32)
    vat_ref[...] = vt[:SWA_KV_COLS].astype(BF16)
    pad_row = lax.broadcasted_iota(jnp.int32, (DIFF_VPAD - DIFF_VDIM, vt.shape[1]), 0)
    ones_rows = jnp.where(pad_row == 0, 1.0, 0.0).astype(BF16)
    for h in range(DIFF_HEADS):
        src = SWA_KV_COLS + h * DIFF_VDIM
        vdt_ref[h * DIFF_VPAD:h * DIFF_VPAD + DIFF_VDIM, :] = vt[src:src + DIFF_VDIM].astype(BF16)
        vdt_ref[h * DIFF_VPAD + DIFF_VDIM:(h + 1) * DIFF_VPAD, :] = ones_rows


def _in_proj(x2d, g_mix, w_in, positions):
    n_tok = x2d.shape[0]
    ts = TOK_TILE
    pos_cols, invf, sign = _rope_inputs(positions, ts)
    n_vt = SWA_KV_COLS + DIFF_V_COLS
    tok = lambda width: pl.BlockSpec((ts, width), lambda i: (i, 0))
    tok_t = lambda rows: pl.BlockSpec((rows, ts), lambda i: (0, i))
    qa_cols = SWA_Q_HEADS * LANES
    kaw_cols = SWA_KV_HEADS * LANES
    qd_cols = DIFF_HEADS * 2 * LANES
    out_shape = [
        jax.ShapeDtypeStruct((n_tok, qa_cols), BF16),
        jax.ShapeDtypeStruct((n_tok, kaw_cols), BF16),
        jax.ShapeDtypeStruct((SWA_KV_COLS, n_tok), BF16),
        jax.ShapeDtypeStruct((n_tok, qd_cols), BF16),
        jax.ShapeDtypeStruct((n_tok, DIFF_QK_COLS), BF16),
        jax.ShapeDtypeStruct((DIFF_HEADS * DIFF_VPAD, n_tok), BF16),
    ]
    return pl.pallas_call(
        _in_proj_kernel,
        out_shape=out_shape,
        grid=(n_tok // ts,),
        in_specs=[tok(D_MODEL), _const_spec((1, D_MODEL)), _const_spec(w_in.shape),
                  pl.BlockSpec((ts // ROPE_GROUPS, ROPE_GROUPS), lambda i: (i, 0)),
                  _const_spec((1, LANES)), _const_spec((1, LANES))],
        out_specs=[tok(qa_cols), tok(kaw_cols), tok_t(SWA_KV_COLS), tok(qd_cols),
                   tok(DIFF_QK_COLS), tok_t(DIFF_HEADS * DIFF_VPAD)],
        scratch_shapes=[pltpu.VMEM((D_MODEL, w_in.shape[1] - n_vt), BF16),
                        pltpu.VMEM((n_vt, D_MODEL), BF16)],
        compiler_params=_cparams(1),
        name="in_proj",
    )(x2d, g_mix.reshape(1, D_MODEL), w_in, pos_cols, invf, sign)


def _swa_kernel(q_ref, kc_ref, kp_ref, vc_ref, vp_ref, sink_ref, tri_ref, o_ref):
    n_sub = q_ref.shape[0] // WINDOW
    kj = lax.broadcasted_iota(jnp.int32, (WINDOW, 2 * WINDOW), 0)
    qi = lax.broadcasted_iota(jnp.int32, (WINDOW, 2 * WINDOW), 1) % WINDOW
    from_prev = kj > qi
    no_prev_bias = jnp.where(pl.program_id(1) == 0, NEG_BIG, 0.0)
    sinks = sink_ref[...] * LOG2E
    pairs = SWA_GROUP // 2
    units = [(r, kv, pair) for r in range(n_sub) for kv in range(SWA_KV_HEADS)
             for pair in range(pairs)]

    def key_rows(r):
        if r == 0:
            return jnp.concatenate([kp_ref[...], kc_ref[0:WINDOW, :]], axis=0)
        return kc_ref[(r - 1) * WINDOW:(r + 1) * WINDOW, :]

    def value_cols(r):
        if r == 0:
            return jnp.concatenate([vp_ref[...], vc_ref[:, 0:WINDOW]], axis=1)
        return vc_ref[:, (r - 1) * WINDOW:(r + 1) * WINDOW]

    def scores(unit):
        r, kv, pair = unit
        head = kv * SWA_GROUP + 2 * pair
        q2 = jnp.concatenate(
            [q_ref[r * WINDOW:(r + 1) * WINDOW, (head + g) * LANES:(head + g + 1) * LANES]
             for g in range(2)], axis=0)
        kw = key_rows(r)[:, kv * LANES:(kv + 1) * LANES]
        s = lax.dot_general(kw, q2, (((1,), (1,)), ((), ())), preferred_element_type=F32)
        s_prev = s[:WINDOW] + no_prev_bias if r == 0 else s[:WINDOW]
        return jnp.where(from_prev, s_prev, s[WINDOW:])

    def attend(unit, s):
        r, kv, pair = unit
        tile = kv * pairs + pair
        sink = sinks[tile:tile + 1, :]
        m = jnp.maximum(jnp.max(s, axis=0, keepdims=True), sink)
        e = jnp.exp2(s - m)
        denom = jnp.sum(e, axis=0, keepdims=True) + jnp.exp2(sink - m)
        eb = e.astype(BF16)
        e2 = jnp.concatenate([eb, eb], axis=0) * tri_ref[...]
        vt = value_cols(r)[kv * HEAD_DIM:(kv + 1) * HEAD_DIM, :]
        o = jnp.dot(vt, e2, preferred_element_type=F32) / denom
        return [o[:, :WINDOW], o[:, WINDOW:]]

    pieces = []
    pending = [scores(u) for u in units[:SWA_AHEAD]]
    for idx, unit in enumerate(units):
        if idx + SWA_AHEAD < len(units):
            pending.append(scores(units[idx + SWA_AHEAD]))
        pieces += attend(unit, pending.pop(0))
        if len(pieces) == SWA_Q_HEADS:
            r = unit[0]
            o_ref[r * WINDOW:(r + 1) * WINDOW, :] = (
                jnp.concatenate(pieces, axis=0).T.astype(BF16))
            pieces = []


def _swa(qa, kaw, vat, sinks, bsz, seq):
    ts = SWA_TILE
    per_b = seq // ts
    sub = ts // WINDOW
    prev_blk = lambda b, t: jnp.maximum((b * per_b + t) * sub - 1, 0)
    cur = lambda width: pl.BlockSpec((ts, width), lambda b, t: (b * per_b + t, 0))
    sink_rows = jnp.repeat(sinks.reshape(SWA_Q_HEADS // 2, 2), WINDOW, axis=1)
    kj = lax.broadcasted_iota(jnp.int32, (WINDOW, 2 * WINDOW), 0)
    qi = lax.broadcasted_iota(jnp.int32, (WINDOW, 2 * WINDOW), 1) % WINDOW
    triangles = jnp.concatenate([kj > qi, kj <= qi], axis=0).astype(BF16)
    return pl.pallas_call(
        _swa_kernel,
        out_shape=jax.ShapeDtypeStruct((qa.shape[0], SWA_Q_COLS), BF16),
        grid=(bsz, per_b),
        in_specs=[cur(qa.shape[1]), cur(kaw.shape[1]),
                  pl.BlockSpec((WINDOW, kaw.shape[1]), lambda b, t: (prev_blk(b, t), 0)),
                  pl.BlockSpec((SWA_KV_COLS, ts), lambda b, t: (0, b * per_b + t)),
                  pl.BlockSpec((SWA_KV_COLS, WINDOW), lambda b, t: (0, prev_blk(b, t))),
                  _const_spec(sink_rows.shape), _const_spec(triangles.shape)],
        out_specs=cur(SWA_Q_COLS),
        compiler_params=_cparams(2),
        name="swa",
    )(qa, kaw, kaw, vat, vat, sink_rows, triangles)


def _diff_kernel(lam_init, q_ref, k_ref, vt_ref, bias_ref, lamp_ref, g_ref, o_ref,
                 s0_ref, s1_ref, acc_ref, max_ref):
    group_idx = pl.program_id(2)
    th = DIFF_TILE
    top = DIFF_GROUP - 1
    n_groups = k_ref.shape[0] // q_ref.shape[0]
    s_refs = (s0_ref, s1_ref)

    def k_tile(j):
        return k_ref[pl.ds(pl.multiple_of(j * th, th), th), :]

    def vt_tile(j):
        return vt_ref[:, pl.ds(pl.multiple_of(j * th, th), th)]

    hh = th // 2
    nt_dims = (((1,), (1,)), ((), ()))

    def start(unit, slot):
        j, t, c, masked = unit
        s_ref = s_refs[slot]
        q = q_ref[t * th:(t + 1) * th, c * LANES:(c + 1) * LANES]
        k = k_tile(j)
        if not masked:
            s = lax.dot_general(k, q, nt_dims, preferred_element_type=F32)
            s_ref[...] = s
            return jnp.max(s, axis=0, keepdims=True)
        s_early = lax.dot_general(k[:hh], q, nt_dims, preferred_element_type=F32)
        s_early = s_early + bias_ref[:hh, :]
        s_late = lax.dot_general(k[hh:], q[hh:], nt_dims, preferred_element_type=F32)
        s_late = s_late + bias_ref[hh:, hh:]
        s_ref[:hh, :] = s_early
        s_ref[hh:, hh:] = s_late
        max_early = jnp.max(s_early, axis=0, keepdims=True)
        max_late = jnp.max(s_late, axis=0, keepdims=True)
        return jnp.concatenate(
            [max_early[:, :hh], jnp.maximum(max_early[:, hh:], max_late)], axis=1)

    def consume(unit, slot, tile_max):
        j, t, c, masked = unit
        s_ref = s_refs[slot]
        idx = 2 * t + c
        m = max_ref[idx:idx + 1, :]
        m_new = jnp.maximum(m, tile_max)
        alpha = jnp.exp2(m - m_new)
        max_ref[idx:idx + 1, :] = m_new
        vt = vt_tile(j)
        if not masked:
            p = jnp.exp2(s_ref[...] - m_new)
            pv = jnp.dot(vt, p.astype(BF16), preferred_element_type=F32)
            acc_ref[idx] = alpha * acc_ref[idx] + pv
            return
        p_early = jnp.exp2(s_ref[:hh, :] - m_new).astype(BF16)
        p_late = jnp.exp2(s_ref[hh:, hh:] - m_new[:, hh:]).astype(BF16)
        pv_lo = jnp.dot(vt[:, :hh], p_early[:, :hh], preferred_element_type=F32)
        pv_hi = (jnp.dot(vt[:, :hh], p_early[:, hh:], preferred_element_type=F32)
                 + jnp.dot(vt[:, hh:], p_late, preferred_element_type=F32))
        acc_ref[idx, :, :hh] = alpha[:, :hh] * acc_ref[idx, :, :hh] + pv_lo
        acc_ref[idx, :, hh:] = alpha[:, hh:] * acc_ref[idx, :, hh:] + pv_hi

    def finish(t):
        lp = lamp_ref[...]
        lam = (jnp.exp(jnp.sum(lp[0:1] * lp[1:2], axis=-1, keepdims=True))
               - jnp.exp(jnp.sum(lp[2:3] * lp[3:4], axis=-1, keepdims=True)) + lam_init)
        l0 = acc_ref[2 * t, DIFF_VDIM:DIFF_VDIM + 1, :]
        l1 = acc_ref[2 * t + 1, DIFF_VDIM:DIFF_VDIM + 1, :]
        o = (acc_ref[2 * t, :DIFF_VDIM, :] * (1.0 / l0)
             - acc_ref[2 * t + 1, :DIFF_VDIM, :] * (lam / l1))
        ms = jnp.mean(o * o, axis=0, keepdims=True)
        y = o * lax.rsqrt(ms + NORM_EPS) * g_ref[...] * (1.0 - lam_init)
        o_ref[t * th:(t + 1) * th, :] = y.T.astype(BF16)

    def run(units, following, tile_max):
        for n, unit in enumerate(units):
            nxt = units[n + 1] if n + 1 < len(units) else following
            next_max = start(nxt, (n + 1) % 2) if nxt is not None else None
            consume(unit, n % 2, tile_max)
            tile_max = next_max
            _, t, c, masked = unit
            if masked and c == 1:
                finish(t)
        return tile_max

    def key_tile_units(j, lowest, lowest_masked):
        return [(j, t, c, lowest_masked and t == lowest)
                for t in range(top, lowest - 1, -1) for c in range(2)]

    tile_max = start((0, top, 0, False), 0)
    acc_ref[...] = jnp.zeros_like(acc_ref)
    max_ref[...] = jnp.full(max_ref.shape, NEG_BIG, F32)
    def earlier_group(g, mx):
        units = [u for d in range(DIFF_GROUP)
                 for u in key_tile_units(g * DIFF_GROUP + d, 0, False)]
        return run(units, ((g + 1) * DIFF_GROUP, top, 0, False), mx)

    if n_groups > 1:
        tile_max = lax.fori_loop(0, group_idx, earlier_group, tile_max)
    j0 = DIFF_GROUP * group_idx
    tail = [u for d in range(DIFF_GROUP) for u in key_tile_units(j0 + d, d, True)]
    run(tail, None, tile_max)


def _diff_attn(qd, kd, vdt, lam_params, g_diff, lam_init, bsz, seq):
    th = DIFF_TILE
    tq = DIFF_GROUP * th
    nq = seq // tq
    key = lax.broadcasted_iota(jnp.int32, (th, th), 0)
    qry = lax.broadcasted_iota(jnp.int32, (th, th), 1)
    causal_bias = jnp.where(key <= qry, 0.0, NEG_BIG).astype(F32)
    return pl.pallas_call(
        functools.partial(_diff_kernel, lam_init),
        out_shape=jax.ShapeDtypeStruct(kd.shape, BF16),
        grid=(bsz, DIFF_HEADS, nq),
        in_specs=[
            pl.BlockSpec((tq, 2 * LANES), lambda b, h, i: (b * nq + i, h)),
            pl.BlockSpec((seq, DIFF_VDIM), lambda b, h, i: (b, h)),
            pl.BlockSpec((DIFF_VPAD, seq), lambda b, h, i: (h, b)),
            _const_spec((th, th)),
            _const_spec((4, HEAD_DIM)),
            _const_spec((DIFF_VDIM, 1)),
        ],
        out_specs=pl.BlockSpec((tq, DIFF_VDIM), lambda b, h, i: (b * nq + i, h)),
        scratch_shapes=[pltpu.VMEM((th, th), F32), pltpu.VMEM((th, th), F32),
                        pltpu.VMEM((2 * DIFF_GROUP, DIFF_VPAD, th), F32),
                        pltpu.VMEM((2 * DIFF_GROUP, th), F32)],
        compiler_params=_cparams(3),
        name="diff_attn",
    )(qd, kd, vdt, causal_bias, lam_params, g_diff.reshape(DIFF_VDIM, 1))


def _mem_kv_kernel(m_ref, g_ref, wkv_ref, k_ref, v_ref, w_ref):
    @pl.when(pl.program_id(0) == 0)
    def _prepare_weights():
        w_ref[...] = wkv_ref[...].astype(BF16)

    hm = _rms(m_ref[...], g_ref[...]).astype(BF16)
    k_ref[...] = jnp.dot(hm, w_ref[:, :D_MODEL], preferred_element_type=F32).astype(BF16)
    v_ref[...] = jnp.dot(hm, w_ref[:, D_MODEL:], preferred_element_type=F32).astype(BF16)


def _mem_kv(mem2d, g_mem, w_ckv, bsz, mem_len):
    blk = pl.BlockSpec((mem_len, D_MODEL), lambda b: (b, 0))
    return pl.pallas_call(
        _mem_kv_kernel,
        out_shape=[jax.ShapeDtypeStruct(mem2d.shape, BF16)] * 2,
        grid=(bsz,),
        in_specs=[blk, _const_spec((1, D_MODEL)), _const_spec((D_MODEL, 2 * D_MODEL))],
        out_specs=[blk, blk],
        scratch_shapes=[pltpu.VMEM(w_ckv.shape, BF16)],
        compiler_params=_cparams(1),
        name="mem_kv",
    )(mem2d, g_mem.reshape(1, D_MODEL), w_ckv)


def _mix_cross_kernel(a_ref, b_ref, x_ref, wout_ref, g_ref, wcq_ref, k_ref, v_ref, wco_ref,
                      o_ref, w_ref, wq_ref, wo_ref):
    @pl.when(pl.program_id(0) == 0)
    def _prepare_weights():
        w_ref[...] = wout_ref[...].astype(BF16)
        wq_ref[...] = wcq_ref[...].astype(BF16)
        wo_ref[...] = wco_ref[...].astype(BF16)

    half = a_ref.shape[1]
    y = (jnp.dot(a_ref[...], w_ref[:half, :], preferred_element_type=F32)
         + jnp.dot(b_ref[...], w_ref[half:, :], preferred_element_type=F32))
    x1 = x_ref[...] + y
    hc = _rms(x1, g_ref[...]).astype(BF16)
    qc = jnp.dot(hc, wq_ref[...], preferred_element_type=F32) * (CROSS_HEAD_DIM ** -0.5 * LOG2E)
    qc = qc.astype(BF16)

    def head_cols(h):
        return slice(h * CROSS_HEAD_DIM, (h + 1) * CROSS_HEAD_DIM)

    def scores(h):
        return lax.dot_general(qc[:, head_cols(h)], k_ref[:, head_cols(h)],
                               (((1,), (1,)), ((), ())), preferred_element_type=F32)

    def attend(h, s):
        m = jnp.max(s, axis=-1, keepdims=True)
        e = jnp.exp2(s - m)
        inv = 1.0 / jnp.sum(e, axis=-1, keepdims=True)
        pv = jnp.dot(e.astype(BF16), v_ref[:, head_cols(h)], preferred_element_type=F32)
        return (pv * inv).astype(BF16)

    heads = []
    s_next = scores(0)
    for h in range(CROSS_HEADS):
        s_cur = s_next
        if h + 1 < CROSS_HEADS:
            s_next = scores(h + 1)
        heads.append(attend(h, s_cur))
    o = jnp.concatenate(heads, axis=-1)
    o_ref[...] = x1 + jnp.dot(o, wo_ref[...], preferred_element_type=F32)


def _mix_cross(out_a, out_b, x2d, w_out, g_cross, w_cq, kc, vc, w_co, seq, mem_len):
    n_tok = x2d.shape[0]
    ts = TOK_TILE
    per_b = seq // ts
    half = SWA_Q_COLS
    tok = lambda width: pl.BlockSpec((ts, width), lambda i: (i, 0))
    memblk = pl.BlockSpec((mem_len, D_MODEL), lambda i: (i // per_b, 0))
    square = _const_spec((D_MODEL, D_MODEL))
    return pl.pallas_call(
        _mix_cross_kernel,
        out_shape=jax.ShapeDtypeStruct(x2d.shape, F32),
        grid=(n_tok // ts,),
        in_specs=[tok(half), tok(half), tok(D_MODEL), square, _const_spec((1, D_MODEL)), square,
                  memblk, memblk, square],
        out_specs=tok(D_MODEL),
        scratch_shapes=[pltpu.VMEM((D_MODEL, D_MODEL), BF16)] * 3,
        compiler_params=_cparams(1),
        name="mix_cross",
    )(out_a, out_b, x2d, w_out, g_cross.reshape(1, D_MODEL), w_cq, kc, vc, w_co)


def _mlp_kernel(final_norm, x_ref, g_ref, wu_ref, wd_ref, gf_ref, o_ref):
    x = x_ref[...]
    hb = _rms(x, g_ref[...]).astype(BF16)
    acc = x
    d_ff = wu_ref.shape[1]
    for c in range(d_ff // FF_CHUNK):
        cols = slice(c * FF_CHUNK, (c + 1) * FF_CHUNK)
        u = jnp.dot(hb, wu_ref[:, cols], preferred_element_type=F32)
        r = jnp.maximum(u, 0.0)
        acc = acc + jnp.dot((r * r).astype(BF16), wd_ref[cols, :], preferred_element_type=F32)
    o_ref[...] = _rms(acc, gf_ref[...]) if final_norm else acc


def _mlp(x2, g_mlp, w_up, w_down, g_final, final_norm):
    n_tok = x2.shape[0]
    ts = TOK_TILE
    tok = pl.BlockSpec((ts, D_MODEL), lambda i: (i, 0))
    return pl.pallas_call(
        functools.partial(_mlp_kernel, final_norm),
        out_shape=jax.ShapeDtypeStruct(x2.shape, F32),
        grid=(n_tok // ts,),
        in_specs=[tok, _const_spec((1, D_MODEL)), _const_spec(w_up.shape),
                  _const_spec(w_down.shape), _const_spec((1, D_MODEL))],
        out_specs=tok,
        compiler_params=_cparams(1),
        name="mlp",
    )(x2, g_mlp.reshape(1, D_MODEL), w_up.astype(BF16), w_down.astype(BF16),
      g_final.reshape(1, D_MODEL))


def kernel(x, mem, positions, g_mix, w_in, sinks, lambda_q1, lambda_k1, lambda_q2, lambda_k2,
           g_diff, w_out, g_cross, g_mem, w_cq, w_ckv, w_co, g_mlp, w_up, w_down, g_final):
    bsz, seq, _ = x.shape
    mem_len = mem.shape[1]
    depth = w_in.shape[0]
    xf = x.reshape(bsz * seq, D_MODEL)
    memf = mem.reshape(bsz * mem_len, D_MODEL)
    for l in range(depth):
        lam_init = 0.8 - 0.6 * math.exp(-0.3 * l)
        qa, kaw, vat, qd, kd, vdt = _in_proj(xf, g_mix[l], w_in[l], positions)
        out_a = _swa(qa, kaw, vat, sinks[l], bsz, seq)
        lam_params = jnp.stack([lambda_q1[l], lambda_k1[l], lambda_q2[l], lambda_k2[l]])
        out_b = _diff_attn(qd, kd, vdt, lam_params, g_diff[l], lam_init, bsz, seq)
        kc, vc = _mem_kv(memf, g_mem[l], w_ckv[l], bsz, mem_len)
        x2 = _mix_cross(out_a, out_b, xf, w_out[l], g_cross[l], w_cq[l], kc, vc, w_co[l],
                        seq, mem_len)
        xf = _mlp(x2, g_mlp[l], w_up[l], w_down[l], g_final, l == depth - 1)
    return xf.reshape(bsz, seq, D_MODEL)
```

```python
import functools
import math

import jax
import jax.numpy as jnp
from jax import lax
from jax.experimental import pallas as pl
from jax.experimental.pallas import tpu as pltpu

D_MODEL = 1024
HEAD_DIM = 64
SWA_Q_HEADS = 8
SWA_KV_HEADS = 2
SWA_GROUP = SWA_Q_HEADS // SWA_KV_HEADS
WINDOW = 128
DIFF_HEADS = 4
DIFF_VDIM = 2 * HEAD_DIM
DIFF_VPAD = DIFF_VDIM + 16
SWA_Q_COLS = SWA_Q_HEADS * HEAD_DIM
SWA_KV_COLS = SWA_KV_HEADS * HEAD_DIM
DIFF_QK_COLS = DIFF_HEADS * 2 * HEAD_DIM
DIFF_V_COLS = DIFF_HEADS * DIFF_VDIM
CROSS_HEADS = 4
CROSS_HEAD_DIM = D_MODEL // CROSS_HEADS
ROPE_THETA = 10000.0
NORM_EPS = 1e-5

LANES = 128
TOK_TILE = 1024
SWA_TILE = 1024
DIFF_TILE = 512
DIFF_GROUP = 4
FF_CHUNK = 512
SWA_AHEAD = 10
VMEM_LIMIT = 56 * 1024 * 1024
NEG_BIG = -1e30
LOG2E = math.log2(math.e)

BF16 = jnp.bfloat16
F32 = jnp.float32


def _cparams(n_axes):
    return pltpu.CompilerParams(dimension_semantics=("arbitrary",) * n_axes,
                                vmem_limit_bytes=VMEM_LIMIT)


def _rms(x, g):
    ms = jnp.mean(x * x, axis=-1, keepdims=True)
    return x * lax.rsqrt(ms + NORM_EPS) * g


def _const_spec(shape):
    return pl.BlockSpec(shape, lambda *_: (0,) * len(shape), pipeline_mode=pl.Buffered(1))


ROPE_HALF = HEAD_DIM // 2
ROPE_GROUPS = LANES // ROPE_HALF


def _rope_tables(pos, invf, sign):
    pos = pos.astype(F32)
    lane_group = lax.broadcasted_iota(jnp.int32, (pos.shape[0], LANES), 1) // ROPE_HALF
    pos_l = pos[:, ROPE_GROUPS - 1:ROPE_GROUPS]
    for g in range(ROPE_GROUPS - 2, -1, -1):
        pos_l = jnp.where(lane_group == g, pos[:, g:g + 1], pos_l)
    ang = pos_l * invf
    tables = []
    transposed = []
    for tab in (jnp.cos(ang), jnp.sin(ang)):
        rolled = [tab] + [pltpu.roll(tab, k * ROPE_HALF, 1) for k in range(1, ROPE_GROUPS)]
        quarters = []
        for g in range(ROPE_GROUPS):
            shift = (lane_group - g) % ROPE_GROUPS
            tiled = rolled[ROPE_GROUPS - 1]
            for k in range(ROPE_GROUPS - 2, -1, -1):
                tiled = jnp.where(shift == k, rolled[k], tiled)
            quarters.append(tiled)
        tables.append(jnp.concatenate(quarters, axis=0))
        tab_t = tab.T
        transposed.append(jnp.concatenate(
            [tab_t[g * ROPE_HALF:(g + 1) * ROPE_HALF, :] for g in range(ROPE_GROUPS)], axis=1))
    return tables[0], tables[1] * sign, transposed[0], transposed[1]


def _rope_inputs(positions, tile):
    n_tok = positions.size
    inv_freq = ROPE_THETA ** (-jnp.arange(0, HEAD_DIM, 2, dtype=F32) / HEAD_DIM)
    quarter = tile // ROPE_GROUPS
    pos_cols = positions.reshape(n_tok // tile, ROPE_GROUPS, quarter)
    pos_cols = pos_cols.transpose(0, 2, 1).reshape(n_tok // ROPE_GROUPS, ROPE_GROUPS)
    invf = jnp.tile(inv_freq, ROPE_GROUPS).reshape(1, LANES)
    sign = jnp.tile(jnp.repeat(jnp.array([-1.0, 1.0], F32), ROPE_HALF), 2).reshape(1, LANES)
    return pos_cols, invf, sign


def _rope(p, cos, sin):
    width = p.shape[-1]
    lane = lax.broadcasted_iota(jnp.int32, p.shape, 1)
    first_half = (lane % HEAD_DIM) < (HEAD_DIM // 2)
    swapped = jnp.where(first_half,
                        pltpu.roll(p, width - HEAD_DIM // 2, 1),
                        pltpu.roll(p, HEAD_DIM // 2, 1))
    return p * cos + swapped * sin


IN_QA = (0, SWA_Q_COLS)
IN_KA = (IN_QA[1], IN_QA[1] + SWA_KV_COLS)
IN_VA = (IN_KA[1], IN_KA[1] + SWA_KV_COLS)
IN_QD = (IN_VA[1], IN_VA[1] + DIFF_QK_COLS)
IN_KD = (IN_QD[1], IN_QD[1] + DIFF_QK_COLS)
IN_VD = (IN_KD[1], IN_KD[1] + DIFF_V_COLS)
IN_TRANSPOSED = (IN_QA, IN_QD, IN_VA, IN_VD)


def _in_proj_kernel(x_ref, g_ref, win_ref, pos_ref, invf_ref, sign_ref,
                    qat_ref, kaw_ref, vat_ref, qdt_ref, kd_ref, vdt_ref, wk_ref, wt_ref):
    @pl.when(pl.program_id(0) == 0)
    def _prepare_weights():
        wk_ref[:, :SWA_KV_COLS] = win_ref[:, IN_KA[0]:IN_KA[1]].astype(BF16)
        wk_ref[:, SWA_KV_COLS:] = win_ref[:, IN_KD[0]:IN_KD[1]].astype(BF16)
        row = 0
        for c0, c1 in IN_TRANSPOSED:
            wt_ref[row:row + c1 - c0, :] = win_ref[:, c0:c1].T.astype(BF16)
            row += c1 - c0

    hb = _rms(x_ref[...], g_ref[...]).astype(BF16)
    n_tok = hb.shape[0]
    nt_dims = (((1,), (1,)), ((), ()))

    t_rows = [0]
    for c0, c1 in IN_TRANSPOSED:
        t_rows.append(t_rows[-1] + c1 - c0)

    def proj_t(first, last):
        return lax.dot_general(wt_ref[t_rows[first]:t_rows[last], :], hb, nt_dims,
                               preferred_element_type=F32)

    vt = proj_t(2, 4)
    ka = jnp.dot(hb, wk_ref[:, :SWA_KV_COLS], preferred_element_type=F32)
    kd_parts = [jnp.dot(hb, wk_ref[:, SWA_KV_COLS + c * 256:SWA_KV_COLS + (c + 1) * 256],
                        preferred_element_type=F32) for c in range(DIFF_QK_COLS // 256)]
    qat = proj_t(0, 1)
    qdt = proj_t(1, 2)

    cos1, sin1, cos_t, sin_t = _rope_tables(pos_ref[...], invf_ref[...], sign_ref[...])
    cos2 = jnp.concatenate([cos1, cos1], axis=-1)
    sin2 = jnp.concatenate([sin1, sin1], axis=-1)

    vat_ref[...] = vt[:SWA_KV_COLS].astype(BF16)
    pad_row = lax.broadcasted_iota(jnp.int32, (DIFF_VPAD - DIFF_VDIM, n_tok), 0)
    ones_rows = jnp.where(pad_row == 0, 1.0, 0.0).astype(BF16)
    for h in range(DIFF_HEADS):
        src = SWA_KV_COLS + h * DIFF_VDIM
        vdt_ref[h * DIFF_VPAD:h * DIFF_VPAD + DIFF_VDIM, :] = vt[src:src + DIFF_VDIM].astype(BF16)
        vdt_ref[h * DIFF_VPAD + DIFF_VDIM:(h + 1) * DIFF_VPAD, :] = ones_rows

    ka = _rope(ka, cos1, sin1)
    ka_swapped = pltpu.roll(ka, HEAD_DIM, 1)
    low_half = lax.broadcasted_iota(jnp.int32, ka.shape, 1) < HEAD_DIM
    kaw_ref[:, :LANES] = jnp.where(low_half, ka, ka_swapped).astype(BF16)
    kaw_ref[:, LANES:] = jnp.where(low_half, ka_swapped, ka).astype(BF16)
    for c, p in enumerate(kd_parts):
        kd_ref[:, c * 256:(c + 1) * 256] = _rope(p, cos2, sin2).astype(BF16)

    q_scale = HEAD_DIM ** -0.5 * LOG2E
    cos_q = cos_t * q_scale
    sin_q = sin_t * q_scale
    zero_rows = jnp.zeros((HEAD_DIM, n_tok), BF16)

    def store_head(pt, src, out_ref, dst, zero_dst):
        x1 = pt[src:src + ROPE_HALF]
        x2 = pt[src + ROPE_HALF:src + HEAD_DIM]
        out_ref[dst:dst + ROPE_HALF, :] = (x1 * cos_q - x2 * sin_q).astype(BF16)
        out_ref[dst + ROPE_HALF:dst + HEAD_DIM, :] = (x2 * cos_q + x1 * sin_q).astype(BF16)
        out_ref[zero_dst:zero_dst + HEAD_DIM, :] = zero_rows

    for h in range(SWA_Q_HEADS):
        store_head(qat, h * HEAD_DIM, qat_ref, h * LANES, h * LANES + HEAD_DIM)
    for hc in range(2 * DIFF_HEADS):
        c = hc % 2
        store_head(qdt, hc * HEAD_DIM, qdt_ref, hc * LANES + c * HEAD_DIM,
                   hc * LANES + (1 - c) * HEAD_DIM)


def _in_proj(x2d, g_mix, w_in, positions):
    n_tok = x2d.shape[0]
    ts = TOK_TILE
    pos_cols, invf, sign = _rope_inputs(positions, ts)
    n_key_cols = SWA_KV_COLS + DIFF_QK_COLS
    n_t_rows = sum(c1 - c0 for c0, c1 in IN_TRANSPOSED)
    tok = lambda width: pl.BlockSpec((ts, width), lambda i: (i, 0))
    tok_t = lambda rows: pl.BlockSpec((rows, ts), lambda i: (0, i))
    qa_rows = SWA_Q_HEADS * LANES
    kaw_cols = SWA_KV_HEADS * LANES
    qd_rows = DIFF_HEADS * 2 * LANES
    out_shape = [
        jax.ShapeDtypeStruct((qa_rows, n_tok), BF16),
        jax.ShapeDtypeStruct((n_tok, kaw_cols), BF16),
        jax.ShapeDtypeStruct((SWA_KV_COLS, n_tok), BF16),
        jax.ShapeDtypeStruct((qd_rows, n_tok), BF16),
        jax.ShapeDtypeStruct((n_tok, DIFF_QK_COLS), BF16),
        jax.ShapeDtypeStruct((DIFF_HEADS * DIFF_VPAD, n_tok), BF16),
    ]
    return pl.pallas_call(
        _in_proj_kernel,
        out_shape=out_shape,
        grid=(n_tok // ts,),
        in_specs=[tok(D_MODEL), _const_spec((1, D_MODEL)), _const_spec(w_in.shape),
                  pl.BlockSpec((ts // ROPE_GROUPS, ROPE_GROUPS), lambda i: (i, 0)),
                  _const_spec((1, LANES)), _const_spec((1, LANES))],
        out_specs=[tok_t(qa_rows), tok(kaw_cols), tok_t(SWA_KV_COLS), tok_t(qd_rows),
                   tok(DIFF_QK_COLS), tok_t(DIFF_HEADS * DIFF_VPAD)],
        scratch_shapes=[pltpu.VMEM((D_MODEL, n_key_cols), BF16),
                        pltpu.VMEM((n_t_rows, D_MODEL), BF16)],
        compiler_params=_cparams(1),
        name="in_proj",
    )(x2d, g_mix.reshape(1, D_MODEL), w_in, pos_cols, invf, sign)


def _swa_kernel(qt_ref, kc_ref, kp_ref, vc_ref, vp_ref, sink_ref, tri_ref, o_ref):
    n_sub = qt_ref.shape[1] // WINDOW
    kj = lax.broadcasted_iota(jnp.int32, (WINDOW, 2 * WINDOW), 0)
    qi = lax.broadcasted_iota(jnp.int32, (WINDOW, 2 * WINDOW), 1) % WINDOW
    from_prev = kj > qi
    no_prev_bias = jnp.where(pl.program_id(1) == 0, NEG_BIG, 0.0)
    sinks = sink_ref[...] * LOG2E
    pairs = SWA_GROUP // 2
    units = [(r, kv, pair) for r in range(n_sub) for kv in range(SWA_KV_HEADS)
             for pair in range(pairs)]

    def key_rows(r):
        if r == 0:
            return jnp.concatenate([kp_ref[...], kc_ref[0:WINDOW, :]], axis=0)
        return kc_ref[(r - 1) * WINDOW:(r + 1) * WINDOW, :]

    def value_cols(r):
        if r == 0:
            return jnp.concatenate([vp_ref[...], vc_ref[:, 0:WINDOW]], axis=1)
        return vc_ref[:, (r - 1) * WINDOW:(r + 1) * WINDOW]

    def scores(unit):
        r, kv, pair = unit
        head = kv * SWA_GROUP + 2 * pair
        q2 = jnp.concatenate(
            [qt_ref[(head + g) * LANES:(head + g + 1) * LANES, r * WINDOW:(r + 1) * WINDOW]
             for g in range(2)], axis=1)
        kw = key_rows(r)[:, kv * LANES:(kv + 1) * LANES]
        s = jnp.dot(kw, q2, preferred_element_type=F32)
        s_prev = s[:WINDOW] + no_prev_bias if r == 0 else s[:WINDOW]
        return jnp.where(from_prev, s_prev, s[WINDOW:])

    def attend(unit, s):
        r, kv, pair = unit
        tile = kv * pairs + pair
        sink = sinks[tile:tile + 1, :]
        m = jnp.maximum(jnp.max(s, axis=0, keepdims=True), sink)
        e = jnp.exp2(s - m)
        denom = jnp.sum(e, axis=0, keepdims=True) + jnp.exp2(sink - m)
        eb = e.astype(BF16)
        e2 = jnp.concatenate([eb, eb], axis=0) * tri_ref[...]
        vt = value_cols(r)[kv * HEAD_DIM:(kv + 1) * HEAD_DIM, :]
        o = jnp.dot(vt, e2, preferred_element_type=F32) / denom
        return [o[:, :WINDOW], o[:, WINDOW:]]

    pieces = []
    pending = [scores(u) for u in units[:SWA_AHEAD]]
    for idx, unit in enumerate(units):
        if idx + SWA_AHEAD < len(units):
            pending.append(scores(units[idx + SWA_AHEAD]))
        pieces += attend(unit, pending.pop(0))
        if len(pieces) == SWA_Q_HEADS:
            r = unit[0]
            o_ref[r * WINDOW:(r + 1) * WINDOW, :] = (
                jnp.concatenate(pieces, axis=0).T.astype(BF16))
            pieces = []


def _swa(qat, kaw, vat, sinks, bsz, seq):
    ts = SWA_TILE
    per_b = seq // ts
    sub = ts // WINDOW
    prev_blk = lambda b, t: jnp.maximum((b * per_b + t) * sub - 1, 0)
    cur = lambda width: pl.BlockSpec((ts, width), lambda b, t: (b * per_b + t, 0))
    sink_rows = jnp.repeat(sinks.reshape(SWA_Q_HEADS // 2, 2), WINDOW, axis=1)
    kj = lax.broadcasted_iota(jnp.int32, (WINDOW, 2 * WINDOW), 0)
    qi = lax.broadcasted_iota(jnp.int32, (WINDOW, 2 * WINDOW), 1) % WINDOW
    triangles = jnp.concatenate([kj > qi, kj <= qi], axis=0).astype(BF16)
    return pl.pallas_call(
        _swa_kernel,
        out_shape=jax.ShapeDtypeStruct((kaw.shape[0], SWA_Q_COLS), BF16),
        grid=(bsz, per_b),
        in_specs=[pl.BlockSpec((qat.shape[0], ts), lambda b, t: (0, b * per_b + t)),
                  cur(kaw.shape[1]),
                  pl.BlockSpec((WINDOW, kaw.shape[1]), lambda b, t: (prev_blk(b, t), 0)),
                  pl.BlockSpec((SWA_KV_COLS, ts), lambda b, t: (0, b * per_b + t)),
                  pl.BlockSpec((SWA_KV_COLS, WINDOW), lambda b, t: (0, prev_blk(b, t))),
                  _const_spec(sink_rows.shape), _const_spec(triangles.shape)],
        out_specs=cur(SWA_Q_COLS),
        compiler_params=_cparams(2),
        name="swa",
    )(qat, kaw, kaw, vat, vat, sink_rows, triangles)


def _diff_kernel(lam_init, q_ref, k_ref, vt_ref, bias_ref, lamp_ref, g_ref, o_ref,
                 s0_ref, s1_ref, acc_ref, max_ref):
    group_idx = pl.program_id(2)
    th = DIFF_TILE
    top = DIFF_GROUP - 1
    n_groups = k_ref.shape[0] // q_ref.shape[1]
    s_refs = (s0_ref, s1_ref)

    def k_tile(j):
        return k_ref[pl.ds(pl.multiple_of(j * th, th), th), :]

    def vt_tile(j):
        return vt_ref[:, pl.ds(pl.multiple_of(j * th, th), th)]

    hh = th // 2

    def start(unit, slot):
        j, t, c, masked = unit
        s_ref = s_refs[slot]
        qt = q_ref[c * LANES:(c + 1) * LANES, t * th:(t + 1) * th]
        k = k_tile(j)
        if not masked:
            s = jnp.dot(k, qt, preferred_element_type=F32)
            s_ref[...] = s
            return jnp.max(s, axis=0, keepdims=True)
        s_early = jnp.dot(k[:hh], qt, preferred_element_type=F32) + bias_ref[:hh, :]
        s_late = jnp.dot(k[hh:], qt[:, hh:], preferred_element_type=F32) + bias_ref[hh:, hh:]
        s_ref[:hh, :] = s_early
        s_ref[hh:, hh:] = s_late
        max_early = jnp.max(s_early, axis=0, keepdims=True)
        max_late = jnp.max(s_late, axis=0, keepdims=True)
        return jnp.concatenate(
            [max_early[:, :hh], jnp.maximum(max_early[:, hh:], max_late)], axis=1)

    def consume(unit, slot, tile_max):
        j, t, c, masked = unit
        s_ref = s_refs[slot]
        idx = 2 * t + c
        m = max_ref[idx:idx + 1, :]
        m_new = jnp.maximum(m, tile_max)
        alpha = jnp.exp2(m - m_new)
        max_ref[idx:idx + 1, :] = m_new
        vt = vt_tile(j)
        if not masked:
            p = jnp.exp2(s_ref[...] - m_new)
            pv = jnp.dot(vt, p.astype(BF16), preferred_element_type=F32)
            acc_ref[idx] = alpha * acc_ref[idx] + pv
            return
        p_early = jnp.exp2(s_ref[:hh, :] - m_new).astype(BF16)
        p_late = jnp.exp2(s_ref[hh:, hh:] - m_new[:, hh:]).astype(BF16)
        pv_lo = jnp.dot(vt[:, :hh], p_early[:, :hh], preferred_element_type=F32)
        pv_hi = (jnp.dot(vt[:, :hh], p_early[:, hh:], preferred_element_type=F32)
                 + jnp.dot(vt[:, hh:], p_late, preferred_element_type=F32))
        acc_ref[idx, :, :hh] = alpha[:, :hh] * acc_ref[idx, :, :hh] + pv_lo
        acc_ref[idx, :, hh:] = alpha[:, hh:] * acc_ref[idx, :, hh:] + pv_hi

    def finish(t):
        lp = lamp_ref[...]
        lam = (jnp.exp(jnp.sum(lp[0:1] * lp[1:2], axis=-1, keepdims=True))
               - jnp.exp(jnp.sum(lp[2:3] * lp[3:4], axis=-1, keepdims=True)) + lam_init)
        l0 = acc_ref[2 * t, DIFF_VDIM:DIFF_VDIM + 1, :]
        l1 = acc_ref[2 * t + 1, DIFF_VDIM:DIFF_VDIM + 1, :]
        o = (acc_ref[2 * t, :DIFF_VDIM, :] * (1.0 / l0)
             - acc_ref[2 * t + 1, :DIFF_VDIM, :] * (lam / l1))
        ms = jnp.mean(o * o, axis=0, keepdims=True)
        y = o * lax.rsqrt(ms + NORM_EPS) * g_ref[...] * (1.0 - lam_init)
        o_ref[t * th:(t + 1) * th, :] = y.T.astype(BF16)

    def run(units, following, tile_max):
        for n, unit in enumerate(units):
            nxt = units[n + 1] if n + 1 < len(units) else following
            next_max = start(nxt, (n + 1) % 2) if nxt is not None else None
            consume(unit, n % 2, tile_max)
            tile_max = next_max
            _, t, c, masked = unit
            if masked and c == 1:
                finish(t)
        return tile_max

    def key_tile_units(j, lowest, lowest_masked):
        return [(j, t, c, lowest_masked and t == lowest)
                for t in range(top, lowest - 1, -1) for c in range(2)]

    tile_max = start((0, top, 0, False), 0)
    acc_ref[...] = jnp.zeros_like(acc_ref)
    max_ref[...] = jnp.full(max_ref.shape, NEG_BIG, F32)
    def earlier_group(g, mx):
        units = [u for d in range(DIFF_GROUP)
                 for u in key_tile_units(g * DIFF_GROUP + d, 0, False)]
        return run(units, ((g + 1) * DIFF_GROUP, top, 0, False), mx)

    if n_groups > 1:
        tile_max = lax.fori_loop(0, group_idx, earlier_group, tile_max)
    j0 = DIFF_GROUP * group_idx
    tail = [u for d in range(DIFF_GROUP) for u in key_tile_units(j0 + d, d, True)]
    run(tail, None, tile_max)


def _diff_attn(qdt, kd, vdt, lam_params, g_diff, lam_init, bsz, seq):
    th = DIFF_TILE
    tq = DIFF_GROUP * th
    nq = seq // tq
    key = lax.broadcasted_iota(jnp.int32, (th, th), 0)
    qry = lax.broadcasted_iota(jnp.int32, (th, th), 1)
    causal_bias = jnp.where(key <= qry, 0.0, NEG_BIG).astype(F32)
    return pl.pallas_call(
        functools.partial(_diff_kernel, lam_init),
        out_shape=jax.ShapeDtypeStruct(kd.shape, BF16),
        grid=(bsz, DIFF_HEADS, nq),
        in_specs=[
            pl.BlockSpec((2 * LANES, tq), lambda b, h, i: (h, b * nq + i)),
            pl.BlockSpec((seq, DIFF_VDIM), lambda b, h, i: (b, h)),
            pl.BlockSpec((DIFF_VPAD, seq), lambda b, h, i: (h, b)),
            _const_spec((th, th)),
            _const_spec((4, HEAD_DIM)),
            _const_spec((DIFF_VDIM, 1)),
        ],
        out_specs=pl.BlockSpec((tq, DIFF_VDIM), lambda b, h, i: (b * nq + i, h)),
        scratch_shapes=[pltpu.VMEM((th, th), F32), pltpu.VMEM((th, th), F32),
                        pltpu.VMEM((2 * DIFF_GROUP, DIFF_VPAD, th), F32),
                        pltpu.VMEM((2 * DIFF_GROUP, th), F32)],
        compiler_params=_cparams(3),
        name="diff_attn",
    )(qdt, kd, vdt, causal_bias, lam_params, g_diff.reshape(DIFF_VDIM, 1))


def _mem_kv_kernel(m_ref, g_ref, wkv_ref, k_ref, v_ref, w_ref):
    @pl.when(pl.program_id(0) == 0)
    def _prepare_weights():
        w_ref[...] = wkv_ref[...].astype(BF16)

    hm = _rms(m_ref[...], g_ref[...]).astype(BF16)
    k_ref[...] = jnp.dot(hm, w_ref[:, :D_MODEL], preferred_element_type=F32).astype(BF16)
    v_ref[...] = jnp.dot(hm, w_ref[:, D_MODEL:], preferred_element_type=F32).astype(BF16)


def _mem_kv(mem2d, g_mem, w_ckv, bsz, mem_len):
    blk = pl.BlockSpec((mem_len, D_MODEL), lambda b: (b, 0))
    return pl.pallas_call(
        _mem_kv_kernel,
        out_shape=[jax.ShapeDtypeStruct(mem2d.shape, BF16)] * 2,
        grid=(bsz,),
        in_specs=[blk, _const_spec((1, D_MODEL)), _const_spec((D_MODEL, 2 * D_MODEL))],
        out_specs=[blk, blk],
        scratch_shapes=[pltpu.VMEM(w_ckv.shape, BF16)],
        compiler_params=_cparams(1),
        name="mem_kv",
    )(mem2d, g_mem.reshape(1, D_MODEL), w_ckv)


def _mix_cross_kernel(a_ref, b_ref, x_ref, wout_ref, g_ref, wcq_ref, k_ref, v_ref, wco_ref,
                      o_ref, w_ref, wq_ref, wo_ref):
    @pl.when(pl.program_id(0) == 0)
    def _prepare_weights():
        w_ref[...] = wout_ref[...].astype(BF16)
        wq_ref[...] = wcq_ref[...].astype(BF16)
        wo_ref[...] = wco_ref[...].astype(BF16)

    half = a_ref.shape[1]
    y = (jnp.dot(a_ref[...], w_ref[:half, :], preferred_element_type=F32)
         + jnp.dot(b_ref[...], w_ref[half:, :], preferred_element_type=F32))
    x1 = x_ref[...] + y
    hc = _rms(x1, g_ref[...]).astype(BF16)
    qc = jnp.dot(hc, wq_ref[...], preferred_element_type=F32) * (CROSS_HEAD_DIM ** -0.5 * LOG2E)
    qc = qc.astype(BF16)

    def head_cols(h):
        return slice(h * CROSS_HEAD_DIM, (h + 1) * CROSS_HEAD_DIM)

    def scores(h):
        return lax.dot_general(qc[:, head_cols(h)], k_ref[:, head_cols(h)],
                               (((1,), (1,)), ((), ())), preferred_element_type=F32)

    def attend(h, s):
        m = jnp.max(s, axis=-1, keepdims=True)
        e = jnp.exp2(s - m)
        inv = 1.0 / jnp.sum(e, axis=-1, keepdims=True)
        pv = jnp.dot(e.astype(BF16), v_ref[:, head_cols(h)], preferred_element_type=F32)
        return (pv * inv).astype(BF16)

    heads = []
    s_next = scores(0)
    for h in range(CROSS_HEADS):
        s_cur = s_next
        if h + 1 < CROSS_HEADS:
            s_next = scores(h + 1)
        heads.append(attend(h, s_cur))
    o = jnp.concatenate(heads, axis=-1)
    o_ref[...] = x1 + jnp.dot(o, wo_ref[...], preferred_element_type=F32)


def _mix_cross(out_a, out_b, x2d, w_out, g_cross, w_cq, kc, vc, w_co, seq, mem_len):
    n_tok = x2d.shape[0]
    ts = TOK_TILE
    per_b = seq // ts
    half = SWA_Q_COLS
    tok = lambda width: pl.BlockSpec((ts, width), lambda i: (i, 0))
    memblk = pl.BlockSpec((mem_len, D_MODEL), lambda i: (i // per_b, 0))
    square = _const_spec((D_MODEL, D_MODEL))
    return pl.pallas_call(
        _mix_cross_kernel,
        out_shape=jax.ShapeDtypeStruct(x2d.shape, F32),
        grid=(n_tok // ts,),
        in_specs=[tok(half), tok(half), tok(D_MODEL), square, _const_spec((1, D_MODEL)), square,
                  memblk, memblk, square],
        out_specs=tok(D_MODEL),
        scratch_shapes=[pltpu.VMEM((D_MODEL, D_MODEL), BF16)] * 3,
        compiler_params=_cparams(1),
        name="mix_cross",
    )(out_a, out_b, x2d, w_out, g_cross.reshape(1, D_MODEL), w_cq, kc, vc, w_co)


def _mlp_kernel(final_norm, x_ref, g_ref, wu_ref, wd_ref, gf_ref, o_ref):
    x = x_ref[...]
    hb = _rms(x, g_ref[...]).astype(BF16)
    acc = x
    d_ff = wu_ref.shape[1]
    for c in range(d_ff // FF_CHUNK):
        cols = slice(c * FF_CHUNK, (c + 1) * FF_CHUNK)
        u = jnp.dot(hb, wu_ref[:, cols], preferred_element_type=F32)
        r = jnp.maximum(u, 0.0)
        acc = acc + jnp.dot((r * r).astype(BF16), wd_ref[cols, :], preferred_element_type=F32)
    o_ref[...] = _rms(acc, gf_ref[...]) if final_norm else acc


def _mlp(x2, g_mlp, w_up, w_down, g_final, final_norm):
    n_tok = x2.shape[0]
    ts = TOK_TILE
    tok = pl.BlockSpec((ts, D_MODEL), lambda i: (i, 0))
    return pl.pallas_call(
        functools.partial(_mlp_kernel, final_norm),
        out_shape=jax.ShapeDtypeStruct(x2.shape, F32),
        grid=(n_tok // ts,),
        in_specs=[tok, _const_spec((1, D_MODEL)), _const_spec(w_up.shape),
                  _const_spec(w_down.shape), _const_spec((1, D_MODEL))],
        out_specs=tok,
        compiler_params=_cparams(1),
        name="mlp",
    )(x2, g_mlp.reshape(1, D_MODEL), w_up.astype(BF16), w_down.astype(BF16),
      g_final.reshape(1, D_MODEL))


def kernel(x, mem, positions, g_mix, w_in, sinks, lambda_q1, lambda_k1, lambda_q2, lambda_k2,
           g_diff, w_out, g_cross, g_mem, w_cq, w_ckv, w_co, g_mlp, w_up, w_down, g_final):
    bsz, seq, _ = x.shape
    mem_len = mem.shape[1]
    depth = w_in.shape[0]
    xf = x.reshape(bsz * seq, D_MODEL)
    memf = mem.reshape(bsz * mem_len, D_MODEL)
    for l in range(depth):
        lam_init = 0.8 - 0.6 * math.exp(-0.3 * l)
        qat, kaw, vat, qdt, kd, vdt = _in_proj(xf, g_mix[l], w_in[l], positions)
        out_a = _swa(qat, kaw, vat, sinks[l], bsz, seq)
        lam_params = jnp.stack([lambda_q1[l], lambda_k1[l], lambda_q2[l], lambda_k2[l]])
        out_b = _diff_attn(qdt, kd, vdt, lam_params, g_diff[l], lam_init, bsz, seq)
        kc, vc = _mem_kv(memf, g_mem[l], w_ckv[l], bsz, mem_len)
        x2 = _mix_cross(out_a, out_b, xf, w_out[l], g_cross[l], w_cq[l], kc, vc, w_co[l],
                        seq, mem_len)
        xf = _mlp(x2, g_mlp[l], w_up[l], w_down[l], g_final, l == depth - 1)
    return xf.reshape(bsz, seq, D_MODEL)
```

```python
import functools
import math

import jax
import jax.numpy as jnp
from jax import lax
from jax.experimental import pallas as pl
from jax.experimental.pallas import tpu as pltpu

D_MODEL = 1024
HEAD_DIM = 64
SWA_Q_HEADS = 8
SWA_KV_HEADS = 2
SWA_GROUP = SWA_Q_HEADS // SWA_KV_HEADS
WINDOW = 128
DIFF_HEADS = 4
DIFF_VDIM = 2 * HEAD_DIM
DIFF_VPAD = DIFF_VDIM + 16
SWA_Q_COLS = SWA_Q_HEADS * HEAD_DIM
SWA_KV_COLS = SWA_KV_HEADS * HEAD_DIM
DIFF_QK_COLS = DIFF_HEADS * 2 * HEAD_DIM
DIFF_V_COLS = DIFF_HEADS * DIFF_VDIM
CROSS_HEADS = 4
CROSS_HEAD_DIM = D_MODEL // CROSS_HEADS
ROPE_THETA = 10000.0
NORM_EPS = 1e-5

LANES = 128
TOK_TILE = 1024
SWA_TILE = 1024
DIFF_TILE = 512
DIFF_GROUP = 4
FF_CHUNK = 512
SWA_AHEAD = 6
VMEM_LIMIT = 56 * 1024 * 1024
NEG_BIG = -1e30
LOG2E = math.log2(math.e)

BF16 = jnp.bfloat16
F32 = jnp.float32


def _cparams(n_axes):
    return pltpu.CompilerParams(dimension_semantics=("arbitrary",) * n_axes,
                                vmem_limit_bytes=VMEM_LIMIT)


def _rms(x, g):
    ms = jnp.mean(x * x, axis=-1, keepdims=True)
    return x * lax.rsqrt(ms + NORM_EPS) * g


def _const_spec(shape):
    return pl.BlockSpec(shape, lambda *_: (0,) * len(shape), pipeline_mode=pl.Buffered(1))


ROPE_HALF = HEAD_DIM // 2
ROPE_GROUPS = LANES // ROPE_HALF


def _rope_tables(pos, invf, sign):
    pos = pos.astype(F32)
    lane_group = lax.broadcasted_iota(jnp.int32, (pos.shape[0], LANES), 1) // ROPE_HALF
    pos_l = pos[:, ROPE_GROUPS - 1:ROPE_GROUPS]
    for g in range(ROPE_GROUPS - 2, -1, -1):
        pos_l = jnp.where(lane_group == g, pos[:, g:g + 1], pos_l)
    ang = pos_l * invf
    tables = []
    transposed = []
    for tab in (jnp.cos(ang), jnp.sin(ang)):
        rolled = [tab] + [pltpu.roll(tab, k * ROPE_HALF, 1) for k in range(1, ROPE_GROUPS)]
        quarters = []
        for g in range(ROPE_GROUPS):
            shift = (lane_group - g) % ROPE_GROUPS
            tiled = rolled[ROPE_GROUPS - 1]
            for k in range(ROPE_GROUPS - 2, -1, -1):
                tiled = jnp.where(shift == k, rolled[k], tiled)
            quarters.append(tiled)
        tables.append(jnp.concatenate(quarters, axis=0))
        tab_t = tab.T
        transposed.append(jnp.concatenate(
            [tab_t[g * ROPE_HALF:(g + 1) * ROPE_HALF, :] for g in range(ROPE_GROUPS)], axis=1))
    return tables[0], tables[1] * sign, transposed[0], transposed[1]


def _rope_inputs(positions, tile):
    n_tok = positions.size
    inv_freq = ROPE_THETA ** (-jnp.arange(0, HEAD_DIM, 2, dtype=F32) / HEAD_DIM)
    quarter = tile // ROPE_GROUPS
    pos_cols = positions.reshape(n_tok // tile, ROPE_GROUPS, quarter)
    pos_cols = pos_cols.transpose(0, 2, 1).reshape(n_tok // ROPE_GROUPS, ROPE_GROUPS)
    invf = jnp.tile(inv_freq, ROPE_GROUPS).reshape(1, LANES)
    sign = jnp.tile(jnp.repeat(jnp.array([-1.0, 1.0], F32), ROPE_HALF), 2).reshape(1, LANES)
    return pos_cols, invf, sign


def _rope(p, cos, sin):
    width = p.shape[-1]
    lane = lax.broadcasted_iota(jnp.int32, p.shape, 1)
    first_half = (lane % HEAD_DIM) < (HEAD_DIM // 2)
    swapped = jnp.where(first_half,
                        pltpu.roll(p, width - HEAD_DIM // 2, 1),
                        pltpu.roll(p, HEAD_DIM // 2, 1))
    return p * cos + swapped * sin


IN_QA = (0, SWA_Q_COLS)
IN_KA = (IN_QA[1], IN_QA[1] + SWA_KV_COLS)
IN_VA = (IN_KA[1], IN_KA[1] + SWA_KV_COLS)
IN_QD = (IN_VA[1], IN_VA[1] + DIFF_QK_COLS)
IN_KD = (IN_QD[1], IN_QD[1] + DIFF_QK_COLS)
IN_VD = (IN_KD[1], IN_KD[1] + DIFF_V_COLS)
IN_TRANSPOSED = (IN_QA, IN_QD, IN_VA, IN_VD)


def _in_proj_kernel(x_ref, g_ref, win_ref, pos_ref, invf_ref, sign_ref,
                    qat_ref, kaw_ref, vat_ref, qdt_ref, kd_ref, vdt_ref, wk_ref, wt_ref):
    @pl.when(pl.program_id(0) == 0)
    def _prepare_weights():
        wk_ref[:, :SWA_KV_COLS] = win_ref[:, IN_KA[0]:IN_KA[1]].astype(BF16)
        wk_ref[:, SWA_KV_COLS:] = win_ref[:, IN_KD[0]:IN_KD[1]].astype(BF16)
        row = 0
        for c0, c1 in IN_TRANSPOSED:
            wt_ref[row:row + c1 - c0, :] = win_ref[:, c0:c1].T.astype(BF16)
            row += c1 - c0

    hb = _rms(x_ref[...], g_ref[...]).astype(BF16)
    n_tok = hb.shape[0]
    nt_dims = (((1,), (1,)), ((), ()))

    t_rows = [0]
    for c0, c1 in IN_TRANSPOSED:
        t_rows.append(t_rows[-1] + c1 - c0)

    def proj_t(first, last):
        return lax.dot_general(wt_ref[t_rows[first]:t_rows[last], :], hb, nt_dims,
                               preferred_element_type=F32)

    vt = proj_t(2, 4)
    ka = jnp.dot(hb, wk_ref[:, :SWA_KV_COLS], preferred_element_type=F32)
    kd_parts = [jnp.dot(hb, wk_ref[:, SWA_KV_COLS + c * 256:SWA_KV_COLS + (c + 1) * 256],
                        preferred_element_type=F32) for c in range(DIFF_QK_COLS // 256)]
    qat = proj_t(0, 1)
    qdt = proj_t(1, 2)

    cos1, sin1, cos_t, sin_t = _rope_tables(pos_ref[...], invf_ref[...], sign_ref[...])
    cos2 = jnp.concatenate([cos1, cos1], axis=-1)
    sin2 = jnp.concatenate([sin1, sin1], axis=-1)

    vat_ref[...] = vt[:SWA_KV_COLS].astype(BF16)
    pad_row = lax.broadcasted_iota(jnp.int32, (DIFF_VPAD - DIFF_VDIM, n_tok), 0)
    ones_rows = jnp.where(pad_row == 0, 1.0, 0.0).astype(BF16)
    for h in range(DIFF_HEADS):
        src = SWA_KV_COLS + h * DIFF_VDIM
        vdt_ref[h * DIFF_VPAD:h * DIFF_VPAD + DIFF_VDIM, :] = vt[src:src + DIFF_VDIM].astype(BF16)
        vdt_ref[h * DIFF_VPAD + DIFF_VDIM:(h + 1) * DIFF_VPAD, :] = ones_rows

    ka = _rope(ka, cos1, sin1)
    ka_swapped = pltpu.roll(ka, HEAD_DIM, 1)
    low_half = lax.broadcasted_iota(jnp.int32, ka.shape, 1) < HEAD_DIM
    kaw_ref[:, :LANES] = jnp.where(low_half, ka, ka_swapped).astype(BF16)
    kaw_ref[:, LANES:] = jnp.where(low_half, ka_swapped, ka).astype(BF16)
    for c, p in enumerate(kd_parts):
        kd_ref[:, c * 256:(c + 1) * 256] = _rope(p, cos2, sin2).astype(BF16)

    q_scale = HEAD_DIM ** -0.5 * LOG2E
    cos_q = cos_t * q_scale
    sin_q = sin_t * q_scale
    zero_rows = jnp.zeros((HEAD_DIM, n_tok), BF16)

    def store_head(pt, src, out_ref, dst, zero_dst):
        x1 = pt[src:src + ROPE_HALF]
        x2 = pt[src + ROPE_HALF:src + HEAD_DIM]
        out_ref[dst:dst + ROPE_HALF, :] = (x1 * cos_q - x2 * sin_q).astype(BF16)
        out_ref[dst + ROPE_HALF:dst + HEAD_DIM, :] = (x2 * cos_q + x1 * sin_q).astype(BF16)
        out_ref[zero_dst:zero_dst + HEAD_DIM, :] = zero_rows

    for h in range(SWA_Q_HEADS):
        store_head(qat, h * HEAD_DIM, qat_ref, h * LANES, h * LANES + HEAD_DIM)
    for hc in range(2 * DIFF_HEADS):
        c = hc % 2
        store_head(qdt, hc * HEAD_DIM, qdt_ref, hc * LANES + c * HEAD_DIM,
                   hc * LANES + (1 - c) * HEAD_DIM)


def _in_proj(x2d, g_mix, w_in, positions):
    n_tok = x2d.shape[0]
    ts = TOK_TILE
    pos_cols, invf, sign = _rope_inputs(positions, ts)
    n_key_cols = SWA_KV_COLS + DIFF_QK_COLS
    n_t_rows = sum(c1 - c0 for c0, c1 in IN_TRANSPOSED)
    tok = lambda width: pl.BlockSpec((ts, width), lambda i: (i, 0))
    tok_t = lambda rows: pl.BlockSpec((rows, ts), lambda i: (0, i))
    qa_rows = SWA_Q_HEADS * LANES
    kaw_cols = SWA_KV_HEADS * LANES
    qd_rows = DIFF_HEADS * 2 * LANES
    out_shape = [
        jax.ShapeDtypeStruct((qa_rows, n_tok), BF16),
        jax.ShapeDtypeStruct((n_tok, kaw_cols), BF16),
        jax.ShapeDtypeStruct((SWA_KV_COLS, n_tok), BF16),
        jax.ShapeDtypeStruct((qd_rows, n_tok), BF16),
        jax.ShapeDtypeStruct((n_tok, DIFF_QK_COLS), BF16),
        jax.ShapeDtypeStruct((DIFF_HEADS * DIFF_VPAD, n_tok), BF16),
    ]
    return pl.pallas_call(
        _in_proj_kernel,
        out_shape=out_shape,
        grid=(n_tok // ts,),
        in_specs=[tok(D_MODEL), _const_spec((1, D_MODEL)), _const_spec(w_in.shape),
                  pl.BlockSpec((ts // ROPE_GROUPS, ROPE_GROUPS), lambda i: (i, 0)),
                  _const_spec((1, LANES)), _const_spec((1, LANES))],
        out_specs=[tok_t(qa_rows), tok(kaw_cols), tok_t(SWA_KV_COLS), tok_t(qd_rows),
                   tok(DIFF_QK_COLS), tok_t(DIFF_HEADS * DIFF_VPAD)],
        scratch_shapes=[pltpu.VMEM((D_MODEL, n_key_cols), BF16),
                        pltpu.VMEM((n_t_rows, D_MODEL), BF16)],
        compiler_params=_cparams(1),
        name="in_proj",
    )(x2d, g_mix.reshape(1, D_MODEL), w_in, pos_cols, invf, sign)


def _swa_kernel(qt_ref, kc_ref, kp_ref, vc_ref, vp_ref, sink_ref, tri_ref, o_ref):
    n_sub = qt_ref.shape[1] // WINDOW
    kj = lax.broadcasted_iota(jnp.int32, (WINDOW, 2 * WINDOW), 0)
    qi = lax.broadcasted_iota(jnp.int32, (WINDOW, 2 * WINDOW), 1) % WINDOW
    from_prev = kj > qi
    no_prev_bias = jnp.where(pl.program_id(1) == 0, NEG_BIG, 0.0)
    sinks = sink_ref[...] * LOG2E
    pairs = SWA_GROUP // 2
    units = [(r, kv, pair) for r in range(n_sub) for kv in range(SWA_KV_HEADS)
             for pair in range(pairs)]

    def key_rows(r):
        if r == 0:
            return jnp.concatenate([kp_ref[...], kc_ref[0:WINDOW, :]], axis=0)
        return kc_ref[(r - 1) * WINDOW:(r + 1) * WINDOW, :]

    def value_cols(r):
        if r == 0:
            return jnp.concatenate([vp_ref[...], vc_ref[:, 0:WINDOW]], axis=1)
        return vc_ref[:, (r - 1) * WINDOW:(r + 1) * WINDOW]

    def scores(unit):
        r, kv, pair = unit
        head = kv * SWA_GROUP + 2 * pair
        q2 = jnp.concatenate(
            [qt_ref[(head + g) * LANES:(head + g + 1) * LANES, r * WINDOW:(r + 1) * WINDOW]
             for g in range(2)], axis=1)
        kw = key_rows(r)[:, kv * LANES:(kv + 1) * LANES]
        s = jnp.dot(kw, q2, preferred_element_type=F32)
        s_prev = s[:WINDOW] + no_prev_bias if r == 0 else s[:WINDOW]
        return jnp.where(from_prev, s_prev, s[WINDOW:])

    def attend(unit, s):
        r, kv, pair = unit
        tile = kv * pairs + pair
        sink = sinks[tile:tile + 1, :]
        m = jnp.maximum(jnp.max(s, axis=0, keepdims=True), sink)
        e = jnp.exp2(s - m)
        denom = jnp.sum(e, axis=0, keepdims=True) + jnp.exp2(sink - m)
        eb = e.astype(BF16)
        e2 = jnp.concatenate([eb, eb], axis=0) * tri_ref[...]
        vt = value_cols(r)[kv * HEAD_DIM:(kv + 1) * HEAD_DIM, :]
        o = jnp.dot(vt, e2, preferred_element_type=F32) / denom
        return [o[:, :WINDOW], o[:, WINDOW:]]

    pieces = []
    pending = [scores(u) for u in units[:SWA_AHEAD]]
    for idx, unit in enumerate(units):
        if idx + SWA_AHEAD < len(units):
            pending.append(scores(units[idx + SWA_AHEAD]))
        pieces += attend(unit, pending.pop(0))
        if len(pieces) == SWA_Q_HEADS:
            r = unit[0]
            o_ref[r * WINDOW:(r + 1) * WINDOW, :] = (
                jnp.concatenate(pieces, axis=0).T.astype(BF16))
            pieces = []


def _swa(qat, kaw, vat, sinks, bsz, seq):
    ts = SWA_TILE
    per_b = seq // ts
    sub = ts // WINDOW
    prev_blk = lambda b, t: jnp.maximum((b * per_b + t) * sub - 1, 0)
    cur = lambda width: pl.BlockSpec((ts, width), lambda b, t: (b * per_b + t, 0))
    sink_rows = jnp.repeat(sinks.reshape(SWA_Q_HEADS // 2, 2), WINDOW, axis=1)
    kj = lax.broadcasted_iota(jnp.int32, (WINDOW, 2 * WINDOW), 0)
    qi = lax.broadcasted_iota(jnp.int32, (WINDOW, 2 * WINDOW), 1) % WINDOW
    triangles = jnp.concatenate([kj > qi, kj <= qi], axis=0).astype(BF16)
    return pl.pallas_call(
        _swa_kernel,
        out_shape=jax.ShapeDtypeStruct((kaw.shape[0], SWA_Q_COLS), BF16),
        grid=(bsz, per_b),
        in_specs=[pl.BlockSpec((qat.shape[0], ts), lambda b, t: (0, b * per_b + t)),
                  cur(kaw.shape[1]),
                  pl.BlockSpec((WINDOW, kaw.shape[1]), lambda b, t: (prev_blk(b, t), 0)),
                  pl.BlockSpec((SWA_KV_COLS, ts), lambda b, t: (0, b * per_b + t)),
                  pl.BlockSpec((SWA_KV_COLS, WINDOW), lambda b, t: (0, prev_blk(b, t))),
                  _const_spec(sink_rows.shape), _const_spec(triangles.shape)],
        out_specs=cur(SWA_Q_COLS),
        compiler_params=_cparams(2),
        name="swa",
    )(qat, kaw, kaw, vat, vat, sink_rows, triangles)


def _diff_kernel(lam_init, q_ref, k_ref, vt_ref, bias_ref, lamp_ref, g_ref, o_ref,
                 s0_ref, s1_ref, acc_ref, max_ref):
    group_idx = pl.program_id(2)
    th = DIFF_TILE
    top = DIFF_GROUP - 1
    n_groups = k_ref.shape[0] // q_ref.shape[1]
    s_refs = (s0_ref, s1_ref)

    def k_tile(j):
        return k_ref[pl.ds(pl.multiple_of(j * th, th), th), :]

    def vt_tile(j):
        return vt_ref[:, pl.ds(pl.multiple_of(j * th, th), th)]

    hh = th // 2

    def start(unit, slot):
        j, t, c, masked = unit
        s_ref = s_refs[slot]
        qt = q_ref[c * LANES:(c + 1) * LANES, t * th:(t + 1) * th]
        k = k_tile(j)
        if not masked:
            s = jnp.dot(k, qt, preferred_element_type=F32)
            s_ref[...] = s
            return jnp.max(s, axis=0, keepdims=True)
        s_early = jnp.dot(k[:hh], qt, preferred_element_type=F32) + bias_ref[:hh, :]
        s_late = jnp.dot(k[hh:], qt[:, hh:], preferred_element_type=F32) + bias_ref[hh:, hh:]
        s_ref[:hh, :] = s_early
        s_ref[hh:, hh:] = s_late
        max_early = jnp.max(s_early, axis=0, keepdims=True)
        max_late = jnp.max(s_late, axis=0, keepdims=True)
        return jnp.concatenate(
            [max_early[:, :hh], jnp.maximum(max_early[:, hh:], max_late)], axis=1)

    def consume(unit, slot, tile_max):
        j, t, c, masked = unit
        s_ref = s_refs[slot]
        idx = 2 * t + c
        m = max_ref[idx:idx + 1, :]
        m_new = jnp.maximum(m, tile_max)
        alpha = jnp.exp2(m - m_new)
        max_ref[idx:idx + 1, :] = m_new
        vt = vt_tile(j)
        if not masked:
            p = jnp.exp2(s_ref[...] - m_new)
            pv = jnp.dot(vt, p.astype(BF16), preferred_element_type=F32)
            acc_ref[idx] = alpha * acc_ref[idx] + pv
            return
        p_early = jnp.exp2(s_ref[:hh, :] - m_new).astype(BF16)
        p_late = jnp.exp2(s_ref[hh:, hh:] - m_new[:, hh:]).astype(BF16)
        pv_lo = jnp.dot(vt[:, :hh], p_early[:, :hh], preferred_element_type=F32)
        pv_hi = (jnp.dot(vt[:, :hh], p_early[:, hh:], preferred_element_type=F32)
                 + jnp.dot(vt[:, hh:], p_late, preferred_element_type=F32))
        acc_ref[idx, :, :hh] = alpha[:, :hh] * acc_ref[idx, :, :hh] + pv_lo
        acc_ref[idx, :, hh:] = alpha[:, hh:] * acc_ref[idx, :, hh:] + pv_hi

    def finish(t):
        lp = lamp_ref[...]
        lam = (jnp.exp(jnp.sum(lp[0:1] * lp[1:2], axis=-1, keepdims=True))
               - jnp.exp(jnp.sum(lp[2:3] * lp[3:4], axis=-1, keepdims=True)) + lam_init)
        l0 = acc_ref[2 * t, DIFF_VDIM:DIFF_VDIM + 1, :]
        l1 = acc_ref[2 * t + 1, DIFF_VDIM:DIFF_VDIM + 1, :]
        o = (acc_ref[2 * t, :DIFF_VDIM, :] * (1.0 / l0)
             - acc_ref[2 * t + 1, :DIFF_VDIM, :] * (lam / l1))
        ms = jnp.mean(o * o, axis=0, keepdims=True)
        y = o * lax.rsqrt(ms + NORM_EPS) * g_ref[...] * (1.0 - lam_init)
        o_ref[t * th:(t + 1) * th, :] = y.T.astype(BF16)

    def run(units, following, tile_max):
        for n, unit in enumerate(units):
            nxt = units[n + 1] if n + 1 < len(units) else following
            next_max = start(nxt, (n + 1) % 2) if nxt is not None else None
            consume(unit, n % 2, tile_max)
            tile_max = next_max
            _, t, c, masked = unit
            if masked and c == 1:
                finish(t)
        return tile_max

    def key_tile_units(j, lowest, lowest_masked):
        return [(j, t, c, lowest_masked and t == lowest)
                for t in range(top, lowest - 1, -1) for c in range(2)]

    tile_max = start((0, top, 0, False), 0)
    acc_ref[...] = jnp.zeros_like(acc_ref)
    max_ref[...] = jnp.full(max_ref.shape, NEG_BIG, F32)
    def earlier_group(g, mx):
        units = [u for d in range(DIFF_GROUP)
                 for u in key_tile_units(g * DIFF_GROUP + d, 0, False)]
        return run(units, ((g + 1) * DIFF_GROUP, top, 0, False), mx)

    if n_groups > 1:
        tile_max = lax.fori_loop(0, group_idx, earlier_group, tile_max)
    j0 = DIFF_GROUP * group_idx
    tail = [u for d in range(DIFF_GROUP) for u in key_tile_units(j0 + d, d, True)]
    run(tail, None, tile_max)


def _diff_attn(qdt, kd, vdt, lam_params, g_diff, lam_init, bsz, seq):
    th = DIFF_TILE
    tq = DIFF_GROUP * th
    nq = seq // tq
    key = lax.broadcasted_iota(jnp.int32, (th, th), 0)
    qry = lax.broadcasted_iota(jnp.int32, (th, th), 1)
    causal_bias = jnp.where(key <= qry, 0.0, NEG_BIG).astype(F32)
    return pl.pallas_call(
        functools.partial(_diff_kernel, lam_init),
        out_shape=jax.ShapeDtypeStruct(kd.shape, BF16),
        grid=(bsz, DIFF_HEADS, nq),
        in_specs=[
            pl.BlockSpec((2 * LANES, tq), lambda b, h, i: (h, b * nq + i)),
            pl.BlockSpec((seq, DIFF_VDIM), lambda b, h, i: (b, h)),
            pl.BlockSpec((DIFF_VPAD, seq), lambda b, h, i: (h, b)),
            _const_spec((th, th)),
            _const_spec((4, HEAD_DIM)),
            _const_spec((DIFF_VDIM, 1)),
        ],
        out_specs=pl.BlockSpec((tq, DIFF_VDIM), lambda b, h, i: (b * nq + i, h)),
        scratch_shapes=[pltpu.VMEM((th, th), F32), pltpu.VMEM((th, th), F32),
                        pltpu.VMEM((2 * DIFF_GROUP, DIFF_VPAD, th), F32),
                        pltpu.VMEM((2 * DIFF_GROUP, th), F32)],
        compiler_params=_cparams(3),
        name="diff_attn",
    )(qdt, kd, vdt, causal_bias, lam_params, g_diff.reshape(DIFF_VDIM, 1))


def _mem_kv_kernel(m_ref, g_ref, wkv_ref, k_ref, v_ref, w_ref):
    @pl.when(pl.program_id(0) == 0)
    def _prepare_weights():
        w_ref[...] = wkv_ref[...].astype(BF16)

    hm = _rms(m_ref[...], g_ref[...]).astype(BF16)
    k_ref[...] = jnp.dot(hm, w_ref[:, :D_MODEL], preferred_element_type=F32).astype(BF16)
    v_ref[...] = jnp.dot(hm, w_ref[:, D_MODEL:], preferred_element_type=F32).astype(BF16)


def _mem_kv(mem2d, g_mem, w_ckv, bsz, mem_len):
    blk = pl.BlockSpec((mem_len, D_MODEL), lambda b: (b, 0))
    return pl.pallas_call(
        _mem_kv_kernel,
        out_shape=[jax.ShapeDtypeStruct(mem2d.shape, BF16)] * 2,
        grid=(bsz,),
        in_specs=[blk, _const_spec((1, D_MODEL)), _const_spec((D_MODEL, 2 * D_MODEL))],
        out_specs=[blk, blk],
        scratch_shapes=[pltpu.VMEM(w_ckv.shape, BF16)],
        compiler_params=_cparams(1),
        name="mem_kv",
    )(mem2d, g_mem.reshape(1, D_MODEL), w_ckv)


def _mix_cross_kernel(a_ref, b_ref, x_ref, wout_ref, g_ref, wcq_ref, k_ref, v_ref, wco_ref,
                      wup_ref, wdown_ref, o_ref, wup_bf_ref, wdown_bf_ref,
                      w_ref, wq_ref, wo_ref):
    @pl.when(pl.program_id(0) == 0)
    def _prepare_weights():
        w_ref[...] = wout_ref[...].astype(BF16)
        wq_ref[...] = wcq_ref[...].astype(BF16)
        wo_ref[...] = wco_ref[...].astype(BF16)

    wup_bf_ref[...] = wup_ref[...].astype(BF16)
    wdown_bf_ref[...] = wdown_ref[...].astype(BF16)

    half = a_ref.shape[1]
    y = (jnp.dot(a_ref[...], w_ref[:half, :], preferred_element_type=F32)
         + jnp.dot(b_ref[...], w_ref[half:, :], preferred_element_type=F32))
    x1 = x_ref[...] + y
    hc = _rms(x1, g_ref[...]).astype(BF16)
    qc = jnp.dot(hc, wq_ref[...], preferred_element_type=F32) * (CROSS_HEAD_DIM ** -0.5 * LOG2E)
    qc = qc.astype(BF16)

    def head_cols(h):
        return slice(h * CROSS_HEAD_DIM, (h + 1) * CROSS_HEAD_DIM)

    def scores(h):
        return lax.dot_general(qc[:, head_cols(h)], k_ref[:, head_cols(h)],
                               (((1,), (1,)), ((), ())), preferred_element_type=F32)

    def attend(h, s):
        m = jnp.max(s, axis=-1, keepdims=True)
        e = jnp.exp2(s - m)
        inv = 1.0 / jnp.sum(e, axis=-1, keepdims=True)
        pv = jnp.dot(e.astype(BF16), v_ref[:, head_cols(h)], preferred_element_type=F32)
        return (pv * inv).astype(BF16)

    heads = []
    s_next = scores(0)
    for h in range(CROSS_HEADS):
        s_cur = s_next
        if h + 1 < CROSS_HEADS:
            s_next = scores(h + 1)
        heads.append(attend(h, s_cur))
    o = jnp.concatenate(heads, axis=-1)
    o_ref[...] = x1 + jnp.dot(o, wo_ref[...], preferred_element_type=F32)


def _mix_cross(out_a, out_b, x2d, w_out, g_cross, w_cq, kc, vc, w_co, w_up, w_down, seq,
               mem_len):
    n_tok = x2d.shape[0]
    ts = TOK_TILE
    n_steps = n_tok // ts
    per_b = seq // ts
    half = SWA_Q_COLS
    tok = lambda width: pl.BlockSpec((ts, width), lambda i: (i, 0))
    memblk = pl.BlockSpec((mem_len, D_MODEL), lambda i: (i // per_b, 0))
    square = _const_spec((D_MODEL, D_MODEL))
    slab = lambda w: pl.BlockSpec((w.shape[0] // n_steps, w.shape[1]), lambda i: (i, 0))
    return pl.pallas_call(
        _mix_cross_kernel,
        out_shape=[jax.ShapeDtypeStruct(x2d.shape, F32),
                   jax.ShapeDtypeStruct(w_up.shape, BF16),
                   jax.ShapeDtypeStruct(w_down.shape, BF16)],
        grid=(n_steps,),
        in_specs=[tok(half), tok(half), tok(D_MODEL), square, _const_spec((1, D_MODEL)), square,
                  memblk, memblk, square, slab(w_up), slab(w_down)],
        out_specs=[tok(D_MODEL), slab(w_up), slab(w_down)],
        scratch_shapes=[pltpu.VMEM((D_MODEL, D_MODEL), BF16)] * 3,
        compiler_params=_cparams(1),
        name="mix_cross",
    )(out_a, out_b, x2d, w_out, g_cross.reshape(1, D_MODEL), w_cq, kc, vc, w_co, w_up, w_down)


def _mlp_kernel(final_norm, x_ref, g_ref, wu_ref, wd_ref, gf_ref, o_ref):
    x = x_ref[...]
    hb = _rms(x, g_ref[...]).astype(BF16)
    acc = x
    d_ff = wu_ref.shape[1]
    for c in range(d_ff // FF_CHUNK):
        cols = slice(c * FF_CHUNK, (c + 1) * FF_CHUNK)
        u = jnp.dot(hb, wu_ref[:, cols], preferred_element_type=F32)
        r = jnp.maximum(u, 0.0)
        acc = acc + jnp.dot((r * r).astype(BF16), wd_ref[cols, :], preferred_element_type=F32)
    o_ref[...] = _rms(acc, gf_ref[...]) if final_norm else acc


def _mlp(x2, g_mlp, w_up, w_down, g_final, final_norm):
    n_tok = x2.shape[0]
    ts = TOK_TILE
    tok = pl.BlockSpec((ts, D_MODEL), lambda i: (i, 0))
    return pl.pallas_call(
        functools.partial(_mlp_kernel, final_norm),
        out_shape=jax.ShapeDtypeStruct(x2.shape, F32),
        grid=(n_tok // ts,),
        in_specs=[tok, _const_spec((1, D_MODEL)), _const_spec(w_up.shape),
                  _const_spec(w_down.shape), _const_spec((1, D_MODEL))],
        out_specs=tok,
        compiler_params=_cparams(1),
        name="mlp",
    )(x2, g_mlp.reshape(1, D_MODEL), w_up, w_down, g_final.reshape(1, D_MODEL))


def kernel(x, mem, positions, g_mix, w_in, sinks, lambda_q1, lambda_k1, lambda_q2, lambda_k2,
           g_diff, w_out, g_cross, g_mem, w_cq, w_ckv, w_co, g_mlp, w_up, w_down, g_final):
    bsz, seq, _ = x.shape
    mem_len = mem.shape[1]
    depth = w_in.shape[0]
    xf = x.reshape(bsz * seq, D_MODEL)
    memf = mem.reshape(bsz * mem_len, D_MODEL)
    for l in range(depth):
        lam_init = 0.8 - 0.6 * math.exp(-0.3 * l)
        qat, kaw, vat, qdt, kd, vdt = _in_proj(xf, g_mix[l], w_in[l], positions)
        out_a = _swa(qat, kaw, vat, sinks[l], bsz, seq)
        lam_params = jnp.stack([lambda_q1[l], lambda_k1[l], lambda_q2[l], lambda_k2[l]])
        out_b = _diff_attn(qdt, kd, vdt, lam_params, g_diff[l], lam_init, bsz, seq)
        kc, vc = _mem_kv(memf, g_mem[l], w_ckv[l], bsz, mem_len)
        x2, w_up_bf, w_down_bf = _mix_cross(out_a, out_b, xf, w_out[l], g_cross[l], w_cq[l],
                                            kc, vc, w_co[l], w_up[l], w_down[l], seq, mem_len)
        xf = _mlp(x2, g_mlp[l], w_up_bf, w_down_bf, g_final, l == depth - 1)
    return xf.reshape(bsz, seq, D_MODEL)
```

```python
import functools
import math

import jax
import jax.numpy as jnp
from jax import lax
from jax.experimental import pallas as pl
from jax.experimental.pallas import tpu as pltpu

D_MODEL = 1024
HEAD_DIM = 64
SWA_Q_HEADS = 8
SWA_KV_HEADS = 2
SWA_GROUP = SWA_Q_HEADS // SWA_KV_HEADS
WINDOW = 128
DIFF_HEADS = 4
DIFF_VDIM = 2 * HEAD_DIM
DIFF_VPAD = DIFF_VDIM + 16
SWA_Q_COLS = SWA_Q_HEADS * HEAD_DIM
SWA_KV_COLS = SWA_KV_HEADS * HEAD_DIM
DIFF_QK_COLS = DIFF_HEADS * 2 * HEAD_DIM
DIFF_V_COLS = DIFF_HEADS * DIFF_VDIM
CROSS_HEADS = 4
CROSS_HEAD_DIM = D_MODEL // CROSS_HEADS
ROPE_THETA = 10000.0
NORM_EPS = 1e-5

LANES = 128
TOK_TILE = 1024
SWA_TILE = 1024
DIFF_TILE = 512
DIFF_GROUP = 4
FF_CHUNK = 512
SWA_AHEAD = 6
VMEM_LIMIT = 56 * 1024 * 1024
NEG_BIG = -1e30
LOG2E = math.log2(math.e)

BF16 = jnp.bfloat16
F32 = jnp.float32


def _cparams(n_axes):
    return pltpu.CompilerParams(dimension_semantics=("arbitrary",) * n_axes,
                                vmem_limit_bytes=VMEM_LIMIT)


def _rms(x, g):
    ms = jnp.mean(x * x, axis=-1, keepdims=True)
    return x * lax.rsqrt(ms + NORM_EPS) * g


def _const_spec(shape):
    return pl.BlockSpec(shape, lambda *_: (0,) * len(shape), pipeline_mode=pl.Buffered(1))


ROPE_HALF = HEAD_DIM // 2
ROPE_GROUPS = LANES // ROPE_HALF


def _rope_tables(pos, invf, sign):
    pos = pos.astype(F32)
    lane_group = lax.broadcasted_iota(jnp.int32, (pos.shape[0], LANES), 1) // ROPE_HALF
    pos_l = pos[:, ROPE_GROUPS - 1:ROPE_GROUPS]
    for g in range(ROPE_GROUPS - 2, -1, -1):
        pos_l = jnp.where(lane_group == g, pos[:, g:g + 1], pos_l)
    ang = pos_l * invf
    tables = []
    transposed = []
    for tab in (jnp.cos(ang), jnp.sin(ang)):
        rolled = [tab] + [pltpu.roll(tab, k * ROPE_HALF, 1) for k in range(1, ROPE_GROUPS)]
        quarters = []
        for g in range(ROPE_GROUPS):
            shift = (lane_group - g) % ROPE_GROUPS
            tiled = rolled[ROPE_GROUPS - 1]
            for k in range(ROPE_GROUPS - 2, -1, -1):
                tiled = jnp.where(shift == k, rolled[k], tiled)
            quarters.append(tiled)
        tables.append(jnp.concatenate(quarters, axis=0))
        tab_t = tab.T
        transposed.append(jnp.concatenate(
            [tab_t[g * ROPE_HALF:(g + 1) * ROPE_HALF, :] for g in range(ROPE_GROUPS)], axis=1))
    return tables[0], tables[1] * sign, transposed[0], transposed[1]


def _rope_inputs(positions, tile):
    n_tok = positions.size
    inv_freq = ROPE_THETA ** (-jnp.arange(0, HEAD_DIM, 2, dtype=F32) / HEAD_DIM)
    quarter = tile // ROPE_GROUPS
    pos_cols = positions.reshape(n_tok // tile, ROPE_GROUPS, quarter)
    pos_cols = pos_cols.transpose(0, 2, 1).reshape(n_tok // ROPE_GROUPS, ROPE_GROUPS)
    invf = jnp.tile(inv_freq, ROPE_GROUPS).reshape(1, LANES)
    sign = jnp.tile(jnp.repeat(jnp.array([-1.0, 1.0], F32), ROPE_HALF), 2).reshape(1, LANES)
    return pos_cols, invf, sign


def _rope(p, cos, sin):
    width = p.shape[-1]
    lane = lax.broadcasted_iota(jnp.int32, p.shape, 1)
    first_half = (lane % HEAD_DIM) < (HEAD_DIM // 2)
    swapped = jnp.where(first_half,
                        pltpu.roll(p, width - HEAD_DIM // 2, 1),
                        pltpu.roll(p, HEAD_DIM // 2, 1))
    return p * cos + swapped * sin


IN_QA = (0, SWA_Q_COLS)
IN_KA = (IN_QA[1], IN_QA[1] + SWA_KV_COLS)
IN_VA = (IN_KA[1], IN_KA[1] + SWA_KV_COLS)
IN_QD = (IN_VA[1], IN_VA[1] + DIFF_QK_COLS)
IN_KD = (IN_QD[1], IN_QD[1] + DIFF_QK_COLS)
IN_VD = (IN_KD[1], IN_KD[1] + DIFF_V_COLS)
IN_TRANSPOSED = (IN_QA, IN_QD, IN_VA, IN_VD)


def _in_proj_kernel(x_ref, g_ref, win_ref, pos_ref, invf_ref, sign_ref,
                    qat_ref, kaw_ref, vat_ref, qdt_ref, kd_ref, vdt_ref, wk_ref, wt_ref):
    @pl.when(pl.program_id(0) == 0)
    def _prepare_weights():
        wk_ref[:, :SWA_KV_COLS] = win_ref[:, IN_KA[0]:IN_KA[1]].astype(BF16)
        wk_ref[:, SWA_KV_COLS:] = win_ref[:, IN_KD[0]:IN_KD[1]].astype(BF16)
        row = 0
        for c0, c1 in IN_TRANSPOSED:
            wt_ref[row:row + c1 - c0, :] = win_ref[:, c0:c1].T.astype(BF16)
            row += c1 - c0

    hb = _rms(x_ref[...], g_ref[...]).astype(BF16)
    n_tok = hb.shape[0]
    nt_dims = (((1,), (1,)), ((), ()))

    t_rows = [0]
    for c0, c1 in IN_TRANSPOSED:
        t_rows.append(t_rows[-1] + c1 - c0)

    def proj_t(first, last):
        return lax.dot_general(wt_ref[t_rows[first]:t_rows[last], :], hb, nt_dims,
                               preferred_element_type=F32)

    vt = proj_t(2, 4)
    ka = jnp.dot(hb, wk_ref[:, :SWA_KV_COLS], preferred_element_type=F32)
    kd_parts = [jnp.dot(hb, wk_ref[:, SWA_KV_COLS + c * 256:SWA_KV_COLS + (c + 1) * 256],
                        preferred_element_type=F32) for c in range(DIFF_QK_COLS // 256)]
    qat = proj_t(0, 1)
    qdt = proj_t(1, 2)

    cos1, sin1, cos_t, sin_t = _rope_tables(pos_ref[...], invf_ref[...], sign_ref[...])
    cos2 = jnp.concatenate([cos1, cos1], axis=-1)
    sin2 = jnp.concatenate([sin1, sin1], axis=-1)

    vat_ref[...] = vt[:SWA_KV_COLS].astype(BF16)
    pad_row = lax.broadcasted_iota(jnp.int32, (DIFF_VPAD - DIFF_VDIM, n_tok), 0)
    ones_rows = jnp.where(pad_row == 0, 1.0, 0.0).astype(BF16)
    for h in range(DIFF_HEADS):
        src = SWA_KV_COLS + h * DIFF_VDIM
        vdt_ref[h * DIFF_VPAD:h * DIFF_VPAD + DIFF_VDIM, :] = vt[src:src + DIFF_VDIM].astype(BF16)
        vdt_ref[h * DIFF_VPAD + DIFF_VDIM:(h + 1) * DIFF_VPAD, :] = ones_rows

    ka = _rope(ka, cos1, sin1)
    ka_swapped = pltpu.roll(ka, HEAD_DIM, 1)
    low_half = lax.broadcasted_iota(jnp.int32, ka.shape, 1) < HEAD_DIM
    kaw_ref[:, :LANES] = jnp.where(low_half, ka, ka_swapped).astype(BF16)
    kaw_ref[:, LANES:] = jnp.where(low_half, ka_swapped, ka).astype(BF16)
    for c, p in enumerate(kd_parts):
        kd_ref[:, c * 256:(c + 1) * 256] = _rope(p, cos2, sin2).astype(BF16)

    q_scale = HEAD_DIM ** -0.5 * LOG2E
    cos_q = cos_t * q_scale
    sin_q = sin_t * q_scale
    zero_rows = jnp.zeros((HEAD_DIM, n_tok), BF16)

    def store_head(pt, src, out_ref, dst, zero_dst):
        x1 = pt[src:src + ROPE_HALF]
        x2 = pt[src + ROPE_HALF:src + HEAD_DIM]
        out_ref[dst:dst + ROPE_HALF, :] = (x1 * cos_q - x2 * sin_q).astype(BF16)
        out_ref[dst + ROPE_HALF:dst + HEAD_DIM, :] = (x2 * cos_q + x1 * sin_q).astype(BF16)
        out_ref[zero_dst:zero_dst + HEAD_DIM, :] = zero_rows

    for h in range(SWA_Q_HEADS):
        store_head(qat, h * HEAD_DIM, qat_ref, h * LANES, h * LANES + HEAD_DIM)
    for hc in range(2 * DIFF_HEADS):
        c = hc % 2
        store_head(qdt, hc * HEAD_DIM, qdt_ref, hc * LANES + c * HEAD_DIM,
                   hc * LANES + (1 - c) * HEAD_DIM)


def _in_proj(x2d, g_mix, w_in, positions):
    n_tok = x2d.shape[0]
    ts = TOK_TILE
    pos_cols, invf, sign = _rope_inputs(positions, ts)
    n_key_cols = SWA_KV_COLS + DIFF_QK_COLS
    n_t_rows = sum(c1 - c0 for c0, c1 in IN_TRANSPOSED)
    tok = lambda width: pl.BlockSpec((ts, width), lambda i: (i, 0))
    tok_t = lambda rows: pl.BlockSpec((rows, ts), lambda i: (0, i))
    qa_rows = SWA_Q_HEADS * LANES
    kaw_cols = SWA_KV_HEADS * LANES
    qd_rows = DIFF_HEADS * 2 * LANES
    out_shape = [
        jax.ShapeDtypeStruct((qa_rows, n_tok), BF16),
        jax.ShapeDtypeStruct((n_tok, kaw_cols), BF16),
        jax.ShapeDtypeStruct((SWA_KV_COLS, n_tok), BF16),
        jax.ShapeDtypeStruct((qd_rows, n_tok), BF16),
        jax.ShapeDtypeStruct((n_tok, DIFF_QK_COLS), BF16),
        jax.ShapeDtypeStruct((DIFF_HEADS * DIFF_VPAD, n_tok), BF16),
    ]
    return pl.pallas_call(
        _in_proj_kernel,
        out_shape=out_shape,
        grid=(n_tok // ts,),
        in_specs=[tok(D_MODEL), _const_spec((1, D_MODEL)), _const_spec(w_in.shape),
                  pl.BlockSpec((ts // ROPE_GROUPS, ROPE_GROUPS), lambda i: (i, 0)),
                  _const_spec((1, LANES)), _const_spec((1, LANES))],
        out_specs=[tok_t(qa_rows), tok(kaw_cols), tok_t(SWA_KV_COLS), tok_t(qd_rows),
                   tok(DIFF_QK_COLS), tok_t(DIFF_HEADS * DIFF_VPAD)],
        scratch_shapes=[pltpu.VMEM((D_MODEL, n_key_cols), BF16),
                        pltpu.VMEM((n_t_rows, D_MODEL), BF16)],
        compiler_params=_cparams(1),
        name="in_proj",
    )(x2d, g_mix.reshape(1, D_MODEL), w_in, pos_cols, invf, sign)


def _swa_kernel(qt_ref, kc_ref, kp_ref, vc_ref, vp_ref, sink_ref, tri_ref, o_ref):
    n_sub = qt_ref.shape[1] // WINDOW
    kj = lax.broadcasted_iota(jnp.int32, (WINDOW, 2 * WINDOW), 0)
    qi = lax.broadcasted_iota(jnp.int32, (WINDOW, 2 * WINDOW), 1) % WINDOW
    from_prev = kj > qi
    no_prev_bias = jnp.where(pl.program_id(1) == 0, NEG_BIG, 0.0)
    sinks = sink_ref[...] * LOG2E
    pairs = SWA_GROUP // 2
    units = [(r, kv, pair) for r in range(n_sub) for kv in range(SWA_KV_HEADS)
             for pair in range(pairs)]

    def key_rows(r):
        if r == 0:
            return jnp.concatenate([kp_ref[...], kc_ref[0:WINDOW, :]], axis=0)
        return kc_ref[(r - 1) * WINDOW:(r + 1) * WINDOW, :]

    def value_cols(r):
        if r == 0:
            return jnp.concatenate([vp_ref[...], vc_ref[:, 0:WINDOW]], axis=1)
        return vc_ref[:, (r - 1) * WINDOW:(r + 1) * WINDOW]

    def scores(unit):
        r, kv, pair = unit
        head = kv * SWA_GROUP + 2 * pair
        q2 = jnp.concatenate(
            [qt_ref[(head + g) * LANES:(head + g + 1) * LANES, r * WINDOW:(r + 1) * WINDOW]
             for g in range(2)], axis=1)
        kw = key_rows(r)[:, kv * LANES:(kv + 1) * LANES]
        s = jnp.dot(kw, q2, preferred_element_type=F32)
        s_prev = s[:WINDOW] + no_prev_bias if r == 0 else s[:WINDOW]
        return jnp.where(from_prev, s_prev, s[WINDOW:])

    def attend(unit, s):
        r, kv, pair = unit
        tile = kv * pairs + pair
        sink = sinks[tile:tile + 1, :]
        m = jnp.maximum(jnp.max(s, axis=0, keepdims=True), sink)
        e = jnp.exp2(s - m)
        denom = jnp.sum(e, axis=0, keepdims=True) + jnp.exp2(sink - m)
        eb = e.astype(BF16)
        e2 = jnp.concatenate([eb, eb], axis=0) * tri_ref[...]
        vt = value_cols(r)[kv * HEAD_DIM:(kv + 1) * HEAD_DIM, :]
        o = jnp.dot(vt, e2, preferred_element_type=F32) / denom
        return [o[:, :WINDOW], o[:, WINDOW:]]

    pieces = []
    pending = [scores(u) for u in units[:SWA_AHEAD]]
    for idx, unit in enumerate(units):
        if idx + SWA_AHEAD < len(units):
            pending.append(scores(units[idx + SWA_AHEAD]))
        pieces += attend(unit, pending.pop(0))
        if len(pieces) == SWA_Q_HEADS:
            r = unit[0]
            o_ref[r * WINDOW:(r + 1) * WINDOW, :] = (
                jnp.concatenate(pieces, axis=0).T.astype(BF16))
            pieces = []


def _swa(qat, kaw, vat, sinks, bsz, seq):
    ts = SWA_TILE
    per_b = seq // ts
    sub = ts // WINDOW
    prev_blk = lambda b, t: jnp.maximum((b * per_b + t) * sub - 1, 0)
    cur = lambda width: pl.BlockSpec((ts, width), lambda b, t: (b * per_b + t, 0))
    sink_rows = jnp.repeat(sinks.reshape(SWA_Q_HEADS // 2, 2), WINDOW, axis=1)
    kj = lax.broadcasted_iota(jnp.int32, (WINDOW, 2 * WINDOW), 0)
    qi = lax.broadcasted_iota(jnp.int32, (WINDOW, 2 * WINDOW), 1) % WINDOW
    triangles = jnp.concatenate([kj > qi, kj <= qi], axis=0).astype(BF16)
    return pl.pallas_call(
        _swa_kernel,
        out_shape=jax.ShapeDtypeStruct((kaw.shape[0], SWA_Q_COLS), BF16),
        grid=(bsz, per_b),
        in_specs=[pl.BlockSpec((qat.shape[0], ts), lambda b, t: (0, b * per_b + t)),
                  cur(kaw.shape[1]),
                  pl.BlockSpec((WINDOW, kaw.shape[1]), lambda b, t: (prev_blk(b, t), 0)),
                  pl.BlockSpec((SWA_KV_COLS, ts), lambda b, t: (0, b * per_b + t)),
                  pl.BlockSpec((SWA_KV_COLS, WINDOW), lambda b, t: (0, prev_blk(b, t))),
                  _const_spec(sink_rows.shape), _const_spec(triangles.shape)],
        out_specs=cur(SWA_Q_COLS),
        compiler_params=_cparams(2),
        name="swa",
    )(qat, kaw, kaw, vat, vat, sink_rows, triangles)


def _diff_kernel(lam_init, q_ref, k_ref, vt_ref, bias_ref, lamp_ref, g_ref, o_ref,
                 s0_ref, s1_ref, acc_ref, max_ref):
    th = DIFF_TILE
    top = DIFF_GROUP - 1
    n_groups = k_ref.shape[0] // (DIFF_GROUP * th)
    s_refs = (s0_ref, s1_ref)

    def q_cols(group_idx, t):
        return pl.ds(pl.multiple_of((group_idx * DIFF_GROUP + t) * th, th), th)

    def k_tile(j):
        return k_ref[pl.ds(pl.multiple_of(j * th, th), th), :]

    def vt_tile(j):
        return vt_ref[:, pl.ds(pl.multiple_of(j * th, th), th)]

    hh = th // 2

    def start(group_idx, unit, slot):
        j, t, c, masked = unit
        s_ref = s_refs[slot]
        qt = q_ref[c * LANES:(c + 1) * LANES, q_cols(group_idx, t)]
        k = k_tile(j)
        if not masked:
            s = jnp.dot(k, qt, preferred_element_type=F32)
            s_ref[...] = s
            return jnp.max(s, axis=0, keepdims=True)
        s_early = jnp.dot(k[:hh], qt, preferred_element_type=F32) + bias_ref[:hh, :]
        s_late = jnp.dot(k[hh:], qt[:, hh:], preferred_element_type=F32) + bias_ref[hh:, hh:]
        s_ref[:hh, :] = s_early
        s_ref[hh:, hh:] = s_late
        max_early = jnp.max(s_early, axis=0, keepdims=True)
        max_late = jnp.max(s_late, axis=0, keepdims=True)
        return jnp.concatenate(
            [max_early[:, :hh], jnp.maximum(max_early[:, hh:], max_late)], axis=1)

    def consume(unit, slot, tile_max):
        j, t, c, masked = unit
        s_ref = s_refs[slot]
        idx = 2 * t + c
        m = max_ref[idx:idx + 1, :]
        m_new = jnp.maximum(m, tile_max)
        alpha = jnp.exp2(m - m_new)
        max_ref[idx:idx + 1, :] = m_new
        vt = vt_tile(j)
        if not masked:
            p = jnp.exp2(s_ref[...] - m_new)
            pv = jnp.dot(vt, p.astype(BF16), preferred_element_type=F32)
            acc_ref[idx] = alpha * acc_ref[idx] + pv
            return
        p_early = jnp.exp2(s_ref[:hh, :] - m_new).astype(BF16)
        p_late = jnp.exp2(s_ref[hh:, hh:] - m_new[:, hh:]).astype(BF16)
        pv_lo = jnp.dot(vt[:, :hh], p_early[:, :hh], preferred_element_type=F32)
        pv_hi = (jnp.dot(vt[:, :hh], p_early[:, hh:], preferred_element_type=F32)
                 + jnp.dot(vt[:, hh:], p_late, preferred_element_type=F32))
        acc_ref[idx, :, :hh] = alpha[:, :hh] * acc_ref[idx, :, :hh] + pv_lo
        acc_ref[idx, :, hh:] = alpha[:, hh:] * acc_ref[idx, :, hh:] + pv_hi

    def finish(group_idx, t):
        lp = lamp_ref[...]
        lam = (jnp.exp(jnp.sum(lp[0:1] * lp[1:2], axis=-1, keepdims=True))
               - jnp.exp(jnp.sum(lp[2:3] * lp[3:4], axis=-1, keepdims=True)) + lam_init)
        l0 = acc_ref[2 * t, DIFF_VDIM:DIFF_VDIM + 1, :]
        l1 = acc_ref[2 * t + 1, DIFF_VDIM:DIFF_VDIM + 1, :]
        o = (acc_ref[2 * t, :DIFF_VDIM, :] * (1.0 / l0)
             - acc_ref[2 * t + 1, :DIFF_VDIM, :] * (lam / l1))
        ms = jnp.mean(o * o, axis=0, keepdims=True)
        y = o * lax.rsqrt(ms + NORM_EPS) * g_ref[...] * (1.0 - lam_init)
        o_ref[q_cols(group_idx, t), :] = y.T.astype(BF16)

    def run(group_idx, units, following, tile_max):
        for n, unit in enumerate(units):
            nxt = units[n + 1] if n + 1 < len(units) else following
            next_max = start(group_idx, nxt, (n + 1) % 2) if nxt is not None else None
            consume(unit, n % 2, tile_max)
            tile_max = next_max
            _, t, c, masked = unit
            if masked and c == 1:
                finish(group_idx, t)
        return tile_max

    def key_tile_units(j, lowest, lowest_masked):
        return [(j, t, c, lowest_masked and t == lowest)
                for t in range(top, lowest - 1, -1) for c in range(2)]

    def sweep_group(group_idx, _):
        tile_max = start(group_idx, (0, top, 0, False), 0)
        acc_ref[...] = jnp.zeros_like(acc_ref)
        max_ref[...] = jnp.full(max_ref.shape, NEG_BIG, F32)

        def earlier_group(g, mx):
            units = [u for d in range(DIFF_GROUP)
                     for u in key_tile_units(g * DIFF_GROUP + d, 0, False)]
            return run(group_idx, units, ((g + 1) * DIFF_GROUP, top, 0, False), mx)

        if n_groups > 1:
            tile_max = lax.fori_loop(0, group_idx, earlier_group, tile_max)
        j0 = DIFF_GROUP * group_idx
        tail = [u for d in range(DIFF_GROUP) for u in key_tile_units(j0 + d, d, True)]
        run(group_idx, tail, None, tile_max)
        return 0

    lax.fori_loop(0, n_groups, sweep_group, 0)


def _diff_attn(qdt, kd, vdt, lam_params, g_diff, lam_init, bsz, seq):
    th = DIFF_TILE
    key = lax.broadcasted_iota(jnp.int32, (th, th), 0)
    qry = lax.broadcasted_iota(jnp.int32, (th, th), 1)
    causal_bias = jnp.where(key <= qry, 0.0, NEG_BIG).astype(F32)
    return pl.pallas_call(
        functools.partial(_diff_kernel, lam_init),
        out_shape=jax.ShapeDtypeStruct(kd.shape, BF16),
        grid=(bsz, DIFF_HEADS),
        in_specs=[
            pl.BlockSpec((2 * LANES, seq), lambda b, h: (h, b)),
            pl.BlockSpec((seq, DIFF_VDIM), lambda b, h: (b, h)),
            pl.BlockSpec((DIFF_VPAD, seq), lambda b, h: (h, b)),
            _const_spec((th, th)),
            _const_spec((4, HEAD_DIM)),
            _const_spec((DIFF_VDIM, 1)),
        ],
        out_specs=pl.BlockSpec((seq, DIFF_VDIM), lambda b, h: (b, h)),
        scratch_shapes=[pltpu.VMEM((th, th), F32), pltpu.VMEM((th, th), F32),
                        pltpu.VMEM((2 * DIFF_GROUP, DIFF_VPAD, th), F32),
                        pltpu.VMEM((2 * DIFF_GROUP, th), F32)],
        compiler_params=_cparams(2),
        name="diff_attn",
    )(qdt, kd, vdt, causal_bias, lam_params, g_diff.reshape(DIFF_VDIM, 1))


def _mem_kv_kernel(m_ref, g_ref, wkv_ref, k_ref, v_ref, w_ref):
    @pl.when(pl.program_id(0) == 0)
    def _prepare_weights():
        w_ref[...] = wkv_ref[...].astype(BF16)

    hm = _rms(m_ref[...], g_ref[...]).astype(BF16)
    k_ref[...] = jnp.dot(hm, w_ref[:, :D_MODEL], preferred_element_type=F32).astype(BF16)
    v_ref[...] = jnp.dot(hm, w_ref[:, D_MODEL:], preferred_element_type=F32).astype(BF16)


def _mem_kv(mem2d, g_mem, w_ckv, bsz, mem_len):
    blk = pl.BlockSpec((mem_len, D_MODEL), lambda b: (b, 0))
    return pl.pallas_call(
        _mem_kv_kernel,
        out_shape=[jax.ShapeDtypeStruct(mem2d.shape, BF16)] * 2,
        grid=(bsz,),
        in_specs=[blk, _const_spec((1, D_MODEL)), _const_spec((D_MODEL, 2 * D_MODEL))],
        out_specs=[blk, blk],
        scratch_shapes=[pltpu.VMEM(w_ckv.shape, BF16)],
        compiler_params=_cparams(1),
        name="mem_kv",
    )(mem2d, g_mem.reshape(1, D_MODEL), w_ckv)


def _mix_cross_kernel(a_ref, b_ref, x_ref, wout_ref, g_ref, wcq_ref, k_ref, v_ref, wco_ref,
                      wup_ref, wdown_ref, o_ref, wup_bf_ref, wdown_bf_ref,
                      w_ref, wq_ref, wo_ref):
    @pl.when(pl.program_id(0) == 0)
    def _prepare_weights():
        w_ref[...] = wout_ref[...].astype(BF16)
        wq_ref[...] = wcq_ref[...].astype(BF16)
        wo_ref[...] = wco_ref[...].astype(BF16)

    wup_bf_ref[...] = wup_ref[...].astype(BF16)
    wdown_bf_ref[...] = wdown_ref[...].astype(BF16)

    half = a_ref.shape[1]
    y = (jnp.dot(a_ref[...], w_ref[:half, :], preferred_element_type=F32)
         + jnp.dot(b_ref[...], w_ref[half:, :], preferred_element_type=F32))
    x1 = x_ref[...] + y
    hc = _rms(x1, g_ref[...]).astype(BF16)
    qc = jnp.dot(hc, wq_ref[...], preferred_element_type=F32) * (CROSS_HEAD_DIM ** -0.5 * LOG2E)
    qc = qc.astype(BF16)

    def head_cols(h):
        return slice(h * CROSS_HEAD_DIM, (h + 1) * CROSS_HEAD_DIM)

    def scores(h):
        return lax.dot_general(qc[:, head_cols(h)], k_ref[:, head_cols(h)],
                               (((1,), (1,)), ((), ())), preferred_element_type=F32)

    def attend(h, s):
        m = jnp.max(s, axis=-1, keepdims=True)
        e = jnp.exp2(s - m)
        inv = 1.0 / jnp.sum(e, axis=-1, keepdims=True)
        pv = jnp.dot(e.astype(BF16), v_ref[:, head_cols(h)], preferred_element_type=F32)
        return (pv * inv).astype(BF16)

    heads = []
    s_next = scores(0)
    for h in range(CROSS_HEADS):
        s_cur = s_next
        if h + 1 < CROSS_HEADS:
            s_next = scores(h + 1)
        heads.append(attend(h, s_cur))
    o = jnp.concatenate(heads, axis=-1)
    o_ref[...] = x1 + jnp.dot(o, wo_ref[...], preferred_element_type=F32)


def _mix_cross(out_a, out_b, x2d, w_out, g_cross, w_cq, kc, vc, w_co, w_up, w_down, seq,
               mem_len):
    n_tok = x2d.shape[0]
    ts = TOK_TILE
    n_steps = n_tok // ts
    per_b = seq // ts
    half = SWA_Q_COLS
    tok = lambda width: pl.BlockSpec((ts, width), lambda i: (i, 0))
    memblk = pl.BlockSpec((mem_len, D_MODEL), lambda i: (i // per_b, 0))
    square = _const_spec((D_MODEL, D_MODEL))
    slab = lambda w: pl.BlockSpec((w.shape[0] // n_steps, w.shape[1]), lambda i: (i, 0))
    return pl.pallas_call(
        _mix_cross_kernel,
        out_shape=[jax.ShapeDtypeStruct(x2d.shape, F32),
                   jax.ShapeDtypeStruct(w_up.shape, BF16),
                   jax.ShapeDtypeStruct(w_down.shape, BF16)],
        grid=(n_steps,),
        in_specs=[tok(half), tok(half), tok(D_MODEL), square, _const_spec((1, D_MODEL)), square,
                  memblk, memblk, square, slab(w_up), slab(w_down)],
        out_specs=[tok(D_MODEL), slab(w_up), slab(w_down)],
        scratch_shapes=[pltpu.VMEM((D_MODEL, D_MODEL), BF16)] * 3,
        compiler_params=_cparams(1),
        name="mix_cross",
    )(out_a, out_b, x2d, w_out, g_cross.reshape(1, D_MODEL), w_cq, kc, vc, w_co, w_up, w_down)


def _mlp_kernel(final_norm, x_ref, g_ref, wu_ref, wd_ref, gf_ref, o_ref):
    x = x_ref[...]
    hb = _rms(x, g_ref[...]).astype(BF16)
    acc = x
    d_ff = wu_ref.shape[1]
    for c in range(d_ff // FF_CHUNK):
        cols = slice(c * FF_CHUNK, (c + 1) * FF_CHUNK)
        u = jnp.dot(hb, wu_ref[:, cols], preferred_element_type=F32)
        r = jnp.maximum(u, 0.0)
        acc = acc + jnp.dot((r * r).astype(BF16), wd_ref[cols, :], preferred_element_type=F32)
    o_ref[...] = _rms(acc, gf_ref[...]) if final_norm else acc


def _mlp(x2, g_mlp, w_up, w_down, g_final, final_norm):
    n_tok = x2.shape[0]
    ts = TOK_TILE
    tok = pl.BlockSpec((ts, D_MODEL), lambda i: (i, 0))
    return pl.pallas_call(
        functools.partial(_mlp_kernel, final_norm),
        out_shape=jax.ShapeDtypeStruct(x2.shape, F32),
        grid=(n_tok // ts,),
        in_specs=[tok, _const_spec((1, D_MODEL)), _const_spec(w_up.shape),
                  _const_spec(w_down.shape), _const_spec((1, D_MODEL))],
        out_specs=tok,
        compiler_params=_cparams(1),
        name="mlp",
    )(x2, g_mlp.reshape(1, D_MODEL), w_up, w_down, g_final.reshape(1, D_MODEL))


def kernel(x, mem, positions, g_mix, w_in, sinks, lambda_q1, lambda_k1, lambda_q2, lambda_k2,
           g_diff, w_out, g_cross, g_mem, w_cq, w_ckv, w_co, g_mlp, w_up, w_down, g_final):
    bsz, seq, _ = x.shape
    mem_len = mem.shape[1]
    depth = w_in.shape[0]
    xf = x.reshape(bsz * seq, D_MODEL)
    memf = mem.reshape(bsz * mem_len, D_MODEL)
    for l in range(depth):
        lam_init = 0.8 - 0.6 * math.exp(-0.3 * l)
        qat, kaw, vat, qdt, kd, vdt = _in_proj(xf, g_mix[l], w_in[l], positions)
        out_a = _swa(qat, kaw, vat, sinks[l], bsz, seq)
        lam_params = jnp.stack([lambda_q1[l], lambda_k1[l], lambda_q2[l], lambda_k2[l]])
        out_b = _diff_attn(qdt, kd, vdt, lam_params, g_diff[l], lam_init, bsz, seq)
        kc, vc = _mem_kv(memf, g_mem[l], w_ckv[l], bsz, mem_len)
        x2, w_up_bf, w_down_bf = _mix_cross(out_a, out_b, xf, w_out[l], g_cross[l], w_cq[l],
                                            kc, vc, w_co[l], w_up[l], w_down[l], seq, mem_len)
        xf = _mlp(x2, g_mlp[l], w_up_bf, w_down_bf, g_final, l == depth - 1)
    return xf.reshape(bsz, seq, D_MODEL)
```

```python
import functools
import math

import jax
import jax.numpy as jnp
from jax import lax
from jax.experimental import pallas as pl
from jax.experimental.pallas import tpu as pltpu

D_MODEL = 1024
HEAD_DIM = 64
SWA_Q_HEADS = 8
SWA_KV_HEADS = 2
SWA_GROUP = SWA_Q_HEADS // SWA_KV_HEADS
WINDOW = 128
DIFF_HEADS = 4
DIFF_VDIM = 2 * HEAD_DIM
DIFF_VPAD = DIFF_VDIM + 16
SWA_Q_COLS = SWA_Q_HEADS * HEAD_DIM
SWA_KV_COLS = SWA_KV_HEADS * HEAD_DIM
DIFF_QK_COLS = DIFF_HEADS * 2 * HEAD_DIM
DIFF_V_COLS = DIFF_HEADS * DIFF_VDIM
CROSS_HEADS = 4
CROSS_HEAD_DIM = D_MODEL // CROSS_HEADS
ROPE_THETA = 10000.0
NORM_EPS = 1e-5

LANES = 128
TOK_TILE = 1024
SWA_TILE = 1024
DIFF_TILE = 512
DIFF_GROUP = 4
DIFF_TAIL_AHEAD = 2
FF_CHUNK = 512
MLP_SUBTILES = 2
SWA_AHEAD = 6
VMEM_LIMIT = 56 * 1024 * 1024
NEG_BIG = -1e30
LOG2E = math.log2(math.e)

BF16 = jnp.bfloat16
F32 = jnp.float32


def _cparams(n_axes):
    return pltpu.CompilerParams(dimension_semantics=("arbitrary",) * n_axes,
                                vmem_limit_bytes=VMEM_LIMIT)


def _rms(x, g):
    ms = jnp.mean(x * x, axis=-1, keepdims=True)
    return x * lax.rsqrt(ms + NORM_EPS) * g


def _const_spec(shape):
    return pl.BlockSpec(shape, lambda *_: (0,) * len(shape), pipeline_mode=pl.Buffered(1))


ROPE_HALF = HEAD_DIM // 2
ROPE_GROUPS = LANES // ROPE_HALF


def _rope_tables(pos, invf, sign):
    pos = pos.astype(F32)
    lane_group = lax.broadcasted_iota(jnp.int32, (pos.shape[0], LANES), 1) // ROPE_HALF
    pos_l = pos[:, ROPE_GROUPS - 1:ROPE_GROUPS]
    for g in range(ROPE_GROUPS - 2, -1, -1):
        pos_l = jnp.where(lane_group == g, pos[:, g:g + 1], pos_l)
    ang = pos_l * invf
    tables = []
    transposed = []
    for tab in (jnp.cos(ang), jnp.sin(ang)):
        rolled = [tab] + [pltpu.roll(tab, k * ROPE_HALF, 1) for k in range(1, ROPE_GROUPS)]
        quarters = []
        for g in range(ROPE_GROUPS):
            shift = (lane_group - g) % ROPE_GROUPS
            tiled = rolled[ROPE_GROUPS - 1]
            for k in range(ROPE_GROUPS - 2, -1, -1):
                tiled = jnp.where(shift == k, rolled[k], tiled)
            quarters.append(tiled)
        tables.append(jnp.concatenate(quarters, axis=0))
        tab_t = tab.T
        transposed.append(jnp.concatenate(
            [tab_t[g * ROPE_HALF:(g + 1) * ROPE_HALF, :] for g in range(ROPE_GROUPS)], axis=1))
    return tables[0], tables[1] * sign, transposed[0], transposed[1]


def _rope_inputs(positions, tile):
    n_tok = positions.size
    inv_freq = ROPE_THETA ** (-jnp.arange(0, HEAD_DIM, 2, dtype=F32) / HEAD_DIM)
    quarter = tile // ROPE_GROUPS
    pos_cols = positions.reshape(n_tok // tile, ROPE_GROUPS, quarter)
    pos_cols = pos_cols.transpose(0, 2, 1).reshape(n_tok // ROPE_GROUPS, ROPE_GROUPS)
    invf = jnp.tile(inv_freq, ROPE_GROUPS).reshape(1, LANES)
    sign = jnp.tile(jnp.repeat(jnp.array([-1.0, 1.0], F32), ROPE_HALF), 2).reshape(1, LANES)
    return pos_cols, invf, sign


def _rope(p, cos, sin):
    width = p.shape[-1]
    lane = lax.broadcasted_iota(jnp.int32, p.shape, 1)
    first_half = (lane % HEAD_DIM) < (HEAD_DIM // 2)
    swapped = jnp.where(first_half,
                        pltpu.roll(p, width - HEAD_DIM // 2, 1),
                        pltpu.roll(p, HEAD_DIM // 2, 1))
    return p * cos + swapped * sin


IN_QA = (0, SWA_Q_COLS)
IN_KA = (IN_QA[1], IN_QA[1] + SWA_KV_COLS)
IN_VA = (IN_KA[1], IN_KA[1] + SWA_KV_COLS)
IN_QD = (IN_VA[1], IN_VA[1] + DIFF_QK_COLS)
IN_KD = (IN_QD[1], IN_QD[1] + DIFF_QK_COLS)
IN_VD = (IN_KD[1], IN_KD[1] + DIFF_V_COLS)
IN_TRANSPOSED = (IN_QA, IN_QD, IN_VA, IN_VD)


def _in_proj_kernel(x_ref, g_ref, win_ref, pos_ref, invf_ref, sign_ref,
                    qat_ref, kaw_ref, vat_ref, qdt_ref, kd_ref, vdt_ref, wk_ref, wt_ref):
    @pl.when(pl.program_id(0) == 0)
    def _prepare_weights():
        wk_ref[:, :SWA_KV_COLS] = win_ref[:, IN_KA[0]:IN_KA[1]].astype(BF16)
        wk_ref[:, SWA_KV_COLS:] = win_ref[:, IN_KD[0]:IN_KD[1]].astype(BF16)
        row = 0
        for c0, c1 in IN_TRANSPOSED:
            wt_ref[row:row + c1 - c0, :] = win_ref[:, c0:c1].T.astype(BF16)
            row += c1 - c0

    hb = _rms(x_ref[...], g_ref[...]).astype(BF16)
    n_tok = hb.shape[0]
    nt_dims = (((1,), (1,)), ((), ()))

    t_rows = [0]
    for c0, c1 in IN_TRANSPOSED:
        t_rows.append(t_rows[-1] + c1 - c0)

    def proj_t(first, last):
        return lax.dot_general(wt_ref[t_rows[first]:t_rows[last], :], hb, nt_dims,
                               preferred_element_type=F32)

    vt = proj_t(2, 4)
    ka = jnp.dot(hb, wk_ref[:, :SWA_KV_COLS], preferred_element_type=F32)
    kd_parts = [jnp.dot(hb, wk_ref[:, SWA_KV_COLS + c * 256:SWA_KV_COLS + (c + 1) * 256],
                        preferred_element_type=F32) for c in range(DIFF_QK_COLS // 256)]
    qat = proj_t(0, 1)
    qdt = proj_t(1, 2)

    cos1, sin1, cos_t, sin_t = _rope_tables(pos_ref[...], invf_ref[...], sign_ref[...])
    cos2 = jnp.concatenate([cos1, cos1], axis=-1)
    sin2 = jnp.concatenate([sin1, sin1], axis=-1)

    vat_ref[...] = vt[:SWA_KV_COLS].astype(BF16)
    pad_row = lax.broadcasted_iota(jnp.int32, (DIFF_VPAD - DIFF_VDIM, n_tok), 0)
    ones_rows = jnp.where(pad_row == 0, 1.0, 0.0).astype(BF16)
    for h in range(DIFF_HEADS):
        src = SWA_KV_COLS + h * DIFF_VDIM
        vdt_ref[h * DIFF_VPAD:h * DIFF_VPAD + DIFF_VDIM, :] = vt[src:src + DIFF_VDIM].astype(BF16)
        vdt_ref[h * DIFF_VPAD + DIFF_VDIM:(h + 1) * DIFF_VPAD, :] = ones_rows

    ka = _rope(ka, cos1, sin1)
    ka_swapped = pltpu.roll(ka, HEAD_DIM, 1)
    low_half = lax.broadcasted_iota(jnp.int32, ka.shape, 1) < HEAD_DIM
    kaw_ref[:, :LANES] = jnp.where(low_half, ka, ka_swapped).astype(BF16)
    kaw_ref[:, LANES:] = jnp.where(low_half, ka_swapped, ka).astype(BF16)
    for c, p in enumerate(kd_parts):
        kd_ref[:, c * 256:(c + 1) * 256] = _rope(p, cos2, sin2).astype(BF16)

    q_scale = HEAD_DIM ** -0.5 * LOG2E
    cos_q = cos_t * q_scale
    sin_q = sin_t * q_scale
    zero_rows = jnp.zeros((HEAD_DIM, n_tok), BF16)

    def store_head(pt, src, out_ref, dst, zero_dst):
        x1 = pt[src:src + ROPE_HALF]
        x2 = pt[src + ROPE_HALF:src + HEAD_DIM]
        out_ref[dst:dst + ROPE_HALF, :] = (x1 * cos_q - x2 * sin_q).astype(BF16)
        out_ref[dst + ROPE_HALF:dst + HEAD_DIM, :] = (x2 * cos_q + x1 * sin_q).astype(BF16)
        out_ref[zero_dst:zero_dst + HEAD_DIM, :] = zero_rows

    for h in range(SWA_Q_HEADS):
        store_head(qat, h * HEAD_DIM, qat_ref, h * LANES, h * LANES + HEAD_DIM)
    for hc in range(2 * DIFF_HEADS):
        c = hc % 2
        store_head(qdt, hc * HEAD_DIM, qdt_ref, hc * LANES + c * HEAD_DIM,
                   hc * LANES + (1 - c) * HEAD_DIM)


def _in_proj(x2d, g_mix, w_in, positions):
    n_tok = x2d.shape[0]
    ts = TOK_TILE
    pos_cols, invf, sign = _rope_inputs(positions, ts)
    n_key_cols = SWA_KV_COLS + DIFF_QK_COLS
    n_t_rows = sum(c1 - c0 for c0, c1 in IN_TRANSPOSED)
    tok = lambda width: pl.BlockSpec((ts, width), lambda i: (i, 0))
    tok_t = lambda rows: pl.BlockSpec((rows, ts), lambda i: (0, i))
    qa_rows = SWA_Q_HEADS * LANES
    kaw_cols = SWA_KV_HEADS * LANES
    qd_rows = DIFF_HEADS * 2 * LANES
    out_shape = [
        jax.ShapeDtypeStruct((qa_rows, n_tok), BF16),
        jax.ShapeDtypeStruct((n_tok, kaw_cols), BF16),
        jax.ShapeDtypeStruct((SWA_KV_COLS, n_tok), BF16),
        jax.ShapeDtypeStruct((qd_rows, n_tok), BF16),
        jax.ShapeDtypeStruct((n_tok, DIFF_QK_COLS), BF16),
        jax.ShapeDtypeStruct((DIFF_HEADS * DIFF_VPAD, n_tok), BF16),
    ]
    return pl.pallas_call(
        _in_proj_kernel,
        out_shape=out_shape,
        grid=(n_tok // ts,),
        in_specs=[tok(D_MODEL), _const_spec((1, D_MODEL)), _const_spec(w_in.shape),
                  pl.BlockSpec((ts // ROPE_GROUPS, ROPE_GROUPS), lambda i: (i, 0)),
                  _const_spec((1, LANES)), _const_spec((1, LANES))],
        out_specs=[tok_t(qa_rows), tok(kaw_cols), tok_t(SWA_KV_COLS), tok_t(qd_rows),
                   tok(DIFF_QK_COLS), tok_t(DIFF_HEADS * DIFF_VPAD)],
        scratch_shapes=[pltpu.VMEM((D_MODEL, n_key_cols), BF16),
                        pltpu.VMEM((n_t_rows, D_MODEL), BF16)],
        compiler_params=_cparams(1),
        name="in_proj",
    )(x2d, g_mix.reshape(1, D_MODEL), w_in, pos_cols, invf, sign)


def _swa_kernel(qt_ref, kc_ref, kp_ref, vc_ref, vp_ref, sink_ref, tri_ref, o_ref):
    n_sub = qt_ref.shape[1] // WINDOW
    kj = lax.broadcasted_iota(jnp.int32, (WINDOW, 2 * WINDOW), 0)
    qi = lax.broadcasted_iota(jnp.int32, (WINDOW, 2 * WINDOW), 1) % WINDOW
    from_prev = kj > qi
    no_prev_bias = jnp.where(pl.program_id(1) == 0, NEG_BIG, 0.0)
    sinks = sink_ref[...] * LOG2E
    pairs = SWA_GROUP // 2
    units = [(r, kv, pair) for r in range(n_sub) for kv in range(SWA_KV_HEADS)
             for pair in range(pairs)]

    def key_rows(r):
        if r == 0:
            return jnp.concatenate([kp_ref[...], kc_ref[0:WINDOW, :]], axis=0)
        return kc_ref[(r - 1) * WINDOW:(r + 1) * WINDOW, :]

    def value_cols(r):
        if r == 0:
            return jnp.concatenate([vp_ref[...], vc_ref[:, 0:WINDOW]], axis=1)
        return vc_ref[:, (r - 1) * WINDOW:(r + 1) * WINDOW]

    def scores(unit):
        r, kv, pair = unit
        head = kv * SWA_GROUP + 2 * pair
        q2 = jnp.concatenate(
            [qt_ref[(head + g) * LANES:(head + g + 1) * LANES, r * WINDOW:(r + 1) * WINDOW]
             for g in range(2)], axis=1)
        kw = key_rows(r)[:, kv * LANES:(kv + 1) * LANES]
        s = jnp.dot(kw, q2, preferred_element_type=F32)
        s_prev = s[:WINDOW] + no_prev_bias if r == 0 else s[:WINDOW]
        return jnp.where(from_prev, s_prev, s[WINDOW:])

    def attend(unit, s):
        r, kv, pair = unit
        tile = kv * pairs + pair
        sink = sinks[tile:tile + 1, :]
        m = jnp.maximum(jnp.max(s, axis=0, keepdims=True), sink)
        e = jnp.exp2(s - m)
        denom = jnp.sum(e, axis=0, keepdims=True) + jnp.exp2(sink - m)
        eb = e.astype(BF16)
        e2 = jnp.concatenate([eb, eb], axis=0) * tri_ref[...]
        vt = value_cols(r)[kv * HEAD_DIM:(kv + 1) * HEAD_DIM, :]
        o = jnp.dot(vt, e2, preferred_element_type=F32) / denom
        return [o[:, :WINDOW], o[:, WINDOW:]]

    pieces = []
    pending = [scores(u) for u in units[:SWA_AHEAD]]
    for idx, unit in enumerate(units):
        if idx + SWA_AHEAD < len(units):
            pending.append(scores(units[idx + SWA_AHEAD]))
        pieces += attend(unit, pending.pop(0))
        if len(pieces) == SWA_Q_HEADS:
            r = unit[0]
            o_ref[r * WINDOW:(r + 1) * WINDOW, :] = (
                jnp.concatenate(pieces, axis=0).T.astype(BF16))
            pieces = []


def _swa(qat, kaw, vat, sinks, bsz, seq):
    ts = SWA_TILE
    per_b = seq // ts
    sub = ts // WINDOW
    prev_blk = lambda b, t: jnp.maximum((b * per_b + t) * sub - 1, 0)
    cur = lambda width: pl.BlockSpec((ts, width), lambda b, t: (b * per_b + t, 0))
    sink_rows = jnp.repeat(sinks.reshape(SWA_Q_HEADS // 2, 2), WINDOW, axis=1)
    kj = lax.broadcasted_iota(jnp.int32, (WINDOW, 2 * WINDOW), 0)
    qi = lax.broadcasted_iota(jnp.int32, (WINDOW, 2 * WINDOW), 1) % WINDOW
    triangles = jnp.concatenate([kj > qi, kj <= qi], axis=0).astype(BF16)
    return pl.pallas_call(
        _swa_kernel,
        out_shape=jax.ShapeDtypeStruct((kaw.shape[0], SWA_Q_COLS), BF16),
        grid=(bsz, per_b),
        in_specs=[pl.BlockSpec((qat.shape[0], ts), lambda b, t: (0, b * per_b + t)),
                  cur(kaw.shape[1]),
                  pl.BlockSpec((WINDOW, kaw.shape[1]), lambda b, t: (prev_blk(b, t), 0)),
                  pl.BlockSpec((SWA_KV_COLS, ts), lambda b, t: (0, b * per_b + t)),
                  pl.BlockSpec((SWA_KV_COLS, WINDOW), lambda b, t: (0, prev_blk(b, t))),
                  _const_spec(sink_rows.shape), _const_spec(triangles.shape)],
        out_specs=cur(SWA_Q_COLS),
        compiler_params=_cparams(2),
        name="swa",
    )(qat, kaw, kaw, vat, vat, sink_rows, triangles)


def _diff_kernel(lam_init, q_ref, k_ref, vt_ref, bias_ref, lamp_ref, g_ref, o_ref,
                 s0_ref, s1_ref, s2_ref, acc_ref, max_ref):
    group_idx = pl.program_id(2)
    th = DIFF_TILE
    top = DIFF_GROUP - 1
    n_groups = k_ref.shape[0] // q_ref.shape[1]
    s_refs = (s0_ref, s1_ref, s2_ref)

    def k_tile(j):
        return k_ref[pl.ds(pl.multiple_of(j * th, th), th), :]

    def vt_tile(j):
        return vt_ref[:, pl.ds(pl.multiple_of(j * th, th), th)]

    hh = th // 2

    def start(unit, slot):
        j, t, c, masked = unit
        s_ref = s_refs[slot]
        qt = q_ref[c * LANES:(c + 1) * LANES, t * th:(t + 1) * th]
        k = k_tile(j)
        if not masked:
            s = jnp.dot(k, qt, preferred_element_type=F32)
            s_ref[...] = s
            return jnp.max(s, axis=0, keepdims=True)
        s_early = jnp.dot(k[:hh], qt, preferred_element_type=F32) + bias_ref[:hh, :]
        s_late = jnp.dot(k[hh:], qt[:, hh:], preferred_element_type=F32) + bias_ref[hh:, hh:]
        s_ref[:hh, :] = s_early
        s_ref[hh:, hh:] = s_late
        max_early = jnp.max(s_early, axis=0, keepdims=True)
        max_late = jnp.max(s_late, axis=0, keepdims=True)
        return jnp.concatenate(
            [max_early[:, :hh], jnp.maximum(max_early[:, hh:], max_late)], axis=1)

    def consume(unit, slot, tile_max):
        j, t, c, masked = unit
        s_ref = s_refs[slot]
        idx = 2 * t + c
        m = max_ref[idx:idx + 1, :]
        m_new = jnp.maximum(m, tile_max)
        alpha = jnp.exp2(m - m_new)
        max_ref[idx:idx + 1, :] = m_new
        vt = vt_tile(j)
        if not masked:
            p = jnp.exp2(s_ref[...] - m_new)
            pv = jnp.dot(vt, p.astype(BF16), preferred_element_type=F32)
            acc_ref[idx] = alpha * acc_ref[idx] + pv
            return
        p_early = jnp.exp2(s_ref[:hh, :] - m_new).astype(BF16)
        p_late = jnp.exp2(s_ref[hh:, hh:] - m_new[:, hh:]).astype(BF16)
        pv_lo = jnp.dot(vt[:, :hh], p_early[:, :hh], preferred_element_type=F32)
        pv_hi = (jnp.dot(vt[:, :hh], p_early[:, hh:], preferred_element_type=F32)
                 + jnp.dot(vt[:, hh:], p_late, preferred_element_type=F32))
        acc_ref[idx, :, :hh] = alpha[:, :hh] * acc_ref[idx, :, :hh] + pv_lo
        acc_ref[idx, :, hh:] = alpha[:, hh:] * acc_ref[idx, :, hh:] + pv_hi

    def finish(t):
        lp = lamp_ref[...]
        lam = (jnp.exp(jnp.sum(lp[0:1] * lp[1:2], axis=-1, keepdims=True))
               - jnp.exp(jnp.sum(lp[2:3] * lp[3:4], axis=-1, keepdims=True)) + lam_init)
        l0 = acc_ref[2 * t, DIFF_VDIM:DIFF_VDIM + 1, :]
        l1 = acc_ref[2 * t + 1, DIFF_VDIM:DIFF_VDIM + 1, :]
        o = (acc_ref[2 * t, :DIFF_VDIM, :] * (1.0 / l0)
             - acc_ref[2 * t + 1, :DIFF_VDIM, :] * (lam / l1))
        ms = jnp.mean(o * o, axis=0, keepdims=True)
        y = o * lax.rsqrt(ms + NORM_EPS) * g_ref[...] * (1.0 - lam_init)
        o_ref[t * th:(t + 1) * th, :] = y.T.astype(BF16)

    def run(units, following, tile_max, ahead=1):
        n_slots = ahead + 1
        todo = units + ([following] if following is not None else [])
        maxes = {0: tile_max}
        started = 1
        for n, unit in enumerate(units):
            while started < len(todo) and started <= n + ahead:
                maxes[started] = start(todo[started], started % n_slots)
                started += 1
            consume(unit, n % n_slots, maxes.pop(n))
            _, t, c, masked = unit
            if masked and c == 1:
                finish(t)
        return maxes.get(len(units))

    def key_tile_units(j, lowest, lowest_masked):
        return [(j, t, c, lowest_masked and t == lowest)
                for t in range(top, lowest - 1, -1) for c in range(2)]

    tile_max = start((0, top, 0, False), 0)
    acc_ref[...] = jnp.zeros_like(acc_ref)
    max_ref[...] = jnp.full(max_ref.shape, NEG_BIG, F32)
    def earlier_group(g, mx):
        units = [u for d in range(DIFF_GROUP)
                 for u in key_tile_units(g * DIFF_GROUP + d, 0, False)]
        return run(units, ((g + 1) * DIFF_GROUP, top, 0, False), mx)

    if n_groups > 1:
        tile_max = lax.fori_loop(0, group_idx, earlier_group, tile_max)
    j0 = DIFF_GROUP * group_idx
    tail = [u for d in range(DIFF_GROUP) for u in key_tile_units(j0 + d, d, True)]
    run(tail, None, tile_max, ahead=DIFF_TAIL_AHEAD)


def _diff_attn(qdt, kd, vdt, lam_params, g_diff, lam_init, bsz, seq):
    th = DIFF_TILE
    tq = DIFF_GROUP * th
    nq = seq // tq
    key = lax.broadcasted_iota(jnp.int32, (th, th), 0)
    qry = lax.broadcasted_iota(jnp.int32, (th, th), 1)
    causal_bias = jnp.where(key <= qry, 0.0, NEG_BIG).astype(F32)
    return pl.pallas_call(
        functools.partial(_diff_kernel, lam_init),
        out_shape=jax.ShapeDtypeStruct(kd.shape, BF16),
        grid=(bsz, DIFF_HEADS, nq),
        in_specs=[
            pl.BlockSpec((2 * LANES, tq), lambda b, h, i: (h, b * nq + i)),
            pl.BlockSpec((seq, DIFF_VDIM), lambda b, h, i: (b, h)),
            pl.BlockSpec((DIFF_VPAD, seq), lambda b, h, i: (h, b)),
            _const_spec((th, th)),
            _const_spec((4, HEAD_DIM)),
            _const_spec((DIFF_VDIM, 1)),
        ],
        out_specs=pl.BlockSpec((tq, DIFF_VDIM), lambda b, h, i: (b * nq + i, h)),
        scratch_shapes=[pltpu.VMEM((th, th), F32)] * (DIFF_TAIL_AHEAD + 1) + [
                        pltpu.VMEM((2 * DIFF_GROUP, DIFF_VPAD, th), F32),
                        pltpu.VMEM((2 * DIFF_GROUP, th), F32)],
        compiler_params=_cparams(3),
        name="diff_attn",
    )(qdt, kd, vdt, causal_bias, lam_params, g_diff.reshape(DIFF_VDIM, 1))


def _mem_kv_kernel(m_ref, g_ref, wkv_ref, k_ref, v_ref, w_ref):
    @pl.when(pl.program_id(0) == 0)
    def _prepare_weights():
        w_ref[...] = wkv_ref[...].astype(BF16)

    hm = _rms(m_ref[...], g_ref[...]).astype(BF16)
    k_ref[...] = jnp.dot(hm, w_ref[:, :D_MODEL], preferred_element_type=F32).astype(BF16)
    v_ref[...] = jnp.dot(hm, w_ref[:, D_MODEL:], preferred_element_type=F32).astype(BF16)


def _mem_kv(mem2d, g_mem, w_ckv, bsz, mem_len):
    blk = pl.BlockSpec((mem_len, D_MODEL), lambda b: (b, 0))
    return pl.pallas_call(
        _mem_kv_kernel,
        out_shape=[jax.ShapeDtypeStruct(mem2d.shape, BF16)] * 2,
        grid=(bsz,),
        in_specs=[blk, _const_spec((1, D_MODEL)), _const_spec((D_MODEL, 2 * D_MODEL))],
        out_specs=[blk, blk],
        scratch_shapes=[pltpu.VMEM(w_ckv.shape, BF16)],
        compiler_params=_cparams(1),
        name="mem_kv",
    )(mem2d, g_mem.reshape(1, D_MODEL), w_ckv)


def _mix_cross_kernel(a_ref, b_ref, x_ref, wout_ref, g_ref, wcq_ref, k_ref, v_ref, wco_ref,
                      wup_ref, wdown_ref, o_ref, wup_bf_ref, wdown_bf_ref,
                      w_ref, wq_ref, wo_ref):
    @pl.when(pl.program_id(0) == 0)
    def _prepare_weights():
        w_ref[...] = wout_ref[...].astype(BF16)
        wq_ref[...] = wcq_ref[...].astype(BF16)
        wo_ref[...] = wco_ref[...].astype(BF16)

    wup_bf_ref[...] = wup_ref[...].astype(BF16)
    wdown_bf_ref[...] = wdown_ref[...].astype(BF16)

    half = a_ref.shape[1]
    y = (jnp.dot(a_ref[...], w_ref[:half, :], preferred_element_type=F32)
         + jnp.dot(b_ref[...], w_ref[half:, :], preferred_element_type=F32))
    x1 = x_ref[...] + y
    hc = _rms(x1, g_ref[...]).astype(BF16)
    qc = jnp.dot(hc, wq_ref[...], preferred_element_type=F32) * (CROSS_HEAD_DIM ** -0.5 * LOG2E)
    qc = qc.astype(BF16)

    def head_cols(h):
        return slice(h * CROSS_HEAD_DIM, (h + 1) * CROSS_HEAD_DIM)

    def scores(h):
        return lax.dot_general(qc[:, head_cols(h)], k_ref[:, head_cols(h)],
                               (((1,), (1,)), ((), ())), preferred_element_type=F32)

    def attend(h, s):
        m = jnp.max(s, axis=-1, keepdims=True)
        e = jnp.exp2(s - m)
        inv = 1.0 / jnp.sum(e, axis=-1, keepdims=True)
        pv = jnp.dot(e.astype(BF16), v_ref[:, head_cols(h)], preferred_element_type=F32)
        return (pv * inv).astype(BF16)

    heads = []
    s_next = scores(0)
    for h in range(CROSS_HEADS):
        s_cur = s_next
        if h + 1 < CROSS_HEADS:
            s_next = scores(h + 1)
        heads.append(attend(h, s_cur))
    o = jnp.concatenate(heads, axis=-1)
    o_ref[...] = x1 + jnp.dot(o, wo_ref[...], preferred_element_type=F32)


def _mix_cross(out_a, out_b, x2d, w_out, g_cross, w_cq, kc, vc, w_co, w_up, w_down, seq,
               mem_len):
    n_tok = x2d.shape[0]
    ts = TOK_TILE
    n_steps = n_tok // ts
    per_b = seq // ts
    half = SWA_Q_COLS
    tok = lambda width: pl.BlockSpec((ts, width), lambda i: (i, 0))
    memblk = pl.BlockSpec((mem_len, D_MODEL), lambda i: (i // per_b, 0))
    square = _const_spec((D_MODEL, D_MODEL))
    slab = lambda w: pl.BlockSpec((w.shape[0] // n_steps, w.shape[1]), lambda i: (i, 0))
    return pl.pallas_call(
        _mix_cross_kernel,
        out_shape=[jax.ShapeDtypeStruct(x2d.shape, F32),
                   jax.ShapeDtypeStruct(w_up.shape, BF16),
                   jax.ShapeDtypeStruct(w_down.shape, BF16)],
        grid=(n_steps,),
        in_specs=[tok(half), tok(half), tok(D_MODEL), square, _const_spec((1, D_MODEL)), square,
                  memblk, memblk, square, slab(w_up), slab(w_down)],
        out_specs=[tok(D_MODEL), slab(w_up), slab(w_down)],
        scratch_shapes=[pltpu.VMEM((D_MODEL, D_MODEL), BF16)] * 3,
        compiler_params=_cparams(1),
        name="mix_cross",
    )(out_a, out_b, x2d, w_out, g_cross.reshape(1, D_MODEL), w_cq, kc, vc, w_co, w_up, w_down)


def _mlp_kernel(final_norm, x_ref, g_ref, wu_ref, wd_ref, gf_ref, o_ref):
    d_ff = wu_ref.shape[1]
    sub = x_ref.shape[0] // MLP_SUBTILES
    xs = [x_ref[t * sub:(t + 1) * sub, :] for t in range(MLP_SUBTILES)]
    hbs = [_rms(x, g_ref[...]).astype(BF16) for x in xs]
    accs = list(xs)
    for c in range(d_ff // FF_CHUNK):
        cols = slice(c * FF_CHUNK, (c + 1) * FF_CHUNK)
        for t in range(MLP_SUBTILES):
            u = jnp.dot(hbs[t], wu_ref[:, cols], preferred_element_type=F32)
            r = jnp.maximum(u, 0.0)
            accs[t] = accs[t] + jnp.dot((r * r).astype(BF16), wd_ref[cols, :],
                                        preferred_element_type=F32)
    for t in range(MLP_SUBTILES):
        o_ref[t * sub:(t + 1) * sub, :] = _rms(accs[t], gf_ref[...]) if final_norm else accs[t]


def _mlp(x2, g_mlp, w_up, w_down, g_final, final_norm):
    n_tok = x2.shape[0]
    ts = TOK_TILE
    tok = pl.BlockSpec((ts, D_MODEL), lambda i: (i, 0))
    return pl.pallas_call(
        functools.partial(_mlp_kernel, final_norm),
        out_shape=jax.ShapeDtypeStruct(x2.shape, F32),
        grid=(n_tok // ts,),
        in_specs=[tok, _const_spec((1, D_MODEL)), _const_spec(w_up.shape),
                  _const_spec(w_down.shape), _const_spec((1, D_MODEL))],
        out_specs=tok,
        compiler_params=_cparams(1),
        name="mlp",
    )(x2, g_mlp.reshape(1, D_MODEL), w_up, w_down, g_final.reshape(1, D_MODEL))


def kernel(x, mem, positions, g_mix, w_in, sinks, lambda_q1, lambda_k1, lambda_q2, lambda_k2,
           g_diff, w_out, g_cross, g_mem, w_cq, w_ckv, w_co, g_mlp, w_up, w_down, g_final):
    bsz, seq, _ = x.shape
    mem_len = mem.shape[1]
    depth = w_in.shape[0]
    xf = x.reshape(bsz * seq, D_MODEL)
    memf = mem.reshape(bsz * mem_len, D_MODEL)
    for l in range(depth):
        lam_init = 0.8 - 0.6 * math.exp(-0.3 * l)
        qat, kaw, vat, qdt, kd, vdt = _in_proj(xf, g_mix[l], w_in[l], positions)
        out_a = _swa(qat, kaw, vat, sinks[l], bsz, seq)
        lam_params = jnp.stack([lambda_q1[l], lambda_k1[l], lambda_q2[l], lambda_k2[l]])
        out_b = _diff_attn(qdt, kd, vdt, lam_params, g_diff[l], lam_init, bsz, seq)
        kc, vc = _mem_kv(memf, g_mem[l], w_ckv[l], bsz, mem_len)
        x2, w_up_bf, w_down_bf = _mix_cross(out_a, out_b, xf, w_out[l], g_cross[l], w_cq[l],
                                            kc, vc, w_co[l], w_up[l], w_down[l], seq, mem_len)
        xf = _mlp(x2, g_mlp[l], w_up_bf, w_down_bf, g_final, l == depth - 1)
    return xf.reshape(bsz, seq, D_MODEL)
```

```python
import functools
import math

import jax
import jax.numpy as jnp
from jax import lax
from jax.experimental import pallas as pl
from jax.experimental.pallas import tpu as pltpu

D_MODEL = 1024
HEAD_DIM = 64
SWA_Q_HEADS = 8
SWA_KV_HEADS = 2
SWA_GROUP = SWA_Q_HEADS // SWA_KV_HEADS
WINDOW = 128
DIFF_HEADS = 4
DIFF_VDIM = 2 * HEAD_DIM
DIFF_VPAD = DIFF_VDIM + 16
SWA_Q_COLS = SWA_Q_HEADS * HEAD_DIM
SWA_KV_COLS = SWA_KV_HEADS * HEAD_DIM
DIFF_QK_COLS = DIFF_HEADS * 2 * HEAD_DIM
DIFF_V_COLS = DIFF_HEADS * DIFF_VDIM
CROSS_HEADS = 4
CROSS_HEAD_DIM = D_MODEL // CROSS_HEADS
ROPE_THETA = 10000.0
NORM_EPS = 1e-5

LANES = 128
TOK_TILE = 1024
SWA_TILE = 1024
DIFF_TILE = 512
DIFF_GROUP = 4
DIFF_AHEAD = 2
DIFF_RING = 4
FF_CHUNK = 512
MLP_SUBTILES = 2
SWA_AHEAD = 6
VMEM_LIMIT = 56 * 1024 * 1024
NEG_BIG = -1e30
LOG2E = math.log2(math.e)

BF16 = jnp.bfloat16
F32 = jnp.float32


def _cparams(n_axes):
    return pltpu.CompilerParams(dimension_semantics=("arbitrary",) * n_axes,
                                vmem_limit_bytes=VMEM_LIMIT)


def _rms(x, g):
    ms = jnp.mean(x * x, axis=-1, keepdims=True)
    return x * lax.rsqrt(ms + NORM_EPS) * g


def _const_spec(shape):
    return pl.BlockSpec(shape, lambda *_: (0,) * len(shape), pipeline_mode=pl.Buffered(1))


ROPE_HALF = HEAD_DIM // 2
ROPE_GROUPS = LANES // ROPE_HALF


def _rope_tables(pos, invf, sign):
    pos = pos.astype(F32)
    lane_group = lax.broadcasted_iota(jnp.int32, (pos.shape[0], LANES), 1) // ROPE_HALF
    pos_l = pos[:, ROPE_GROUPS - 1:ROPE_GROUPS]
    for g in range(ROPE_GROUPS - 2, -1, -1):
        pos_l = jnp.where(lane_group == g, pos[:, g:g + 1], pos_l)
    ang = pos_l * invf
    tables = []
    transposed = []
    for tab in (jnp.cos(ang), jnp.sin(ang)):
        rolled = [tab] + [pltpu.roll(tab, k * ROPE_HALF, 1) for k in range(1, ROPE_GROUPS)]
        quarters = []
        for g in range(ROPE_GROUPS):
            shift = (lane_group - g) % ROPE_GROUPS
            tiled = rolled[ROPE_GROUPS - 1]
            for k in range(ROPE_GROUPS - 2, -1, -1):
                tiled = jnp.where(shift == k, rolled[k], tiled)
            quarters.append(tiled)
        tables.append(jnp.concatenate(quarters, axis=0))
        tab_t = tab.T
        transposed.append(jnp.concatenate(
            [tab_t[g * ROPE_HALF:(g + 1) * ROPE_HALF, :] for g in range(ROPE_GROUPS)], axis=1))
    return tables[0], tables[1] * sign, transposed[0], transposed[1]


def _rope_inputs(positions, tile):
    n_tok = positions.size
    inv_freq = ROPE_THETA ** (-jnp.arange(0, HEAD_DIM, 2, dtype=F32) / HEAD_DIM)
    quarter = tile // ROPE_GROUPS
    pos_cols = positions.reshape(n_tok // tile, ROPE_GROUPS, quarter)
    pos_cols = pos_cols.transpose(0, 2, 1).reshape(n_tok // ROPE_GROUPS, ROPE_GROUPS)
    invf = jnp.tile(inv_freq, ROPE_GROUPS).reshape(1, LANES)
    sign = jnp.tile(jnp.repeat(jnp.array([-1.0, 1.0], F32), ROPE_HALF), 2).reshape(1, LANES)
    return pos_cols, invf, sign


def _rope(p, cos, sin):
    width = p.shape[-1]
    lane = lax.broadcasted_iota(jnp.int32, p.shape, 1)
    first_half = (lane % HEAD_DIM) < (HEAD_DIM // 2)
    swapped = jnp.where(first_half,
                        pltpu.roll(p, width - HEAD_DIM // 2, 1),
                        pltpu.roll(p, HEAD_DIM // 2, 1))
    return p * cos + swapped * sin


IN_QA = (0, SWA_Q_COLS)
IN_KA = (IN_QA[1], IN_QA[1] + SWA_KV_COLS)
IN_VA = (IN_KA[1], IN_KA[1] + SWA_KV_COLS)
IN_QD = (IN_VA[1], IN_VA[1] + DIFF_QK_COLS)
IN_KD = (IN_QD[1], IN_QD[1] + DIFF_QK_COLS)
IN_VD = (IN_KD[1], IN_KD[1] + DIFF_V_COLS)
IN_TRANSPOSED = (IN_QA, IN_QD, IN_VA, IN_VD)


def _in_proj_kernel(x_ref, g_ref, win_ref, pos_ref, invf_ref, sign_ref,
                    qat_ref, kaw_ref, vat_ref, qdt_ref, kd_ref, vdt_ref, wk_ref, wt_ref):
    @pl.when(pl.program_id(0) == 0)
    def _prepare_weights():
        wk_ref[:, :SWA_KV_COLS] = win_ref[:, IN_KA[0]:IN_KA[1]].astype(BF16)
        wk_ref[:, SWA_KV_COLS:] = win_ref[:, IN_KD[0]:IN_KD[1]].astype(BF16)
        row = 0
        for c0, c1 in IN_TRANSPOSED:
            wt_ref[row:row + c1 - c0, :] = win_ref[:, c0:c1].T.astype(BF16)
            row += c1 - c0

    hb = _rms(x_ref[...], g_ref[...]).astype(BF16)
    n_tok = hb.shape[0]
    nt_dims = (((1,), (1,)), ((), ()))

    t_rows = [0]
    for c0, c1 in IN_TRANSPOSED:
        t_rows.append(t_rows[-1] + c1 - c0)

    def proj_t(first, last):
        return lax.dot_general(wt_ref[t_rows[first]:t_rows[last], :], hb, nt_dims,
                               preferred_element_type=F32)

    vt = proj_t(2, 4)
    ka = jnp.dot(hb, wk_ref[:, :SWA_KV_COLS], preferred_element_type=F32)
    kd_parts = [jnp.dot(hb, wk_ref[:, SWA_KV_COLS + c * 256:SWA_KV_COLS + (c + 1) * 256],
                        preferred_element_type=F32) for c in range(DIFF_QK_COLS // 256)]
    qat = proj_t(0, 1)
    qdt = proj_t(1, 2)

    cos1, sin1, cos_t, sin_t = _rope_tables(pos_ref[...], invf_ref[...], sign_ref[...])
    cos2 = jnp.concatenate([cos1, cos1], axis=-1)
    sin2 = jnp.concatenate([sin1, sin1], axis=-1)

    vat_ref[...] = vt[:SWA_KV_COLS].astype(BF16)
    pad_row = lax.broadcasted_iota(jnp.int32, (DIFF_VPAD - DIFF_VDIM, n_tok), 0)
    ones_rows = jnp.where(pad_row == 0, 1.0, 0.0).astype(BF16)
    for h in range(DIFF_HEADS):
        src = SWA_KV_COLS + h * DIFF_VDIM
        vdt_ref[h * DIFF_VPAD:h * DIFF_VPAD + DIFF_VDIM, :] = vt[src:src + DIFF_VDIM].astype(BF16)
        vdt_ref[h * DIFF_VPAD + DIFF_VDIM:(h + 1) * DIFF_VPAD, :] = ones_rows

    ka = _rope(ka, cos1, sin1)
    ka_swapped = pltpu.roll(ka, HEAD_DIM, 1)
    low_half = lax.broadcasted_iota(jnp.int32, ka.shape, 1) < HEAD_DIM
    kaw_ref[:, :LANES] = jnp.where(low_half, ka, ka_swapped).astype(BF16)
    kaw_ref[:, LANES:] = jnp.where(low_half, ka_swapped, ka).astype(BF16)
    for c, p in enumerate(kd_parts):
        kd_ref[:, c * 256:(c + 1) * 256] = _rope(p, cos2, sin2).astype(BF16)

    q_scale = HEAD_DIM ** -0.5 * LOG2E
    cos_q = cos_t * q_scale
    sin_q = sin_t * q_scale
    zero_rows = jnp.zeros((HEAD_DIM, n_tok), BF16)

    def store_head(pt, src, out_ref, dst, zero_dst):
        x1 = pt[src:src + ROPE_HALF]
        x2 = pt[src + ROPE_HALF:src + HEAD_DIM]
        out_ref[dst:dst + ROPE_HALF, :] = (x1 * cos_q - x2 * sin_q).astype(BF16)
        out_ref[dst + ROPE_HALF:dst + HEAD_DIM, :] = (x2 * cos_q + x1 * sin_q).astype(BF16)
        out_ref[zero_dst:zero_dst + HEAD_DIM, :] = zero_rows

    for h in range(SWA_Q_HEADS):
        store_head(qat, h * HEAD_DIM, qat_ref, h * LANES, h * LANES + HEAD_DIM)
    for hc in range(2 * DIFF_HEADS):
        c = hc % 2
        store_head(qdt, hc * HEAD_DIM, qdt_ref, hc * LANES + c * HEAD_DIM,
                   hc * LANES + (1 - c) * HEAD_DIM)


def _in_proj(x2d, g_mix, w_in, positions):
    n_tok = x2d.shape[0]
    ts = TOK_TILE
    pos_cols, invf, sign = _rope_inputs(positions, ts)
    n_key_cols = SWA_KV_COLS + DIFF_QK_COLS
    n_t_rows = sum(c1 - c0 for c0, c1 in IN_TRANSPOSED)
    tok = lambda width: pl.BlockSpec((ts, width), lambda i: (i, 0))
    tok_t = lambda rows: pl.BlockSpec((rows, ts), lambda i: (0, i))
    qa_rows = SWA_Q_HEADS * LANES
    kaw_cols = SWA_KV_HEADS * LANES
    qd_rows = DIFF_HEADS * 2 * LANES
    out_shape = [
        jax.ShapeDtypeStruct((qa_rows, n_tok), BF16),
        jax.ShapeDtypeStruct((n_tok, kaw_cols), BF16),
        jax.ShapeDtypeStruct((SWA_KV_COLS, n_tok), BF16),
        jax.ShapeDtypeStruct((qd_rows, n_tok), BF16),
        jax.ShapeDtypeStruct((n_tok, DIFF_QK_COLS), BF16),
        jax.ShapeDtypeStruct((DIFF_HEADS * DIFF_VPAD, n_tok), BF16),
    ]
    return pl.pallas_call(
        _in_proj_kernel,
        out_shape=out_shape,
        grid=(n_tok // ts,),
        in_specs=[tok(D_MODEL), _const_spec((1, D_MODEL)), _const_spec(w_in.shape),
                  pl.BlockSpec((ts // ROPE_GROUPS, ROPE_GROUPS), lambda i: (i, 0)),
                  _const_spec((1, LANES)), _const_spec((1, LANES))],
        out_specs=[tok_t(qa_rows), tok(kaw_cols), tok_t(SWA_KV_COLS), tok_t(qd_rows),
                   tok(DIFF_QK_COLS), tok_t(DIFF_HEADS * DIFF_VPAD)],
        scratch_shapes=[pltpu.VMEM((D_MODEL, n_key_cols), BF16),
                        pltpu.VMEM((n_t_rows, D_MODEL), BF16)],
        compiler_params=_cparams(1),
        name="in_proj",
    )(x2d, g_mix.reshape(1, D_MODEL), w_in, pos_cols, invf, sign)


def _swa_kernel(qt_ref, kc_ref, kp_ref, vc_ref, vp_ref, sink_ref, tri_ref, o_ref):
    n_sub = qt_ref.shape[1] // WINDOW
    kj = lax.broadcasted_iota(jnp.int32, (WINDOW, 2 * WINDOW), 0)
    qi = lax.broadcasted_iota(jnp.int32, (WINDOW, 2 * WINDOW), 1) % WINDOW
    from_prev = kj > qi
    no_prev_bias = jnp.where(pl.program_id(1) == 0, NEG_BIG, 0.0)
    sinks = sink_ref[...] * LOG2E
    pairs = SWA_GROUP // 2
    units = [(r, kv, pair) for r in range(n_sub) for kv in range(SWA_KV_HEADS)
             for pair in range(pairs)]

    def key_rows(r):
        if r == 0:
            return jnp.concatenate([kp_ref[...], kc_ref[0:WINDOW, :]], axis=0)
        return kc_ref[(r - 1) * WINDOW:(r + 1) * WINDOW, :]

    def value_cols(r):
        if r == 0:
            return jnp.concatenate([vp_ref[...], vc_ref[:, 0:WINDOW]], axis=1)
        return vc_ref[:, (r - 1) * WINDOW:(r + 1) * WINDOW]

    def scores(unit):
        r, kv, pair = unit
        head = kv * SWA_GROUP + 2 * pair
        q2 = jnp.concatenate(
            [qt_ref[(head + g) * LANES:(head + g + 1) * LANES, r * WINDOW:(r + 1) * WINDOW]
             for g in range(2)], axis=1)
        kw = key_rows(r)[:, kv * LANES:(kv + 1) * LANES]
        s = jnp.dot(kw, q2, preferred_element_type=F32)
        s_prev = s[:WINDOW] + no_prev_bias if r == 0 else s[:WINDOW]
        return jnp.where(from_prev, s_prev, s[WINDOW:])

    def attend(unit, s):
        r, kv, pair = unit
        tile = kv * pairs + pair
        sink = sinks[tile:tile + 1, :]
        m = jnp.maximum(jnp.max(s, axis=0, keepdims=True), sink)
        e = jnp.exp2(s - m)
        denom = jnp.sum(e, axis=0, keepdims=True) + jnp.exp2(sink - m)
        eb = e.astype(BF16)
        e2 = jnp.concatenate([eb, eb], axis=0) * tri_ref[...]
        vt = value_cols(r)[kv * HEAD_DIM:(kv + 1) * HEAD_DIM, :]
        o = jnp.dot(vt, e2, preferred_element_type=F32) / denom
        return [o[:, :WINDOW], o[:, WINDOW:]]

    pieces = []
    pending = [scores(u) for u in units[:SWA_AHEAD]]
    for idx, unit in enumerate(units):
        if idx + SWA_AHEAD < len(units):
            pending.append(scores(units[idx + SWA_AHEAD]))
        pieces += attend(unit, pending.pop(0))
        if len(pieces) == SWA_Q_HEADS:
            r = unit[0]
            o_ref[r * WINDOW:(r + 1) * WINDOW, :] = (
                jnp.concatenate(pieces, axis=0).T.astype(BF16))
            pieces = []


def _swa(qat, kaw, vat, sinks, bsz, seq):
    ts = SWA_TILE
    per_b = seq // ts
    sub = ts // WINDOW
    prev_blk = lambda b, t: jnp.maximum((b * per_b + t) * sub - 1, 0)
    cur = lambda width: pl.BlockSpec((ts, width), lambda b, t: (b * per_b + t, 0))
    sink_rows = jnp.repeat(sinks.reshape(SWA_Q_HEADS // 2, 2), WINDOW, axis=1)
    kj = lax.broadcasted_iota(jnp.int32, (WINDOW, 2 * WINDOW), 0)
    qi = lax.broadcasted_iota(jnp.int32, (WINDOW, 2 * WINDOW), 1) % WINDOW
    triangles = jnp.concatenate([kj > qi, kj <= qi], axis=0).astype(BF16)
    return pl.pallas_call(
        _swa_kernel,
        out_shape=jax.ShapeDtypeStruct((kaw.shape[0], SWA_Q_COLS), BF16),
        grid=(bsz, per_b),
        in_specs=[pl.BlockSpec((qat.shape[0], ts), lambda b, t: (0, b * per_b + t)),
                  cur(kaw.shape[1]),
                  pl.BlockSpec((WINDOW, kaw.shape[1]), lambda b, t: (prev_blk(b, t), 0)),
                  pl.BlockSpec((SWA_KV_COLS, ts), lambda b, t: (0, b * per_b + t)),
                  pl.BlockSpec((SWA_KV_COLS, WINDOW), lambda b, t: (0, prev_blk(b, t))),
                  _const_spec(sink_rows.shape), _const_spec(triangles.shape)],
        out_specs=cur(SWA_Q_COLS),
        compiler_params=_cparams(2),
        name="swa",
    )(qat, kaw, kaw, vat, vat, sink_rows, triangles)


def _diff_kernel(lam_init, q_ref, k_ref, vt_ref, bias_ref, lamp_ref, g_ref, o_ref,
                 s0_ref, s1_ref, s2_ref, s3_ref, acc_ref, max_ref):
    group_idx = pl.program_id(2)
    th = DIFF_TILE
    top = DIFF_GROUP - 1
    n_groups = k_ref.shape[0] // q_ref.shape[1]
    s_refs = (s0_ref, s1_ref, s2_ref, s3_ref)

    def k_tile(j):
        return k_ref[pl.ds(pl.multiple_of(j * th, th), th), :]

    def vt_tile(j):
        return vt_ref[:, pl.ds(pl.multiple_of(j * th, th), th)]

    hh = th // 2

    def start(unit, slot):
        j, t, c, masked = unit
        s_ref = s_refs[slot]
        qt = q_ref[c * LANES:(c + 1) * LANES, t * th:(t + 1) * th]
        k = k_tile(j)
        if not masked:
            s = jnp.dot(k, qt, preferred_element_type=F32)
            s_ref[...] = s
            return jnp.max(s, axis=0, keepdims=True)
        s_early = jnp.dot(k[:hh], qt, preferred_element_type=F32) + bias_ref[:hh, :]
        s_late = jnp.dot(k[hh:], qt[:, hh:], preferred_element_type=F32) + bias_ref[hh:, hh:]
        s_ref[:hh, :] = s_early
        s_ref[hh:, hh:] = s_late
        max_early = jnp.max(s_early, axis=0, keepdims=True)
        max_late = jnp.max(s_late, axis=0, keepdims=True)
        return jnp.concatenate(
            [max_early[:, :hh], jnp.maximum(max_early[:, hh:], max_late)], axis=1)

    def consume(unit, slot, tile_max):
        j, t, c, masked = unit
        s_ref = s_refs[slot]
        idx = 2 * t + c
        m = max_ref[idx:idx + 1, :]
        m_new = jnp.maximum(m, tile_max)
        alpha = jnp.exp2(m - m_new)
        max_ref[idx:idx + 1, :] = m_new
        vt = vt_tile(j)
        if not masked:
            p = jnp.exp2(s_ref[...] - m_new)
            pv = jnp.dot(vt, p.astype(BF16), preferred_element_type=F32)
            acc_ref[idx] = alpha * acc_ref[idx] + pv
            return
        p_early = jnp.exp2(s_ref[:hh, :] - m_new).astype(BF16)
        p_late = jnp.exp2(s_ref[hh:, hh:] - m_new[:, hh:]).astype(BF16)
        pv_lo = jnp.dot(vt[:, :hh], p_early[:, :hh], preferred_element_type=F32)
        pv_hi = (jnp.dot(vt[:, :hh], p_early[:, hh:], preferred_element_type=F32)
                 + jnp.dot(vt[:, hh:], p_late, preferred_element_type=F32))
        acc_ref[idx, :, :hh] = alpha[:, :hh] * acc_ref[idx, :, :hh] + pv_lo
        acc_ref[idx, :, hh:] = alpha[:, hh:] * acc_ref[idx, :, hh:] + pv_hi

    def finish(t):
        lp = lamp_ref[...]
        lam = (jnp.exp(jnp.sum(lp[0:1] * lp[1:2], axis=-1, keepdims=True))
               - jnp.exp(jnp.sum(lp[2:3] * lp[3:4], axis=-1, keepdims=True)) + lam_init)
        l0 = acc_ref[2 * t, DIFF_VDIM:DIFF_VDIM + 1, :]
        l1 = acc_ref[2 * t + 1, DIFF_VDIM:DIFF_VDIM + 1, :]
        o = (acc_ref[2 * t, :DIFF_VDIM, :] * (1.0 / l0)
             - acc_ref[2 * t + 1, :DIFF_VDIM, :] * (lam / l1))
        ms = jnp.mean(o * o, axis=0, keepdims=True)
        y = o * lax.rsqrt(ms + NORM_EPS) * g_ref[...] * (1.0 - lam_init)
        o_ref[t * th:(t + 1) * th, :] = y.T.astype(BF16)

    n_slots = len(s_refs)

    def run(units, following, pending):
        todo = units + list(following)
        maxes = dict(enumerate(pending))
        started = len(pending)
        for n, unit in enumerate(units):
            while started < len(todo) and started <= n + DIFF_AHEAD:
                maxes[started] = start(todo[started], started % n_slots)
                started += 1
            consume(unit, n % n_slots, maxes.pop(n))
            _, t, c, masked = unit
            if masked and c == 1:
                finish(t)
        return tuple(maxes[len(units) + k] for k in range(len(following)))

    def key_tile_units(j, lowest, lowest_masked):
        return [(j, t, c, lowest_masked and t == lowest)
                for t in range(top, lowest - 1, -1) for c in range(2)]

    def first_units(j):
        return key_tile_units(j, 0, False)[:DIFF_AHEAD]

    pending = tuple(start(u, n) for n, u in enumerate(first_units(0)))
    acc_ref[...] = jnp.zeros_like(acc_ref)
    max_ref[...] = jnp.full(max_ref.shape, NEG_BIG, F32)
    def earlier_group(g, mx):
        units = [u for d in range(DIFF_GROUP)
                 for u in key_tile_units(g * DIFF_GROUP + d, 0, False)]
        return run(units, first_units((g + 1) * DIFF_GROUP), mx)

    if n_groups > 1:
        pending = lax.fori_loop(0, group_idx, earlier_group, pending)
    j0 = DIFF_GROUP * group_idx
    tail = [u for d in range(DIFF_GROUP) for u in key_tile_units(j0 + d, d, True)]
    run(tail, (), pending)


def _diff_attn(qdt, kd, vdt, lam_params, g_diff, lam_init, bsz, seq):
    th = DIFF_TILE
    tq = DIFF_GROUP * th
    nq = seq // tq
    key = lax.broadcasted_iota(jnp.int32, (th, th), 0)
    qry = lax.broadcasted_iota(jnp.int32, (th, th), 1)
    causal_bias = jnp.where(key <= qry, 0.0, NEG_BIG).astype(F32)
    return pl.pallas_call(
        functools.partial(_diff_kernel, lam_init),
        out_shape=jax.ShapeDtypeStruct(kd.shape, BF16),
        grid=(bsz, DIFF_HEADS, nq),
        in_specs=[
            pl.BlockSpec((2 * LANES, tq), lambda b, h, i: (h, b * nq + i)),
            pl.BlockSpec((seq, DIFF_VDIM), lambda b, h, i: (b, h)),
            pl.BlockSpec((DIFF_VPAD, seq), lambda b, h, i: (h, b)),
            _const_spec((th, th)),
            _const_spec((4, HEAD_DIM)),
            _const_spec((DIFF_VDIM, 1)),
        ],
        out_specs=pl.BlockSpec((tq, DIFF_VDIM), lambda b, h, i: (b * nq + i, h)),
        scratch_shapes=[pltpu.VMEM((th, th), F32)] * DIFF_RING + [
                        pltpu.VMEM((2 * DIFF_GROUP, DIFF_VPAD, th), F32),
                        pltpu.VMEM((2 * DIFF_GROUP, th), F32)],
        compiler_params=_cparams(3),
        name="diff_attn",
    )(qdt, kd, vdt, causal_bias, lam_params, g_diff.reshape(DIFF_VDIM, 1))


def _mem_kv_kernel(m_ref, g_ref, wkv_ref, k_ref, v_ref, w_ref):
    @pl.when(pl.program_id(0) == 0)
    def _prepare_weights():
        w_ref[...] = wkv_ref[...].astype(BF16)

    hm = _rms(m_ref[...], g_ref[...]).astype(BF16)
    k_ref[...] = jnp.dot(hm, w_ref[:, :D_MODEL], preferred_element_type=F32).astype(BF16)
    v_ref[...] = jnp.dot(hm, w_ref[:, D_MODEL:], preferred_element_type=F32).astype(BF16)


def _mem_kv(mem2d, g_mem, w_ckv, bsz, mem_len):
    blk = pl.BlockSpec((mem_len, D_MODEL), lambda b: (b, 0))
    return pl.pallas_call(
        _mem_kv_kernel,
        out_shape=[jax.ShapeDtypeStruct(mem2d.shape, BF16)] * 2,
        grid=(bsz,),
        in_specs=[blk, _const_spec((1, D_MODEL)), _const_spec((D_MODEL, 2 * D_MODEL))],
        out_specs=[blk, blk],
        scratch_shapes=[pltpu.VMEM(w_ckv.shape, BF16)],
        compiler_params=_cparams(1),
        name="mem_kv",
    )(mem2d, g_mem.reshape(1, D_MODEL), w_ckv)


def _mix_cross_kernel(a_ref, b_ref, x_ref, wout_ref, g_ref, wcq_ref, k_ref, v_ref, wco_ref,
                      wup_ref, wdown_ref, o_ref, wup_bf_ref, wdown_bf_ref,
                      w_ref, wq_ref, wo_ref):
    @pl.when(pl.program_id(0) == 0)
    def _prepare_weights():
        w_ref[...] = wout_ref[...].astype(BF16)
        wq_ref[...] = wcq_ref[...].astype(BF16)
        wo_ref[...] = wco_ref[...].astype(BF16)

    wup_bf_ref[...] = wup_ref[...].astype(BF16)
    wdown_bf_ref[...] = wdown_ref[...].astype(BF16)

    half = a_ref.shape[1]
    y = (jnp.dot(a_ref[...], w_ref[:half, :], preferred_element_type=F32)
         + jnp.dot(b_ref[...], w_ref[half:, :], preferred_element_type=F32))
    x1 = x_ref[...] + y
    hc = _rms(x1, g_ref[...]).astype(BF16)
    qc = jnp.dot(hc, wq_ref[...], preferred_element_type=F32) * (CROSS_HEAD_DIM ** -0.5 * LOG2E)
    qc = qc.astype(BF16)

    def head_cols(h):
        return slice(h * CROSS_HEAD_DIM, (h + 1) * CROSS_HEAD_DIM)

    def scores(h):
        return lax.dot_general(qc[:, head_cols(h)], k_ref[:, head_cols(h)],
                               (((1,), (1,)), ((), ())), preferred_element_type=F32)

    def attend(h, s):
        m = jnp.max(s, axis=-1, keepdims=True)
        e = jnp.exp2(s - m)
        inv = 1.0 / jnp.sum(e, axis=-1, keepdims=True)
        pv = jnp.dot(e.astype(BF16), v_ref[:, head_cols(h)], preferred_element_type=F32)
        return (pv * inv).astype(BF16)

    heads = []
    s_next = scores(0)
    for h in range(CROSS_HEADS):
        s_cur = s_next
        if h + 1 < CROSS_HEADS:
            s_next = scores(h + 1)
        heads.append(attend(h, s_cur))
    o = jnp.concatenate(heads, axis=-1)
    o_ref[...] = x1 + jnp.dot(o, wo_ref[...], preferred_element_type=F32)


def _mix_cross(out_a, out_b, x2d, w_out, g_cross, w_cq, kc, vc, w_co, w_up, w_down, seq,
               mem_len):
    n_tok = x2d.shape[0]
    ts = TOK_TILE
    n_steps = n_tok // ts
    per_b = seq // ts
    half = SWA_Q_COLS
    tok = lambda width: pl.BlockSpec((ts, width), lambda i: (i, 0))
    memblk = pl.BlockSpec((mem_len, D_MODEL), lambda i: (i // per_b, 0))
    square = _const_spec((D_MODEL, D_MODEL))
    slab = lambda w: pl.BlockSpec((w.shape[0] // n_steps, w.shape[1]), lambda i: (i, 0))
    return pl.pallas_call(
        _mix_cross_kernel,
        out_shape=[jax.ShapeDtypeStruct(x2d.shape, F32),
                   jax.ShapeDtypeStruct(w_up.shape, BF16),
                   jax.ShapeDtypeStruct(w_down.shape, BF16)],
        grid=(n_steps,),
        in_specs=[tok(half), tok(half), tok(D_MODEL), square, _const_spec((1, D_MODEL)), square,
                  memblk, memblk, square, slab(w_up), slab(w_down)],
        out_specs=[tok(D_MODEL), slab(w_up), slab(w_down)],
        scratch_shapes=[pltpu.VMEM((D_MODEL, D_MODEL), BF16)] * 3,
        compiler_params=_cparams(1),
        name="mix_cross",
    )(out_a, out_b, x2d, w_out, g_cross.reshape(1, D_MODEL), w_cq, kc, vc, w_co, w_up, w_down)


def _mlp_kernel(final_norm, x_ref, g_ref, wu_ref, wd_ref, gf_ref, o_ref):
    d_ff = wu_ref.shape[1]
    sub = x_ref.shape[0] // MLP_SUBTILES
    xs = [x_ref[t * sub:(t + 1) * sub, :] for t in range(MLP_SUBTILES)]
    hbs = [_rms(x, g_ref[...]).astype(BF16) for x in xs]
    accs = list(xs)
    for c in range(d_ff // FF_CHUNK):
        cols = slice(c * FF_CHUNK, (c + 1) * FF_CHUNK)
        for t in range(MLP_SUBTILES):
            u = jnp.dot(hbs[t], wu_ref[:, cols], preferred_element_type=F32)
            r = jnp.maximum(u, 0.0)
            accs[t] = accs[t] + jnp.dot((r * r).astype(BF16), wd_ref[cols, :],
                                        preferred_element_type=F32)
    for t in range(MLP_SUBTILES):
        o_ref[t * sub:(t + 1) * sub, :] = _rms(accs[t], gf_ref[...]) if final_norm else accs[t]


def _mlp(x2, g_mlp, w_up, w_down, g_final, final_norm):
    n_tok = x2.shape[0]
    ts = TOK_TILE
    tok = pl.BlockSpec((ts, D_MODEL), lambda i: (i, 0))
    return pl.pallas_call(
        functools.partial(_mlp_kernel, final_norm),
        out_shape=jax.ShapeDtypeStruct(x2.shape, F32),
        grid=(n_tok // ts,),
        in_specs=[tok, _const_spec((1, D_MODEL)), _const_spec(w_up.shape),
                  _const_spec(w_down.shape), _const_spec((1, D_MODEL))],
        out_specs=tok,
        compiler_params=_cparams(1),
        name="mlp",
    )(x2, g_mlp.reshape(1, D_MODEL), w_up, w_down, g_final.reshape(1, D_MODEL))


def kernel(x, mem, positions, g_mix, w_in, sinks, lambda_q1, lambda_k1, lambda_q2, lambda_k2,
           g_diff, w_out, g_cross, g_mem, w_cq, w_ckv, w_co, g_mlp, w_up, w_down, g_final):
    bsz, seq, _ = x.shape
    mem_len = mem.shape[1]
    depth = w_in.shape[0]
    xf = x.reshape(bsz * seq, D_MODEL)
    memf = mem.reshape(bsz * mem_len, D_MODEL)
    for l in range(depth):
        lam_init = 0.8 - 0.6 * math.exp(-0.3 * l)
        qat, kaw, vat, qdt, kd, vdt = _in_proj(xf, g_mix[l], w_in[l], positions)
        out_a = _swa(qat, kaw, vat, sinks[l], bsz, seq)
        lam_params = jnp.stack([lambda_q1[l], lambda_k1[l], lambda_q2[l], lambda_k2[l]])
        out_b = _diff_attn(qdt, kd, vdt, lam_params, g_diff[l], lam_init, bsz, seq)
        kc, vc = _mem_kv(memf, g_mem[l], w_ckv[l], bsz, mem_len)
        x2, w_up_bf, w_down_bf = _mix_cross(out_a, out_b, xf, w_out[l], g_cross[l], w_cq[l],
                                            kc, vc, w_co[l], w_up[l], w_down[l], seq, mem_len)
        xf = _mlp(x2, g_mlp[l], w_up_bf, w_down_bf, g_final, l == depth - 1)
    return xf.reshape(bsz, seq, D_MODEL)
```

```python
import functools
import math

import jax
import jax.numpy as jnp
from jax import lax
from jax.experimental import pallas as pl
from jax.experimental.pallas import tpu as pltpu

D_MODEL = 1024
HEAD_DIM = 64
SWA_Q_HEADS = 8
SWA_KV_HEADS = 2
SWA_GROUP = SWA_Q_HEADS // SWA_KV_HEADS
WINDOW = 128
DIFF_HEADS = 4
DIFF_VDIM = 2 * HEAD_DIM
DIFF_VPAD = DIFF_VDIM + 16
SWA_Q_COLS = SWA_Q_HEADS * HEAD_DIM
SWA_KV_COLS = SWA_KV_HEADS * HEAD_DIM
DIFF_QK_COLS = DIFF_HEADS * 2 * HEAD_DIM
DIFF_V_COLS = DIFF_HEADS * DIFF_VDIM
CROSS_HEADS = 4
CROSS_HEAD_DIM = D_MODEL // CROSS_HEADS
ROPE_THETA = 10000.0
NORM_EPS = 1e-5

LANES = 128
TOK_TILE = 1024
SWA_TILE = 1024
DIFF_TILE = 512
DIFF_GROUP = 4
DIFF_AHEAD = 3
DIFF_RING = 4
FF_CHUNK = 512
MLP_SUBTILES = 2
SWA_AHEAD = 6
VMEM_LIMIT = 56 * 1024 * 1024
NEG_BIG = -1e30
LOG2E = math.log2(math.e)

BF16 = jnp.bfloat16
F32 = jnp.float32


def _cparams(n_axes):
    return pltpu.CompilerParams(dimension_semantics=("arbitrary",) * n_axes,
                                vmem_limit_bytes=VMEM_LIMIT)


def _rms(x, g):
    ms = jnp.mean(x * x, axis=-1, keepdims=True)
    return x * lax.rsqrt(ms + NORM_EPS) * g


def _const_spec(shape):
    return pl.BlockSpec(shape, lambda *_: (0,) * len(shape), pipeline_mode=pl.Buffered(1))


ROPE_HALF = HEAD_DIM // 2
ROPE_GROUPS = LANES // ROPE_HALF


def _rope_tables(pos, invf, sign):
    pos = pos.astype(F32)
    lane_group = lax.broadcasted_iota(jnp.int32, (pos.shape[0], LANES), 1) // ROPE_HALF
    pos_l = pos[:, ROPE_GROUPS - 1:ROPE_GROUPS]
    for g in range(ROPE_GROUPS - 2, -1, -1):
        pos_l = jnp.where(lane_group == g, pos[:, g:g + 1], pos_l)
    ang = pos_l * invf
    tables = []
    transposed = []
    for tab in (jnp.cos(ang), jnp.sin(ang)):
        rolled = [tab] + [pltpu.roll(tab, k * ROPE_HALF, 1) for k in range(1, ROPE_GROUPS)]
        quarters = []
        for g in range(ROPE_GROUPS):
            shift = (lane_group - g) % ROPE_GROUPS
            tiled = rolled[ROPE_GROUPS - 1]
            for k in range(ROPE_GROUPS - 2, -1, -1):
                tiled = jnp.where(shift == k, rolled[k], tiled)
            quarters.append(tiled)
        tables.append(jnp.concatenate(quarters, axis=0))
        tab_t = tab.T
        transposed.append(jnp.concatenate(
            [tab_t[g * ROPE_HALF:(g + 1) * ROPE_HALF, :] for g in range(ROPE_GROUPS)], axis=1))
    return tables[0], tables[1] * sign, transposed[0], transposed[1]


def _rope_inputs(positions, tile):
    n_tok = positions.size
    inv_freq = ROPE_THETA ** (-jnp.arange(0, HEAD_DIM, 2, dtype=F32) / HEAD_DIM)
    quarter = tile // ROPE_GROUPS
    pos_cols = positions.reshape(n_tok // tile, ROPE_GROUPS, quarter)
    pos_cols = pos_cols.transpose(0, 2, 1).reshape(n_tok // ROPE_GROUPS, ROPE_GROUPS)
    invf = jnp.tile(inv_freq, ROPE_GROUPS).reshape(1, LANES)
    sign = jnp.tile(jnp.repeat(jnp.array([-1.0, 1.0], F32), ROPE_HALF), 2).reshape(1, LANES)
    return pos_cols, invf, sign


def _rope(p, cos, sin):
    width = p.shape[-1]
    lane = lax.broadcasted_iota(jnp.int32, p.shape, 1)
    first_half = (lane % HEAD_DIM) < (HEAD_DIM // 2)
    swapped = jnp.where(first_half,
                        pltpu.roll(p, width - HEAD_DIM // 2, 1),
                        pltpu.roll(p, HEAD_DIM // 2, 1))
    return p * cos + swapped * sin


IN_QA = (0, SWA_Q_COLS)
IN_KA = (IN_QA[1], IN_QA[1] + SWA_KV_COLS)
IN_VA = (IN_KA[1], IN_KA[1] + SWA_KV_COLS)
IN_QD = (IN_VA[1], IN_VA[1] + DIFF_QK_COLS)
IN_KD = (IN_QD[1], IN_QD[1] + DIFF_QK_COLS)
IN_VD = (IN_KD[1], IN_KD[1] + DIFF_V_COLS)
IN_TRANSPOSED = (IN_QA, IN_QD, IN_VA, IN_VD)


def _in_proj_kernel(x_ref, g_ref, win_ref, pos_ref, invf_ref, sign_ref,
                    qat_ref, kaw_ref, vat_ref, qdt_ref, kd_ref, vdt_ref, wk_ref, wt_ref):
    @pl.when(pl.program_id(0) == 0)
    def _prepare_weights():
        wk_ref[:, :SWA_KV_COLS] = win_ref[:, IN_KA[0]:IN_KA[1]].astype(BF16)
        wk_ref[:, SWA_KV_COLS:] = win_ref[:, IN_KD[0]:IN_KD[1]].astype(BF16)
        row = 0
        for c0, c1 in IN_TRANSPOSED:
            wt_ref[row:row + c1 - c0, :] = win_ref[:, c0:c1].T.astype(BF16)
            row += c1 - c0

    hb = _rms(x_ref[...], g_ref[...]).astype(BF16)
    n_tok = hb.shape[0]
    nt_dims = (((1,), (1,)), ((), ()))

    t_rows = [0]
    for c0, c1 in IN_TRANSPOSED:
        t_rows.append(t_rows[-1] + c1 - c0)

    def proj_t(first, last):
        return lax.dot_general(wt_ref[t_rows[first]:t_rows[last], :], hb, nt_dims,
                               preferred_element_type=F32)

    vt = proj_t(2, 4)
    ka = jnp.dot(hb, wk_ref[:, :SWA_KV_COLS], preferred_element_type=F32)
    kd_parts = [jnp.dot(hb, wk_ref[:, SWA_KV_COLS + c * 256:SWA_KV_COLS + (c + 1) * 256],
                        preferred_element_type=F32) for c in range(DIFF_QK_COLS // 256)]
    qat = proj_t(0, 1)
    qdt = proj_t(1, 2)

    cos1, sin1, cos_t, sin_t = _rope_tables(pos_ref[...], invf_ref[...], sign_ref[...])
    cos2 = jnp.concatenate([cos1, cos1], axis=-1)
    sin2 = jnp.concatenate([sin1, sin1], axis=-1)

    vat_ref[...] = vt[:SWA_KV_COLS].astype(BF16)
    pad_row = lax.broadcasted_iota(jnp.int32, (DIFF_VPAD - DIFF_VDIM, n_tok), 0)
    ones_rows = jnp.where(pad_row == 0, 1.0, 0.0).astype(BF16)
    for h in range(DIFF_HEADS):
        src = SWA_KV_COLS + h * DIFF_VDIM
        vdt_ref[h * DIFF_VPAD:h * DIFF_VPAD + DIFF_VDIM, :] = vt[src:src + DIFF_VDIM].astype(BF16)
        vdt_ref[h * DIFF_VPAD + DIFF_VDIM:(h + 1) * DIFF_VPAD, :] = ones_rows

    ka = _rope(ka, cos1, sin1)
    ka_swapped = pltpu.roll(ka, HEAD_DIM, 1)
    low_half = lax.broadcasted_iota(jnp.int32, ka.shape, 1) < HEAD_DIM
    kaw_ref[:, :LANES] = jnp.where(low_half, ka, ka_swapped).astype(BF16)
    kaw_ref[:, LANES:] = jnp.where(low_half, ka_swapped, ka).astype(BF16)
    for c, p in enumerate(kd_parts):
        kd_ref[:, c * 256:(c + 1) * 256] = _rope(p, cos2, sin2).astype(BF16)

    q_scale = HEAD_DIM ** -0.5 * LOG2E
    cos_q = cos_t * q_scale
    sin_q = sin_t * q_scale
    zero_rows = jnp.zeros((HEAD_DIM, n_tok), BF16)

    def store_head(pt, src, out_ref, dst, zero_dst):
        x1 = pt[src:src + ROPE_HALF]
        x2 = pt[src + ROPE_HALF:src + HEAD_DIM]
        out_ref[dst:dst + ROPE_HALF, :] = (x1 * cos_q - x2 * sin_q).astype(BF16)
        out_ref[dst + ROPE_HALF:dst + HEAD_DIM, :] = (x2 * cos_q + x1 * sin_q).astype(BF16)
        out_ref[zero_dst:zero_dst + HEAD_DIM, :] = zero_rows

    for h in range(SWA_Q_HEADS):
        store_head(qat, h * HEAD_DIM, qat_ref, h * LANES, h * LANES + HEAD_DIM)
    for hc in range(2 * DIFF_HEADS):
        c = hc % 2
        store_head(qdt, hc * HEAD_DIM, qdt_ref, hc * LANES + c * HEAD_DIM,
                   hc * LANES + (1 - c) * HEAD_DIM)


def _in_proj(x2d, g_mix, w_in, positions):
    n_tok = x2d.shape[0]
    ts = TOK_TILE
    pos_cols, invf, sign = _rope_inputs(positions, ts)
    n_key_cols = SWA_KV_COLS + DIFF_QK_COLS
    n_t_rows = sum(c1 - c0 for c0, c1 in IN_TRANSPOSED)
    tok = lambda width: pl.BlockSpec((ts, width), lambda i: (i, 0))
    tok_t = lambda rows: pl.BlockSpec((rows, ts), lambda i: (0, i))
    qa_rows = SWA_Q_HEADS * LANES
    kaw_cols = SWA_KV_HEADS * LANES
    qd_rows = DIFF_HEADS * 2 * LANES
    out_shape = [
        jax.ShapeDtypeStruct((qa_rows, n_tok), BF16),
        jax.ShapeDtypeStruct((n_tok, kaw_cols), BF16),
        jax.ShapeDtypeStruct((SWA_KV_COLS, n_tok), BF16),
        jax.ShapeDtypeStruct((qd_rows, n_tok), BF16),
        jax.ShapeDtypeStruct((n_tok, DIFF_QK_COLS), BF16),
        jax.ShapeDtypeStruct((DIFF_HEADS * DIFF_VPAD, n_tok), BF16),
    ]
    return pl.pallas_call(
        _in_proj_kernel,
        out_shape=out_shape,
        grid=(n_tok // ts,),
        in_specs=[tok(D_MODEL), _const_spec((1, D_MODEL)), _const_spec(w_in.shape),
                  pl.BlockSpec((ts // ROPE_GROUPS, ROPE_GROUPS), lambda i: (i, 0)),
                  _const_spec((1, LANES)), _const_spec((1, LANES))],
        out_specs=[tok_t(qa_rows), tok(kaw_cols), tok_t(SWA_KV_COLS), tok_t(qd_rows),
                   tok(DIFF_QK_COLS), tok_t(DIFF_HEADS * DIFF_VPAD)],
        scratch_shapes=[pltpu.VMEM((D_MODEL, n_key_cols), BF16),
                        pltpu.VMEM((n_t_rows, D_MODEL), BF16)],
        compiler_params=_cparams(1),
        name="in_proj",
    )(x2d, g_mix.reshape(1, D_MODEL), w_in, pos_cols, invf, sign)


def _swa_kernel(qt_ref, kc_ref, kp_ref, vc_ref, vp_ref, sink_ref, tri_ref, o_ref):
    n_sub = qt_ref.shape[1] // WINDOW
    kj = lax.broadcasted_iota(jnp.int32, (WINDOW, 2 * WINDOW), 0)
    qi = lax.broadcasted_iota(jnp.int32, (WINDOW, 2 * WINDOW), 1) % WINDOW
    from_prev = kj > qi
    no_prev_bias = jnp.where(pl.program_id(1) == 0, NEG_BIG, 0.0)
    sinks = sink_ref[...] * LOG2E
    pairs = SWA_GROUP // 2
    units = [(r, kv, pair) for r in range(n_sub) for kv in range(SWA_KV_HEADS)
             for pair in range(pairs)]

    def key_rows(r):
        if r == 0:
            return jnp.concatenate([kp_ref[...], kc_ref[0:WINDOW, :]], axis=0)
        return kc_ref[(r - 1) * WINDOW:(r + 1) * WINDOW, :]

    def value_cols(r):
        if r == 0:
            return jnp.concatenate([vp_ref[...], vc_ref[:, 0:WINDOW]], axis=1)
        return vc_ref[:, (r - 1) * WINDOW:(r + 1) * WINDOW]

    def scores(unit):
        r, kv, pair = unit
        head = kv * SWA_GROUP + 2 * pair
        q2 = jnp.concatenate(
            [qt_ref[(head + g) * LANES:(head + g + 1) * LANES, r * WINDOW:(r + 1) * WINDOW]
             for g in range(2)], axis=1)
        kw = key_rows(r)[:, kv * LANES:(kv + 1) * LANES]
        s = jnp.dot(kw, q2, preferred_element_type=F32)
        s_prev = s[:WINDOW] + no_prev_bias if r == 0 else s[:WINDOW]
        return jnp.where(from_prev, s_prev, s[WINDOW:])

    def attend(unit, s):
        r, kv, pair = unit
        tile = kv * pairs + pair
        sink = sinks[tile:tile + 1, :]
        m = jnp.maximum(jnp.max(s, axis=0, keepdims=True), sink)
        e = jnp.exp2(s - m)
        denom = jnp.sum(e, axis=0, keepdims=True) + jnp.exp2(sink - m)
        eb = e.astype(BF16)
        e2 = jnp.concatenate([eb, eb], axis=0) * tri_ref[...]
        vt = value_cols(r)[kv * HEAD_DIM:(kv + 1) * HEAD_DIM, :]
        o = jnp.dot(vt, e2, preferred_element_type=F32) / denom
        return [o[:, :WINDOW], o[:, WINDOW:]]

    pieces = []
    pending = [scores(u) for u in units[:SWA_AHEAD]]
    for idx, unit in enumerate(units):
        if idx + SWA_AHEAD < len(units):
            pending.append(scores(units[idx + SWA_AHEAD]))
        pieces += attend(unit, pending.pop(0))
        if len(pieces) == SWA_Q_HEADS:
            r = unit[0]
            o_ref[r * WINDOW:(r + 1) * WINDOW, :] = (
                jnp.concatenate(pieces, axis=0).T.astype(BF16))
            pieces = []


def _swa(qat, kaw, vat, sinks, bsz, seq):
    ts = SWA_TILE
    per_b = seq // ts
    sub = ts // WINDOW
    prev_blk = lambda b, t: jnp.maximum((b * per_b + t) * sub - 1, 0)
    cur = lambda width: pl.BlockSpec((ts, width), lambda b, t: (b * per_b + t, 0))
    sink_rows = jnp.repeat(sinks.reshape(SWA_Q_HEADS // 2, 2), WINDOW, axis=1)
    kj = lax.broadcasted_iota(jnp.int32, (WINDOW, 2 * WINDOW), 0)
    qi = lax.broadcasted_iota(jnp.int32, (WINDOW, 2 * WINDOW), 1) % WINDOW
    triangles = jnp.concatenate([kj > qi, kj <= qi], axis=0).astype(BF16)
    return pl.pallas_call(
        _swa_kernel,
        out_shape=jax.ShapeDtypeStruct((kaw.shape[0], SWA_Q_COLS), BF16),
        grid=(bsz, per_b),
        in_specs=[pl.BlockSpec((qat.shape[0], ts), lambda b, t: (0, b * per_b + t)),
                  cur(kaw.shape[1]),
                  pl.BlockSpec((WINDOW, kaw.shape[1]), lambda b, t: (prev_blk(b, t), 0)),
                  pl.BlockSpec((SWA_KV_COLS, ts), lambda b, t: (0, b * per_b + t)),
                  pl.BlockSpec((SWA_KV_COLS, WINDOW), lambda b, t: (0, prev_blk(b, t))),
                  _const_spec(sink_rows.shape), _const_spec(triangles.shape)],
        out_specs=cur(SWA_Q_COLS),
        compiler_params=_cparams(2),
        name="swa",
    )(qat, kaw, kaw, vat, vat, sink_rows, triangles)


def _diff_kernel(lam_init, q_ref, k_ref, vt_ref, bias_ref, lamp_ref, g_ref, o_ref,
                 s0_ref, s1_ref, s2_ref, s3_ref, acc_ref, max_ref):
    group_idx = pl.program_id(2)
    th = DIFF_TILE
    top = DIFF_GROUP - 1
    n_groups = k_ref.shape[0] // q_ref.shape[1]
    s_refs = (s0_ref, s1_ref, s2_ref, s3_ref)

    def k_tile(j):
        return k_ref[pl.ds(pl.multiple_of(j * th, th), th), :]

    def vt_tile(j):
        return vt_ref[:, pl.ds(pl.multiple_of(j * th, th), th)]

    hh = th // 2

    def start(unit, slot):
        j, t, c, masked = unit
        s_ref = s_refs[slot]
        qt = q_ref[c * LANES:(c + 1) * LANES, t * th:(t + 1) * th]
        k = k_tile(j)
        if not masked:
            s = jnp.dot(k, qt, preferred_element_type=F32)
            s_ref[...] = s
            return jnp.max(s, axis=0, keepdims=True)
        s_early = jnp.dot(k[:hh], qt, preferred_element_type=F32) + bias_ref[:hh, :]
        s_late = jnp.dot(k[hh:], qt[:, hh:], preferred_element_type=F32) + bias_ref[hh:, hh:]
        s_ref[:hh, :] = s_early
        s_ref[hh:, hh:] = s_late
        max_early = jnp.max(s_early, axis=0, keepdims=True)
        max_late = jnp.max(s_late, axis=0, keepdims=True)
        return jnp.concatenate(
            [max_early[:, :hh], jnp.maximum(max_early[:, hh:], max_late)], axis=1)

    def consume(unit, slot, tile_max):
        j, t, c, masked = unit
        s_ref = s_refs[slot]
        idx = 2 * t + c
        m = max_ref[idx:idx + 1, :]
        m_new = jnp.maximum(m, tile_max)
        alpha = jnp.exp2(m - m_new)
        max_ref[idx:idx + 1, :] = m_new
        vt = vt_tile(j)
        if not masked:
            p = jnp.exp2(s_ref[...] - m_new)
            pv = jnp.dot(vt, p.astype(BF16), preferred_element_type=F32)
            acc_ref[idx] = alpha * acc_ref[idx] + pv
            return
        p_early = jnp.exp2(s_ref[:hh, :] - m_new).astype(BF16)
        p_late = jnp.exp2(s_ref[hh:, hh:] - m_new[:, hh:]).astype(BF16)
        pv_lo = jnp.dot(vt[:, :hh], p_early[:, :hh], preferred_element_type=F32)
        pv_hi = (jnp.dot(vt[:, :hh], p_early[:, hh:], preferred_element_type=F32)
                 + jnp.dot(vt[:, hh:], p_late, preferred_element_type=F32))
        acc_ref[idx, :, :hh] = alpha[:, :hh] * acc_ref[idx, :, :hh] + pv_lo
        acc_ref[idx, :, hh:] = alpha[:, hh:] * acc_ref[idx, :, hh:] + pv_hi

    def finish(t):
        lp = lamp_ref[...]
        lam = (jnp.exp(jnp.sum(lp[0:1] * lp[1:2], axis=-1, keepdims=True))
               - jnp.exp(jnp.sum(lp[2:3] * lp[3:4], axis=-1, keepdims=True)) + lam_init)
        l0 = acc_ref[2 * t, DIFF_VDIM:DIFF_VDIM + 1, :]
        l1 = acc_ref[2 * t + 1, DIFF_VDIM:DIFF_VDIM + 1, :]
        o = (acc_ref[2 * t, :DIFF_VDIM, :] * (1.0 / l0)
             - acc_ref[2 * t + 1, :DIFF_VDIM, :] * (lam / l1))
        ms = jnp.mean(o * o, axis=0, keepdims=True)
        y = o * lax.rsqrt(ms + NORM_EPS) * g_ref[...] * (1.0 - lam_init)
        o_ref[t * th:(t + 1) * th, :] = y.T.astype(BF16)

    n_slots = len(s_refs)

    def run(units, following, pending):
        todo = units + list(following)
        maxes = dict(enumerate(pending))
        started = len(pending)
        for n, unit in enumerate(units):
            while started < len(todo) and started <= n + DIFF_AHEAD:
                maxes[started] = start(todo[started], started % n_slots)
                started += 1
            consume(unit, n % n_slots, maxes.pop(n))
            _, t, c, masked = unit
            if masked and c == 1:
                finish(t)
        return tuple(maxes[len(units) + k] for k in range(len(following)))

    def key_tile_units(j, lowest, lowest_masked):
        return [(j, t, c, lowest_masked and t == lowest)
                for t in range(top, lowest - 1, -1) for c in range(2)]

    def first_units(j):
        return key_tile_units(j, 0, False)[:DIFF_AHEAD]

    pending = tuple(start(u, n) for n, u in enumerate(first_units(0)))
    acc_ref[...] = jnp.zeros_like(acc_ref)
    max_ref[...] = jnp.full(max_ref.shape, NEG_BIG, F32)
    def earlier_group(g, mx):
        units = [u for d in range(DIFF_GROUP)
                 for u in key_tile_units(g * DIFF_GROUP + d, 0, False)]
        return run(units, first_units((g + 1) * DIFF_GROUP), mx)

    if n_groups > 1:
        pending = lax.fori_loop(0, group_idx, earlier_group, pending)
    j0 = DIFF_GROUP * group_idx
    tail = [u for d in range(DIFF_GROUP) for u in key_tile_units(j0 + d, d, True)]
    run(tail, (), pending)


def _diff_attn(qdt, kd, vdt, lam_params, g_diff, lam_init, bsz, seq):
    th = DIFF_TILE
    tq = DIFF_GROUP * th
    nq = seq // tq
    key = lax.broadcasted_iota(jnp.int32, (th, th), 0)
    qry = lax.broadcasted_iota(jnp.int32, (th, th), 1)
    causal_bias = jnp.where(key <= qry, 0.0, NEG_BIG).astype(F32)
    return pl.pallas_call(
        functools.partial(_diff_kernel, lam_init),
        out_shape=jax.ShapeDtypeStruct(kd.shape, BF16),
        grid=(bsz, DIFF_HEADS, nq),
        in_specs=[
            pl.BlockSpec((2 * LANES, tq), lambda b, h, i: (h, b * nq + i)),
            pl.BlockSpec((seq, DIFF_VDIM), lambda b, h, i: (b, h)),
            pl.BlockSpec((DIFF_VPAD, seq), lambda b, h, i: (h, b)),
            _const_spec((th, th)),
            _const_spec((4, HEAD_DIM)),
            _const_spec((DIFF_VDIM, 1)),
        ],
        out_specs=pl.BlockSpec((tq, DIFF_VDIM), lambda b, h, i: (b * nq + i, h)),
        scratch_shapes=[pltpu.VMEM((th, th), F32)] * DIFF_RING + [
                        pltpu.VMEM((2 * DIFF_GROUP, DIFF_VPAD, th), F32),
                        pltpu.VMEM((2 * DIFF_GROUP, th), F32)],
        compiler_params=_cparams(3),
        name="diff_attn",
    )(qdt, kd, vdt, causal_bias, lam_params, g_diff.reshape(DIFF_VDIM, 1))


def _mem_kv_kernel(m_ref, g_ref, wkv_ref, k_ref, v_ref, w_ref):
    @pl.when(pl.program_id(0) == 0)
    def _prepare_weights():
        w_ref[...] = wkv_ref[...].astype(BF16)

    hm = _rms(m_ref[...], g_ref[...]).astype(BF16)
    k_ref[...] = jnp.dot(hm, w_ref[:, :D_MODEL], preferred_element_type=F32).astype(BF16)
    v_ref[...] = jnp.dot(hm, w_ref[:, D_MODEL:], preferred_element_type=F32).astype(BF16)


def _mem_kv(mem2d, g_mem, w_ckv, bsz, mem_len):
    blk = pl.BlockSpec((mem_len, D_MODEL), lambda b: (b, 0))
    return pl.pallas_call(
        _mem_kv_kernel,
        out_shape=[jax.ShapeDtypeStruct(mem2d.shape, BF16)] * 2,
        grid=(bsz,),
        in_specs=[blk, _const_spec((1, D_MODEL)), _const_spec((D_MODEL, 2 * D_MODEL))],
        out_specs=[blk, blk],
        scratch_shapes=[pltpu.VMEM(w_ckv.shape, BF16)],
        compiler_params=_cparams(1),
        name="mem_kv",
    )(mem2d, g_mem.reshape(1, D_MODEL), w_ckv)


def _mix_cross_kernel(a_ref, b_ref, x_ref, wout_ref, g_ref, wcq_ref, k_ref, v_ref, wco_ref,
                      wup_ref, wdown_ref, o_ref, wup_bf_ref, wdown_bf_ref,
                      w_ref, wq_ref, wo_ref):
    @pl.when(pl.program_id(0) == 0)
    def _prepare_weights():
        w_ref[...] = wout_ref[...].astype(BF16)
        wq_ref[...] = wcq_ref[...].astype(BF16)
        wo_ref[...] = wco_ref[...].astype(BF16)

    wup_bf_ref[...] = wup_ref[...].astype(BF16)
    wdown_bf_ref[...] = wdown_ref[...].astype(BF16)

    half = a_ref.shape[1]
    y = (jnp.dot(a_ref[...], w_ref[:half, :], preferred_element_type=F32)
         + jnp.dot(b_ref[...], w_ref[half:, :], preferred_element_type=F32))
    x1 = x_ref[...] + y
    hc = _rms(x1, g_ref[...]).astype(BF16)
    qc = jnp.dot(hc, wq_ref[...], preferred_element_type=F32) * (CROSS_HEAD_DIM ** -0.5 * LOG2E)
    qc = qc.astype(BF16)

    def head_cols(h):
        return slice(h * CROSS_HEAD_DIM, (h + 1) * CROSS_HEAD_DIM)

    def scores(h):
        return lax.dot_general(qc[:, head_cols(h)], k_ref[:, head_cols(h)],
                               (((1,), (1,)), ((), ())), preferred_element_type=F32)

    def attend(h, s):
        m = jnp.max(s, axis=-1, keepdims=True)
        e = jnp.exp2(s - m)
        inv = 1.0 / jnp.sum(e, axis=-1, keepdims=True)
        pv = jnp.dot(e.astype(BF16), v_ref[:, head_cols(h)], preferred_element_type=F32)
        return (pv * inv).astype(BF16)

    heads = []
    s_next = scores(0)
    for h in range(CROSS_HEADS):
        s_cur = s_next
        if h + 1 < CROSS_HEADS:
            s_next = scores(h + 1)
        heads.append(attend(h, s_cur))
    o = jnp.concatenate(heads, axis=-1)
    o_ref[...] = x1 + jnp.dot(o, wo_ref[...], preferred_element_type=F32)


def _mix_cross(out_a, out_b, x2d, w_out, g_cross, w_cq, kc, vc, w_co, w_up, w_down, seq,
               mem_len):
    n_tok = x2d.shape[0]
    ts = TOK_TILE
    n_steps = n_tok // ts
    per_b = seq // ts
    half = SWA_Q_COLS
    tok = lambda width: pl.BlockSpec((ts, width), lambda i: (i, 0))
    memblk = pl.BlockSpec((mem_len, D_MODEL), lambda i: (i // per_b, 0))
    square = _const_spec((D_MODEL, D_MODEL))
    slab = lambda w: pl.BlockSpec((w.shape[0] // n_steps, w.shape[1]), lambda i: (i, 0))
    return pl.pallas_call(
        _mix_cross_kernel,
        out_shape=[jax.ShapeDtypeStruct(x2d.shape, F32),
                   jax.ShapeDtypeStruct(w_up.shape, BF16),
                   jax.ShapeDtypeStruct(w_down.shape, BF16)],
        grid=(n_steps,),
        in_specs=[tok(half), tok(half), tok(D_MODEL), square, _const_spec((1, D_MODEL)), square,
                  memblk, memblk, square, slab(w_up), slab(w_down)],
        out_specs=[tok(D_MODEL), slab(w_up), slab(w_down)],
        scratch_shapes=[pltpu.VMEM((D_MODEL, D_MODEL), BF16)] * 3,
        compiler_params=_cparams(1),
        name="mix_cross",
    )(out_a, out_b, x2d, w_out, g_cross.reshape(1, D_MODEL), w_cq, kc, vc, w_co, w_up, w_down)


def _mlp_kernel(final_norm, x_ref, g_ref, wu_ref, wd_ref, gf_ref, o_ref):
    d_ff = wu_ref.shape[1]
    sub = x_ref.shape[0] // MLP_SUBTILES
    xs = [x_ref[t * sub:(t + 1) * sub, :] for t in range(MLP_SUBTILES)]
    hbs = [_rms(x, g_ref[...]).astype(BF16) for x in xs]
    accs = list(xs)
    for c in range(d_ff // FF_CHUNK):
        cols = slice(c * FF_CHUNK, (c + 1) * FF_CHUNK)
        for t in range(MLP_SUBTILES):
            u = jnp.dot(hbs[t], wu_ref[:, cols], preferred_element_type=F32)
            r = jnp.maximum(u, 0.0)
            accs[t] = accs[t] + jnp.dot((r * r).astype(BF16), wd_ref[cols, :],
                                        preferred_element_type=F32)
    for t in range(MLP_SUBTILES):
        o_ref[t * sub:(t + 1) * sub, :] = _rms(accs[t], gf_ref[...]) if final_norm else accs[t]


def _mlp(x2, g_mlp, w_up, w_down, g_final, final_norm):
    n_tok = x2.shape[0]
    ts = TOK_TILE
    tok = pl.BlockSpec((ts, D_MODEL), lambda i: (i, 0))
    return pl.pallas_call(
        functools.partial(_mlp_kernel, final_norm),
        out_shape=jax.ShapeDtypeStruct(x2.shape, F32),
        grid=(n_tok // ts,),
        in_specs=[tok, _const_spec((1, D_MODEL)), _const_spec(w_up.shape),
                  _const_spec(w_down.shape), _const_spec((1, D_MODEL))],
        out_specs=tok,
        compiler_params=_cparams(1),
        name="mlp",
    )(x2, g_mlp.reshape(1, D_MODEL), w_up, w_down, g_final.reshape(1, D_MODEL))


def kernel(x, mem, positions, g_mix, w_in, sinks, lambda_q1, lambda_k1, lambda_q2, lambda_k2,
           g_diff, w_out, g_cross, g_mem, w_cq, w_ckv, w_co, g_mlp, w_up, w_down, g_final):
    bsz, seq, _ = x.shape
    mem_len = mem.shape[1]
    depth = w_in.shape[0]
    xf = x.reshape(bsz * seq, D_MODEL)
    memf = mem.reshape(bsz * mem_len, D_MODEL)
    for l in range(depth):
        lam_init = 0.8 - 0.6 * math.exp(-0.3 * l)
        qat, kaw, vat, qdt, kd, vdt = _in_proj(xf, g_mix[l], w_in[l], positions)
        out_a = _swa(qat, kaw, vat, sinks[l], bsz, seq)
        lam_params = jnp.stack([lambda_q1[l], lambda_k1[l], lambda_q2[l], lambda_k2[l]])
        out_b = _diff_attn(qdt, kd, vdt, lam_params, g_diff[l], lam_init, bsz, seq)
        kc, vc = _mem_kv(memf, g_mem[l], w_ckv[l], bsz, mem_len)
        x2, w_up_bf, w_down_bf = _mix_cross(out_a, out_b, xf, w_out[l], g_cross[l], w_cq[l],
                                            kc, vc, w_co[l], w_up[l], w_down[l], seq, mem_len)
        xf = _mlp(x2, g_mlp[l], w_up_bf, w_down_bf, g_final, l == depth - 1)
    return xf.reshape(bsz, seq, D_MODEL)
```

```python
import functools
import math

import jax
import jax.numpy as jnp
from jax import lax
from jax.experimental import pallas as pl
from jax.experimental.pallas import tpu as pltpu

D_MODEL = 1024
HEAD_DIM = 64
SWA_Q_HEADS = 8
SWA_KV_HEADS = 2
SWA_GROUP = SWA_Q_HEADS // SWA_KV_HEADS
WINDOW = 128
DIFF_HEADS = 4
DIFF_VDIM = 2 * HEAD_DIM
DIFF_VPAD = DIFF_VDIM + 16
SWA_Q_COLS = SWA_Q_HEADS * HEAD_DIM
SWA_KV_COLS = SWA_KV_HEADS * HEAD_DIM
DIFF_QK_COLS = DIFF_HEADS * 2 * HEAD_DIM
DIFF_V_COLS = DIFF_HEADS * DIFF_VDIM
CROSS_HEADS = 4
CROSS_HEAD_DIM = D_MODEL // CROSS_HEADS
ROPE_THETA = 10000.0
NORM_EPS = 1e-5

LANES = 128
TOK_TILE = 1024
SWA_TILE = 1024
DIFF_TILE = 512
DIFF_GROUP = 4
DIFF_AHEAD = 2
DIFF_RING = 4
FF_CHUNK = 512
CROSS_AHEAD = 2
MLP_SUBTILES = 2
SWA_AHEAD = 6
VMEM_LIMIT = 56 * 1024 * 1024
NEG_BIG = -1e30
LOG2E = math.log2(math.e)

BF16 = jnp.bfloat16
F32 = jnp.float32


def _cparams(n_axes):
    return pltpu.CompilerParams(dimension_semantics=("arbitrary",) * n_axes,
                                vmem_limit_bytes=VMEM_LIMIT)


def _rms(x, g):
    ms = jnp.mean(x * x, axis=-1, keepdims=True)
    return x * lax.rsqrt(ms + NORM_EPS) * g


def _const_spec(shape):
    return pl.BlockSpec(shape, lambda *_: (0,) * len(shape), pipeline_mode=pl.Buffered(1))


ROPE_HALF = HEAD_DIM // 2
ROPE_GROUPS = LANES // ROPE_HALF


def _rope_tables(pos, invf, sign):
    pos = pos.astype(F32)
    lane_group = lax.broadcasted_iota(jnp.int32, (pos.shape[0], LANES), 1) // ROPE_HALF
    pos_l = pos[:, ROPE_GROUPS - 1:ROPE_GROUPS]
    for g in range(ROPE_GROUPS - 2, -1, -1):
        pos_l = jnp.where(lane_group == g, pos[:, g:g + 1], pos_l)
    ang = pos_l * invf
    tables = []
    transposed = []
    for tab in (jnp.cos(ang), jnp.sin(ang)):
        rolled = [tab] + [pltpu.roll(tab, k * ROPE_HALF, 1) for k in range(1, ROPE_GROUPS)]
        quarters = []
        for g in range(ROPE_GROUPS):
            shift = (lane_group - g) % ROPE_GROUPS
            tiled = rolled[ROPE_GROUPS - 1]
            for k in range(ROPE_GROUPS - 2, -1, -1):
                tiled = jnp.where(shift == k, rolled[k], tiled)
            quarters.append(tiled)
        tables.append(jnp.concatenate(quarters, axis=0))
        tab_t = tab.T
        transposed.append(jnp.concatenate(
            [tab_t[g * ROPE_HALF:(g + 1) * ROPE_HALF, :] for g in range(ROPE_GROUPS)], axis=1))
    return tables[0], tables[1] * sign, transposed[0], transposed[1]


def _rope_inputs(positions, tile):
    n_tok = positions.size
    inv_freq = ROPE_THETA ** (-jnp.arange(0, HEAD_DIM, 2, dtype=F32) / HEAD_DIM)
    quarter = tile // ROPE_GROUPS
    pos_cols = positions.reshape(n_tok // tile, ROPE_GROUPS, quarter)
    pos_cols = pos_cols.transpose(0, 2, 1).reshape(n_tok // ROPE_GROUPS, ROPE_GROUPS)
    invf = jnp.tile(inv_freq, ROPE_GROUPS).reshape(1, LANES)
    sign = jnp.tile(jnp.repeat(jnp.array([-1.0, 1.0], F32), ROPE_HALF), 2).reshape(1, LANES)
    return pos_cols, invf, sign


def _rope(p, cos, sin):
    width = p.shape[-1]
    lane = lax.broadcasted_iota(jnp.int32, p.shape, 1)
    first_half = (lane % HEAD_DIM) < (HEAD_DIM // 2)
    swapped = jnp.where(first_half,
                        pltpu.roll(p, width - HEAD_DIM // 2, 1),
                        pltpu.roll(p, HEAD_DIM // 2, 1))
    return p * cos + swapped * sin


IN_QA = (0, SWA_Q_COLS)
IN_KA = (IN_QA[1], IN_QA[1] + SWA_KV_COLS)
IN_VA = (IN_KA[1], IN_KA[1] + SWA_KV_COLS)
IN_QD = (IN_VA[1], IN_VA[1] + DIFF_QK_COLS)
IN_KD = (IN_QD[1], IN_QD[1] + DIFF_QK_COLS)
IN_VD = (IN_KD[1], IN_KD[1] + DIFF_V_COLS)
IN_TRANSPOSED = (IN_QA, IN_QD, IN_VA, IN_VD)


def _in_proj_kernel(x_ref, g_ref, win_ref, pos_ref, invf_ref, sign_ref,
                    qat_ref, kaw_ref, vat_ref, qdt_ref, kd_ref, vdt_ref, wk_ref, wt_ref):
    @pl.when(pl.program_id(0) == 0)
    def _prepare_weights():
        wk_ref[:, :SWA_KV_COLS] = win_ref[:, IN_KA[0]:IN_KA[1]].astype(BF16)
        wk_ref[:, SWA_KV_COLS:] = win_ref[:, IN_KD[0]:IN_KD[1]].astype(BF16)
        row = 0
        for c0, c1 in IN_TRANSPOSED:
            wt_ref[row:row + c1 - c0, :] = win_ref[:, c0:c1].T.astype(BF16)
            row += c1 - c0

    hb = _rms(x_ref[...], g_ref[...]).astype(BF16)
    n_tok = hb.shape[0]
    nt_dims = (((1,), (1,)), ((), ()))

    t_rows = [0]
    for c0, c1 in IN_TRANSPOSED:
        t_rows.append(t_rows[-1] + c1 - c0)

    def proj_t(first, last):
        return lax.dot_general(wt_ref[t_rows[first]:t_rows[last], :], hb, nt_dims,
                               preferred_element_type=F32)

    vt = proj_t(2, 4)
    ka = jnp.dot(hb, wk_ref[:, :SWA_KV_COLS], preferred_element_type=F32)
    kd_parts = [jnp.dot(hb, wk_ref[:, SWA_KV_COLS + c * 256:SWA_KV_COLS + (c + 1) * 256],
                        preferred_element_type=F32) for c in range(DIFF_QK_COLS // 256)]
    qat = proj_t(0, 1)
    qdt = proj_t(1, 2)

    cos1, sin1, cos_t, sin_t = _rope_tables(pos_ref[...], invf_ref[...], sign_ref[...])
    cos2 = jnp.concatenate([cos1, cos1], axis=-1)
    sin2 = jnp.concatenate([sin1, sin1], axis=-1)

    vat_ref[...] = vt[:SWA_KV_COLS].astype(BF16)
    pad_row = lax.broadcasted_iota(jnp.int32, (DIFF_VPAD - DIFF_VDIM, n_tok), 0)
    ones_rows = jnp.where(pad_row == 0, 1.0, 0.0).astype(BF16)
    for h in range(DIFF_HEADS):
        src = SWA_KV_COLS + h * DIFF_VDIM
        vdt_ref[h * DIFF_VPAD:h * DIFF_VPAD + DIFF_VDIM, :] = vt[src:src + DIFF_VDIM].astype(BF16)
        vdt_ref[h * DIFF_VPAD + DIFF_VDIM:(h + 1) * DIFF_VPAD, :] = ones_rows

    ka = _rope(ka, cos1, sin1)
    ka_swapped = pltpu.roll(ka, HEAD_DIM, 1)
    low_half = lax.broadcasted_iota(jnp.int32, ka.shape, 1) < HEAD_DIM
    kaw_ref[:, :LANES] = jnp.where(low_half, ka, ka_swapped).astype(BF16)
    kaw_ref[:, LANES:] = jnp.where(low_half, ka_swapped, ka).astype(BF16)
    for c, p in enumerate(kd_parts):
        kd_ref[:, c * 256:(c + 1) * 256] = _rope(p, cos2, sin2).astype(BF16)

    q_scale = HEAD_DIM ** -0.5 * LOG2E
    cos_q = cos_t * q_scale
    sin_q = sin_t * q_scale
    zero_rows = jnp.zeros((HEAD_DIM, n_tok), BF16)

    def store_head(pt, src, out_ref, dst, zero_dst):
        x1 = pt[src:src + ROPE_HALF]
        x2 = pt[src + ROPE_HALF:src + HEAD_DIM]
        out_ref[dst:dst + ROPE_HALF, :] = (x1 * cos_q - x2 * sin_q).astype(BF16)
        out_ref[dst + ROPE_HALF:dst + HEAD_DIM, :] = (x2 * cos_q + x1 * sin_q).astype(BF16)
        out_ref[zero_dst:zero_dst + HEAD_DIM, :] = zero_rows

    for h in range(SWA_Q_HEADS):
        store_head(qat, h * HEAD_DIM, qat_ref, h * LANES, h * LANES + HEAD_DIM)
    for hc in range(2 * DIFF_HEADS):
        c = hc % 2
        store_head(qdt, hc * HEAD_DIM, qdt_ref, hc * LANES + c * HEAD_DIM,
                   hc * LANES + (1 - c) * HEAD_DIM)


def _in_proj(x2d, g_mix, w_in, positions):
    n_tok = x2d.shape[0]
    ts = TOK_TILE
    pos_cols, invf, sign = _rope_inputs(positions, ts)
    n_key_cols = SWA_KV_COLS + DIFF_QK_COLS
    n_t_rows = sum(c1 - c0 for c0, c1 in IN_TRANSPOSED)
    tok = lambda width: pl.BlockSpec((ts, width), lambda i: (i, 0))
    tok_t = lambda rows: pl.BlockSpec((rows, ts), lambda i: (0, i))
    qa_rows = SWA_Q_HEADS * LANES
    kaw_cols = SWA_KV_HEADS * LANES
    qd_rows = DIFF_HEADS * 2 * LANES
    out_shape = [
        jax.ShapeDtypeStruct((qa_rows, n_tok), BF16),
        jax.ShapeDtypeStruct((n_tok, kaw_cols), BF16),
        jax.ShapeDtypeStruct((SWA_KV_COLS, n_tok), BF16),
        jax.ShapeDtypeStruct((qd_rows, n_tok), BF16),
        jax.ShapeDtypeStruct((n_tok, DIFF_QK_COLS), BF16),
        jax.ShapeDtypeStruct((DIFF_HEADS * DIFF_VPAD, n_tok), BF16),
    ]
    return pl.pallas_call(
        _in_proj_kernel,
        out_shape=out_shape,
        grid=(n_tok // ts,),
        in_specs=[tok(D_MODEL), _const_spec((1, D_MODEL)), _const_spec(w_in.shape),
                  pl.BlockSpec((ts // ROPE_GROUPS, ROPE_GROUPS), lambda i: (i, 0)),
                  _const_spec((1, LANES)), _const_spec((1, LANES))],
        out_specs=[tok_t(qa_rows), tok(kaw_cols), tok_t(SWA_KV_COLS), tok_t(qd_rows),
                   tok(DIFF_QK_COLS), tok_t(DIFF_HEADS * DIFF_VPAD)],
        scratch_shapes=[pltpu.VMEM((D_MODEL, n_key_cols), BF16),
                        pltpu.VMEM((n_t_rows, D_MODEL), BF16)],
        compiler_params=_cparams(1),
        name="in_proj",
    )(x2d, g_mix.reshape(1, D_MODEL), w_in, pos_cols, invf, sign)


def _swa_kernel(qt_ref, kc_ref, kp_ref, vc_ref, vp_ref, sink_ref, tri_ref, o_ref):
    n_sub = qt_ref.shape[1] // WINDOW
    kj = lax.broadcasted_iota(jnp.int32, (WINDOW, 2 * WINDOW), 0)
    qi = lax.broadcasted_iota(jnp.int32, (WINDOW, 2 * WINDOW), 1) % WINDOW
    from_prev = kj > qi
    no_prev_bias = jnp.where(pl.program_id(1) == 0, NEG_BIG, 0.0)
    sinks = sink_ref[...] * LOG2E
    pairs = SWA_GROUP // 2
    units = [(r, kv, pair) for r in range(n_sub) for kv in range(SWA_KV_HEADS)
             for pair in range(pairs)]

    def key_rows(r):
        if r == 0:
            return jnp.concatenate([kp_ref[...], kc_ref[0:WINDOW, :]], axis=0)
        return kc_ref[(r - 1) * WINDOW:(r + 1) * WINDOW, :]

    def value_cols(r):
        if r == 0:
            return jnp.concatenate([vp_ref[...], vc_ref[:, 0:WINDOW]], axis=1)
        return vc_ref[:, (r - 1) * WINDOW:(r + 1) * WINDOW]

    def scores(unit):
        r, kv, pair = unit
        head = kv * SWA_GROUP + 2 * pair
        q2 = jnp.concatenate(
            [qt_ref[(head + g) * LANES:(head + g + 1) * LANES, r * WINDOW:(r + 1) * WINDOW]
             for g in range(2)], axis=1)
        kw = key_rows(r)[:, kv * LANES:(kv + 1) * LANES]
        s = jnp.dot(kw, q2, preferred_element_type=F32)
        s_prev = s[:WINDOW] + no_prev_bias if r == 0 else s[:WINDOW]
        return jnp.where(from_prev, s_prev, s[WINDOW:])

    def attend(unit, s):
        r, kv, pair = unit
        tile = kv * pairs + pair
        sink = sinks[tile:tile + 1, :]
        m = jnp.maximum(jnp.max(s, axis=0, keepdims=True), sink)
        e = jnp.exp2(s - m)
        denom = jnp.sum(e, axis=0, keepdims=True) + jnp.exp2(sink - m)
        eb = e.astype(BF16)
        e2 = jnp.concatenate([eb, eb], axis=0) * tri_ref[...]
        vt = value_cols(r)[kv * HEAD_DIM:(kv + 1) * HEAD_DIM, :]
        o = jnp.dot(vt, e2, preferred_element_type=F32) / denom
        return [o[:, :WINDOW], o[:, WINDOW:]]

    pieces = []
    pending = [scores(u) for u in units[:SWA_AHEAD]]
    for idx, unit in enumerate(units):
        if idx + SWA_AHEAD < len(units):
            pending.append(scores(units[idx + SWA_AHEAD]))
        pieces += attend(unit, pending.pop(0))
        if len(pieces) == SWA_Q_HEADS:
            r = unit[0]
            o_ref[r * WINDOW:(r + 1) * WINDOW, :] = (
                jnp.concatenate(pieces, axis=0).T.astype(BF16))
            pieces = []


def _swa(qat, kaw, vat, sinks, bsz, seq):
    ts = SWA_TILE
    per_b = seq // ts
    sub = ts // WINDOW
    prev_blk = lambda b, t: jnp.maximum((b * per_b + t) * sub - 1, 0)
    cur = lambda width: pl.BlockSpec((ts, width), lambda b, t: (b * per_b + t, 0))
    sink_rows = jnp.repeat(sinks.reshape(SWA_Q_HEADS // 2, 2), WINDOW, axis=1)
    kj = lax.broadcasted_iota(jnp.int32, (WINDOW, 2 * WINDOW), 0)
    qi = lax.broadcasted_iota(jnp.int32, (WINDOW, 2 * WINDOW), 1) % WINDOW
    triangles = jnp.concatenate([kj > qi, kj <= qi], axis=0).astype(BF16)
    return pl.pallas_call(
        _swa_kernel,
        out_shape=jax.ShapeDtypeStruct((kaw.shape[0], SWA_Q_COLS), BF16),
        grid=(bsz, per_b),
        in_specs=[pl.BlockSpec((qat.shape[0], ts), lambda b, t: (0, b * per_b + t)),
                  cur(kaw.shape[1]),
                  pl.BlockSpec((WINDOW, kaw.shape[1]), lambda b, t: (prev_blk(b, t), 0)),
                  pl.BlockSpec((SWA_KV_COLS, ts), lambda b, t: (0, b * per_b + t)),
                  pl.BlockSpec((SWA_KV_COLS, WINDOW), lambda b, t: (0, prev_blk(b, t))),
                  _const_spec(sink_rows.shape), _const_spec(triangles.shape)],
        out_specs=cur(SWA_Q_COLS),
        compiler_params=_cparams(2),
        name="swa",
    )(qat, kaw, kaw, vat, vat, sink_rows, triangles)


def _diff_kernel(lam_init, q_ref, k_ref, vt_ref, bias_ref, lamp_ref, g_ref, o_ref,
                 s0_ref, s1_ref, s2_ref, s3_ref, acc_ref, max_ref):
    group_idx = pl.program_id(2)
    th = DIFF_TILE
    top = DIFF_GROUP - 1
    n_groups = k_ref.shape[0] // q_ref.shape[1]
    s_refs = (s0_ref, s1_ref, s2_ref, s3_ref)

    def k_tile(j):
        return k_ref[pl.ds(pl.multiple_of(j * th, th), th), :]

    def vt_tile(j):
        return vt_ref[:, pl.ds(pl.multiple_of(j * th, th), th)]

    hh = th // 2

    def start(unit, slot):
        j, t, c, masked = unit
        s_ref = s_refs[slot]
        qt = q_ref[c * LANES:(c + 1) * LANES, t * th:(t + 1) * th]
        k = k_tile(j)
        if not masked:
            s = jnp.dot(k, qt, preferred_element_type=F32)
            s_ref[...] = s
            return jnp.max(s, axis=0, keepdims=True)
        s_early = jnp.dot(k[:hh], qt, preferred_element_type=F32) + bias_ref[:hh, :]
        s_late = jnp.dot(k[hh:], qt[:, hh:], preferred_element_type=F32) + bias_ref[hh:, hh:]
        s_ref[:hh, :] = s_early
        s_ref[hh:, hh:] = s_late
        max_early = jnp.max(s_early, axis=0, keepdims=True)
        max_late = jnp.max(s_late, axis=0, keepdims=True)
        return jnp.concatenate(
            [max_early[:, :hh], jnp.maximum(max_early[:, hh:], max_late)], axis=1)

    def consume(unit, slot, tile_max):
        j, t, c, masked = unit
        s_ref = s_refs[slot]
        idx = 2 * t + c
        m = max_ref[idx:idx + 1, :]
        m_new = jnp.maximum(m, tile_max)
        alpha = jnp.exp2(m - m_new)
        max_ref[idx:idx + 1, :] = m_new
        vt = vt_tile(j)
        if not masked:
            p = jnp.exp2(s_ref[...] - m_new)
            pv = jnp.dot(vt, p.astype(BF16), preferred_element_type=F32)
            acc_ref[idx] = alpha * acc_ref[idx] + pv
            return
        p_early = jnp.exp2(s_ref[:hh, :] - m_new).astype(BF16)
        p_late = jnp.exp2(s_ref[hh:, hh:] - m_new[:, hh:]).astype(BF16)
        pv_lo = jnp.dot(vt[:, :hh], p_early[:, :hh], preferred_element_type=F32)
        pv_hi = (jnp.dot(vt[:, :hh], p_early[:, hh:], preferred_element_type=F32)
                 + jnp.dot(vt[:, hh:], p_late, preferred_element_type=F32))
        acc_ref[idx, :, :hh] = alpha[:, :hh] * acc_ref[idx, :, :hh] + pv_lo
        acc_ref[idx, :, hh:] = alpha[:, hh:] * acc_ref[idx, :, hh:] + pv_hi

    def finish(t):
        lp = lamp_ref[...]
        lam = (jnp.exp(jnp.sum(lp[0:1] * lp[1:2], axis=-1, keepdims=True))
               - jnp.exp(jnp.sum(lp[2:3] * lp[3:4], axis=-1, keepdims=True)) + lam_init)
        l0 = acc_ref[2 * t, DIFF_VDIM:DIFF_VDIM + 1, :]
        l1 = acc_ref[2 * t + 1, DIFF_VDIM:DIFF_VDIM + 1, :]
        o = (acc_ref[2 * t, :DIFF_VDIM, :] * (1.0 / l0)
             - acc_ref[2 * t + 1, :DIFF_VDIM, :] * (lam / l1))
        ms = jnp.mean(o * o, axis=0, keepdims=True)
        y = o * lax.rsqrt(ms + NORM_EPS) * g_ref[...] * (1.0 - lam_init)
        o_ref[t * th:(t + 1) * th, :] = y.T.astype(BF16)

    n_slots = len(s_refs)

    def run(units, following, pending):
        todo = units + list(following)
        maxes = dict(enumerate(pending))
        started = len(pending)
        for n, unit in enumerate(units):
            while started < len(todo) and started <= n + DIFF_AHEAD:
                maxes[started] = start(todo[started], started % n_slots)
                started += 1
            consume(unit, n % n_slots, maxes.pop(n))
            _, t, c, masked = unit
            if masked and c == 1:
                finish(t)
        return tuple(maxes[len(units) + k] for k in range(len(following)))

    def key_tile_units(j, lowest, lowest_masked):
        return [(j, t, c, lowest_masked and t == lowest)
                for t in range(top, lowest - 1, -1) for c in range(2)]

    def first_units(j):
        return key_tile_units(j, 0, False)[:DIFF_AHEAD]

    pending = tuple(start(u, n) for n, u in enumerate(first_units(0)))
    acc_ref[...] = jnp.zeros_like(acc_ref)
    max_ref[...] = jnp.full(max_ref.shape, NEG_BIG, F32)
    def earlier_group(g, mx):
        units = [u for d in range(DIFF_GROUP)
                 for u in key_tile_units(g * DIFF_GROUP + d, 0, False)]
        return run(units, first_units((g + 1) * DIFF_GROUP), mx)

    if n_groups > 1:
        pending = lax.fori_loop(0, group_idx, earlier_group, pending)
    j0 = DIFF_GROUP * group_idx
    tail = [u for d in range(DIFF_GROUP) for u in key_tile_units(j0 + d, d, True)]
    run(tail, (), pending)


def _diff_attn(qdt, kd, vdt, lam_params, g_diff, lam_init, bsz, seq):
    th = DIFF_TILE
    tq = DIFF_GROUP * th
    nq = seq // tq
    key = lax.broadcasted_iota(jnp.int32, (th, th), 0)
    qry = lax.broadcasted_iota(jnp.int32, (th, th), 1)
    causal_bias = jnp.where(key <= qry, 0.0, NEG_BIG).astype(F32)
    return pl.pallas_call(
        functools.partial(_diff_kernel, lam_init),
        out_shape=jax.ShapeDtypeStruct(kd.shape, BF16),
        grid=(bsz, DIFF_HEADS, nq),
        in_specs=[
            pl.BlockSpec((2 * LANES, tq), lambda b, h, i: (h, b * nq + i)),
            pl.BlockSpec((seq, DIFF_VDIM), lambda b, h, i: (b, h)),
            pl.BlockSpec((DIFF_VPAD, seq), lambda b, h, i: (h, b)),
            _const_spec((th, th)),
            _const_spec((4, HEAD_DIM)),
            _const_spec((DIFF_VDIM, 1)),
        ],
        out_specs=pl.BlockSpec((tq, DIFF_VDIM), lambda b, h, i: (b * nq + i, h)),
        scratch_shapes=[pltpu.VMEM((th, th), F32)] * DIFF_RING + [
                        pltpu.VMEM((2 * DIFF_GROUP, DIFF_VPAD, th), F32),
                        pltpu.VMEM((2 * DIFF_GROUP, th), F32)],
        compiler_params=_cparams(3),
        name="diff_attn",
    )(qdt, kd, vdt, causal_bias, lam_params, g_diff.reshape(DIFF_VDIM, 1))


def _mem_kv_kernel(m_ref, g_ref, wkv_ref, k_ref, v_ref, w_ref):
    @pl.when(pl.program_id(0) == 0)
    def _prepare_weights():
        w_ref[...] = wkv_ref[...].astype(BF16)

    hm = _rms(m_ref[...], g_ref[...]).astype(BF16)
    k_ref[...] = jnp.dot(hm, w_ref[:, :D_MODEL], preferred_element_type=F32).astype(BF16)
    v_ref[...] = jnp.dot(hm, w_ref[:, D_MODEL:], preferred_element_type=F32).astype(BF16)


def _mem_kv(mem2d, g_mem, w_ckv, bsz, mem_len):
    blk = pl.BlockSpec((mem_len, D_MODEL), lambda b: (b, 0))
    return pl.pallas_call(
        _mem_kv_kernel,
        out_shape=[jax.ShapeDtypeStruct(mem2d.shape, BF16)] * 2,
        grid=(bsz,),
        in_specs=[blk, _const_spec((1, D_MODEL)), _const_spec((D_MODEL, 2 * D_MODEL))],
        out_specs=[blk, blk],
        scratch_shapes=[pltpu.VMEM(w_ckv.shape, BF16)],
        compiler_params=_cparams(1),
        name="mem_kv",
    )(mem2d, g_mem.reshape(1, D_MODEL), w_ckv)


def _mix_cross_kernel(a_ref, b_ref, x_ref, wout_ref, g_ref, wcq_ref, k_ref, v_ref, wco_ref,
                      wup_ref, wdown_ref, o_ref, wup_bf_ref, wdown_bf_ref,
                      w_ref, wq_ref, wo_ref):
    @pl.when(pl.program_id(0) == 0)
    def _prepare_weights():
        w_ref[...] = wout_ref[...].astype(BF16)
        wq_ref[...] = wcq_ref[...].astype(BF16)
        wo_ref[...] = wco_ref[...].astype(BF16)

    wup_bf_ref[...] = wup_ref[...].astype(BF16)
    wdown_bf_ref[...] = wdown_ref[...].astype(BF16)

    half = a_ref.shape[1]
    y = (jnp.dot(a_ref[...], w_ref[:half, :], preferred_element_type=F32)
         + jnp.dot(b_ref[...], w_ref[half:, :], preferred_element_type=F32))
    x1 = x_ref[...] + y
    hc = _rms(x1, g_ref[...]).astype(BF16)
    qc = jnp.dot(hc, wq_ref[...], preferred_element_type=F32) * (CROSS_HEAD_DIM ** -0.5 * LOG2E)
    qc = qc.astype(BF16)

    def head_cols(h):
        return slice(h * CROSS_HEAD_DIM, (h + 1) * CROSS_HEAD_DIM)

    def scores(h):
        return lax.dot_general(qc[:, head_cols(h)], k_ref[:, head_cols(h)],
                               (((1,), (1,)), ((), ())), preferred_element_type=F32)

    def attend(h, s):
        m = jnp.max(s, axis=-1, keepdims=True)
        e = jnp.exp2(s - m)
        inv = 1.0 / jnp.sum(e, axis=-1, keepdims=True)
        pv = jnp.dot(e.astype(BF16), v_ref[:, head_cols(h)], preferred_element_type=F32)
        return (pv * inv).astype(BF16)

    heads = []
    pending = [scores(h) for h in range(CROSS_AHEAD)]
    for h in range(CROSS_HEADS):
        if h + CROSS_AHEAD < CROSS_HEADS:
            pending.append(scores(h + CROSS_AHEAD))
        heads.append(attend(h, pending.pop(0)))
    o = jnp.concatenate(heads, axis=-1)
    o_ref[...] = x1 + jnp.dot(o, wo_ref[...], preferred_element_type=F32)


def _mix_cross(out_a, out_b, x2d, w_out, g_cross, w_cq, kc, vc, w_co, w_up, w_down, seq,
               mem_len):
    n_tok = x2d.shape[0]
    ts = TOK_TILE
    n_steps = n_tok // ts
    per_b = seq // ts
    half = SWA_Q_COLS
    tok = lambda width: pl.BlockSpec((ts, width), lambda i: (i, 0))
    memblk = pl.BlockSpec((mem_len, D_MODEL), lambda i: (i // per_b, 0))
    square = _const_spec((D_MODEL, D_MODEL))
    slab = lambda w: pl.BlockSpec((w.shape[0] // n_steps, w.shape[1]), lambda i: (i, 0))
    return pl.pallas_call(
        _mix_cross_kernel,
        out_shape=[jax.ShapeDtypeStruct(x2d.shape, F32),
                   jax.ShapeDtypeStruct(w_up.shape, BF16),
                   jax.ShapeDtypeStruct(w_down.shape, BF16)],
        grid=(n_steps,),
        in_specs=[tok(half), tok(half), tok(D_MODEL), square, _const_spec((1, D_MODEL)), square,
                  memblk, memblk, square, slab(w_up), slab(w_down)],
        out_specs=[tok(D_MODEL), slab(w_up), slab(w_down)],
        scratch_shapes=[pltpu.VMEM((D_MODEL, D_MODEL), BF16)] * 3,
        compiler_params=_cparams(1),
        name="mix_cross",
    )(out_a, out_b, x2d, w_out, g_cross.reshape(1, D_MODEL), w_cq, kc, vc, w_co, w_up, w_down)


def _mlp_kernel(final_norm, x_ref, g_ref, wu_ref, wd_ref, gf_ref, o_ref):
    d_ff = wu_ref.shape[1]
    sub = x_ref.shape[0] // MLP_SUBTILES
    xs = [x_ref[t * sub:(t + 1) * sub, :] for t in range(MLP_SUBTILES)]
    hbs = [_rms(x, g_ref[...]).astype(BF16) for x in xs]
    accs = list(xs)
    for c in range(d_ff // FF_CHUNK):
        cols = slice(c * FF_CHUNK, (c + 1) * FF_CHUNK)
        for t in range(MLP_SUBTILES):
            u = jnp.dot(hbs[t], wu_ref[:, cols], preferred_element_type=F32)
            r = jnp.maximum(u, 0.0)
            accs[t] = accs[t] + jnp.dot((r * r).astype(BF16), wd_ref[cols, :],
                                        preferred_element_type=F32)
    for t in range(MLP_SUBTILES):
        o_ref[t * sub:(t + 1) * sub, :] = _rms(accs[t], gf_ref[...]) if final_norm else accs[t]


def _mlp(x2, g_mlp, w_up, w_down, g_final, final_norm):
    n_tok = x2.shape[0]
    ts = TOK_TILE
    tok = pl.BlockSpec((ts, D_MODEL), lambda i: (i, 0))
    return pl.pallas_call(
        functools.partial(_mlp_kernel, final_norm),
        out_shape=jax.ShapeDtypeStruct(x2.shape, F32),
        grid=(n_tok // ts,),
        in_specs=[tok, _const_spec((1, D_MODEL)), _const_spec(w_up.shape),
                  _const_spec(w_down.shape), _const_spec((1, D_MODEL))],
        out_specs=tok,
        compiler_params=_cparams(1),
        name="mlp",
    )(x2, g_mlp.reshape(1, D_MODEL), w_up, w_down, g_final.reshape(1, D_MODEL))


def kernel(x, mem, positions, g_mix, w_in, sinks, lambda_q1, lambda_k1, lambda_q2, lambda_k2,
           g_diff, w_out, g_cross, g_mem, w_cq, w_ckv, w_co, g_mlp, w_up, w_down, g_final):
    bsz, seq, _ = x.shape
    mem_len = mem.shape[1]
    depth = w_in.shape[0]
    xf = x.reshape(bsz * seq, D_MODEL)
    memf = mem.reshape(bsz * mem_len, D_MODEL)
    for l in range(depth):
        lam_init = 0.8 - 0.6 * math.exp(-0.3 * l)
        qat, kaw, vat, qdt, kd, vdt = _in_proj(xf, g_mix[l], w_in[l], positions)
        out_a = _swa(qat, kaw, vat, sinks[l], bsz, seq)
        lam_params = jnp.stack([lambda_q1[l], lambda_k1[l], lambda_q2[l], lambda_k2[l]])
        out_b = _diff_attn(qdt, kd, vdt, lam_params, g_diff[l], lam_init, bsz, seq)
        kc, vc = _mem_kv(memf, g_mem[l], w_ckv[l], bsz, mem_len)
        x2, w_up_bf, w_down_bf = _mix_cross(out_a, out_b, xf, w_out[l], g_cross[l], w_cq[l],
                                            kc, vc, w_co[l], w_up[l], w_down[l], seq, mem_len)
        xf = _mlp(x2, g_mlp[l], w_up_bf, w_down_bf, g_final, l == depth - 1)
    return xf.reshape(bsz, seq, D_MODEL)
```
